```python
import math
import jax, jax.numpy as jnp
from jax import lax
import numpy as np


D_MODEL = 2048
BATCH = 2
SEQ = 4096
DEPTH = 4
DEC_BATCH = 8
DEC_SEQ = 1
PAST_LEN = 16384
PAGE_SIZE = 128

HEAD_DIM = 128
N_A_LAYERS = DEPTH // 2
N_B_LAYERS = DEPTH - N_A_LAYERS
DIL_WINDOWS = (128, 512, 2048)
DIL_RATES = (1, 4, 16)
N_GROUPS = 3
A_HEADS_PER_GROUP = 5
A_HEADS = N_GROUPS * A_HEADS_PER_GROUP
B_HEADS = D_MODEL // HEAD_DIM
D_FF = ((8 * D_MODEL // 3 + 255) // 256) * 256
BLOCK = 128
EPS = 1e-6
N_MOD = 9
SB_BIAS_INIT = -7.0

kernel_name = 'yoco_dilated_stickbreak_decoder_step'


def rms_norm(x, g):
    xf = x.astype(jnp.float32)
    y = xf * lax.rsqrt(jnp.mean(xf * xf, axis=-1, keepdims=True) + EPS)
    return (y * g.astype(jnp.float32)).astype(x.dtype)


def modulate(x, shift, scale):
    return x * (1 + scale[:, None, :]) + shift[:, None, :]


def swiglu(x, w_in, w_out):
    gate, up = jnp.split(x @ w_in, 2, axis=-1)
    return (jax.nn.silu(gate) * up) @ w_out


def alibi_slopes(n):
    return 2.0 ** (-8.0 * jnp.arange(1, n + 1, dtype=jnp.float32) / n)


def a_qkv(u, w_qkv, q_norm, k_norm):
    b, s, _ = u.shape
    qkv = (u @ w_qkv).reshape(b, s, 3, A_HEADS, HEAD_DIM)
    return rms_norm(qkv[:, :, 0], q_norm), rms_norm(qkv[:, :, 1], k_norm), qkv[:, :, 2]


def dilated_group_prompt(q, k, v, dil, n_back, slopes):
    b, s, h, d = q.shape
    span = dil * BLOCK
    s_pad = -(-s // span) * span
    m_len = s_pad // dil
    nb = m_len // BLOCK

    def to_blocks(a):
        a = jnp.pad(a, ((0, 0), (0, s_pad - s), (0, 0), (0, 0)))
        a = a.reshape(b, m_len, dil, h, d).transpose(0, 2, 1, 3, 4)
        return a.reshape(b, dil, nb, BLOCK, h, d)

    def with_prev(a):
        prev = jnp.concatenate([jnp.zeros_like(a[:, :, :1]), a[:, :, :-1]], axis=2)
        return jnp.concatenate([prev, a], axis=3)

    qb = to_blocks(q).astype(jnp.float32)
    kk = with_prev(to_blocks(k)).astype(jnp.float32)
    vv = with_prev(to_blocks(v)).astype(jnp.float32)
    scores = jnp.einsum('brnqhd,brnkhd->brnhqk', qb, kk) / math.sqrt(d)
    qi = jnp.arange(BLOCK)[:, None]
    ki = jnp.arange(2 * BLOCK)[None, :] - BLOCK
    steps = qi - ki
    key_m = jnp.arange(nb)[:, None, None] * BLOCK + ki[None]
    valid = (steps >= 0) & (steps <= n_back) & (key_m >= 0)
    bias = -slopes[:, None, None] * (steps * dil).astype(jnp.float32)[None]
    scores = jnp.where(valid[None, None, :, None], scores + bias, -jnp.inf)
    lse = jax.nn.logsumexp(scores, axis=-1)
    probs = jnp.exp(scores - lse[..., None])
    out = jnp.einsum('brnhqk,brnkhd->brnqhd', probs, vv)
    out = out.reshape(b, dil, m_len, h, d).transpose(0, 2, 1, 3, 4).reshape(b, s_pad, h, d)[:, :s]
    lse = lse.transpose(0, 1, 2, 4, 3).reshape(b, dil, m_len, h).transpose(0, 2, 1, 3).reshape(b, s_pad, h)[:, :s]
    return out, lse


def dilated_group_sample(q, k_ext, v_ext, dil, n_back, slopes, buf_len):
    ds, d = q.shape[1], q.shape[-1]
    steps = jnp.arange(n_back + 1)
    idx = buf_len + jnp.arange(ds)[:, None] - steps[None, :] * dil
    valid = idx >= 0
    idx = jnp.maximum(idx, 0)
    kg = jnp.take(k_ext, idx, axis=1).astype(jnp.float32)
    vg = jnp.take(v_ext, idx, axis=1).astype(jnp.float32)
    scores = jnp.einsum('bqhd,bqkhd->bhqk', q.astype(jnp.float32), kg) / math.sqrt(d)
    bias = -slopes[:, None, None] * (steps * dil).astype(jnp.float32)[None, None, :]
    scores = jnp.where(valid[None, None], scores + bias, -jnp.inf)
    lse = jax.nn.logsumexp(scores, axis=-1)
    probs = jnp.exp(scores - lse[..., None])
    out = jnp.einsum('bhqk,bqkhd->bqhd', probs, vg)
    return out, lse.transpose(0, 2, 1)


def combine_groups(outs, lses, dtype):
    group_lse = jnp.stack([jax.nn.logsumexp(l, axis=-1) - math.log(l.shape[-1]) for l in lses], axis=-1)
    alpha = jax.nn.softmax(group_lse, axis=-1)
    mixed = jnp.concatenate([o * alpha[..., g, None, None] for g, o in enumerate(outs)], axis=2)
    return mixed.reshape(mixed.shape[0], mixed.shape[1], -1).astype(dtype)


def dilated_mixer_prompt(u, w_qkv, q_norm, k_norm, w_o, slopes):
    s = u.shape[1]
    q, k, v = a_qkv(u, w_qkv, q_norm, k_norm)
    outs, lses, wins = [], [], []
    for g in range(N_GROUPS):
        hs = slice(g * A_HEADS_PER_GROUP, (g + 1) * A_HEADS_PER_GROUP)
        o, lse = dilated_group_prompt(q[:, :, hs], k[:, :, hs], v[:, :, hs], DIL_RATES[g],
                                      DIL_WINDOWS[g] // DIL_RATES[g], slopes[hs])
        outs.append(o)
        lses.append(lse)
        keep = min(DIL_WINDOWS[g], s)
        wins.append(jnp.stack([k[:, :, hs], v[:, :, hs]], axis=2)[:, s - keep:])
    return combine_groups(outs, lses, u.dtype) @ w_o, wins


def dilated_mixer_sample(u, bufs, w_qkv, q_norm, k_norm, w_o, slopes):
    ds = u.shape[1]
    q, k, v = a_qkv(u, w_qkv, q_norm, k_norm)
    outs, lses, wins = [], [], []
    for g in range(N_GROUPS):
        hs = slice(g * A_HEADS_PER_GROUP, (g + 1) * A_HEADS_PER_GROUP)
        buf = bufs[g]
        buf_len = buf.shape[1]
        new = jnp.stack([k[:, :, hs], v[:, :, hs]], axis=2).astype(buf.dtype)
        ext = jnp.concatenate([buf, new], axis=1)
        o, lse = dilated_group_sample(q[:, :, hs], ext[:, :, 0], ext[:, :, 1], DIL_RATES[g],
                                      DIL_WINDOWS[g] // DIL_RATES[g], slopes[hs], buf_len)
        outs.append(o)
        lses.append(lse)
        keep = min(DIL_WINDOWS[g], buf_len + ds)
        wins.append(ext[:, buf_len + ds - keep:])
    return combine_groups(outs, lses, u.dtype) @ w_o, wins


def stick_break(q, k, v, q_pos, k_pos, sb_bias):
    d = q.shape[-1]
    z = jnp.einsum('bqhd,bkhd->bhqk', q.astype(jnp.float32), k.astype(jnp.float32)) / math.sqrt(d)
    z = z + sb_bias.astype(jnp.float32)[None, :, None, None]
    causal = k_pos[None, :] < q_pos[:, None]
    log_stay = jnp.where(causal, jax.nn.log_sigmoid(-z), 0.0)
    later = lax.cumsum(log_stay, axis=3, reverse=True) - log_stay
    weights = jnp.where(causal, jnp.exp(jax.nn.log_sigmoid(z) + later), 0.0)
    return jnp.einsum('bhqk,bkhd->bqhd', weights, v.astype(jnp.float32))


def sb_mixer_prompt(u, k, v, w_q, q_norm, sb_bias, w_o):
    b, s, _ = u.shape
    nq = s // BLOCK
    q = rms_norm((u @ w_q).reshape(b, s, B_HEADS, HEAD_DIM), q_norm)
    q = q.reshape(b, nq, BLOCK, B_HEADS, HEAD_DIM).transpose(1, 0, 2, 3, 4)
    k_pos = jnp.arange(s)

    def one_block(args):
        q_blk, j = args
        return stick_break(q_blk, k, v, j * BLOCK + jnp.arange(BLOCK), k_pos, sb_bias)

    o = lax.map(one_block, (q, jnp.arange(nq)))
    o = o.transpose(1, 0, 2, 3, 4).reshape(b, s, B_HEADS * HEAD_DIM)
    return o.astype(u.dtype) @ w_o, None


def sb_mixer_sample(u, k, v, past_len, w_q, q_norm, sb_bias, w_o):
    b, s, _ = u.shape
    q = rms_norm((u @ w_q).reshape(b, s, B_HEADS, HEAD_DIM), q_norm)
    o = stick_break(q, k, v, past_len + jnp.arange(s), jnp.arange(k.shape[1]), sb_bias)
    return o.reshape(b, s, B_HEADS * HEAD_DIM).astype(u.dtype) @ w_o, None


def shared_kv(h, c, kv_norm, w_mod_kv, b_mod_kv, w_kv, sb_k_norm):
    shift, scale = jnp.split(jax.nn.silu(c) @ w_mod_kv + b_mod_kv, 2, axis=-1)
    u = modulate(rms_norm(h, kv_norm), shift, scale)
    b, s, _ = h.shape
    kv = (u @ w_kv).reshape(b, s, 2, B_HEADS, HEAD_DIM)
    return jnp.stack([rms_norm(kv[:, :, 0], sb_k_norm), kv[:, :, 1]], axis=2)


def layer(h, c, mixer, w_mod_l, b_mod_l, norm_l, ffn_in_l, ffn_out_l):
    sh1, sc1, g1, sh2, sc2, g2, sh3, sc3, g3 = jnp.split(jax.nn.silu(c) @ w_mod_l + b_mod_l, N_MOD, axis=-1)
    h = h + 0.5 * g1[:, None] * swiglu(modulate(rms_norm(h, norm_l[0]), sh1, sc1), ffn_in_l[0], ffn_out_l[0])
    mix, st = mixer(modulate(rms_norm(h, norm_l[1]), sh2, sc2))
    h = h + g2[:, None] * mix
    h = h + 0.5 * g3[:, None] * swiglu(modulate(rms_norm(h, norm_l[2]), sh3, sc3), ffn_in_l[1], ffn_out_l[1])
    return h, st


def setup_inputs(seed: int = 0) -> dict:
    key = jax.random.key(seed)
    ks = iter(jax.random.split(key, 40))

    def nrm(shape, scale=1.0):
        return jax.random.normal(next(ks), shape, jnp.float32) * scale

    n_pages = PAST_LEN // PAGE_SIZE
    in_use = DEC_BATCH * n_pages
    n_phys = in_use + max(1, in_use // 4)
    inp = {}
    inp['x_prompt'] = nrm((BATCH, SEQ, D_MODEL))
    inp['x_sample'] = nrm((DEC_BATCH, DEC_SEQ, D_MODEL))
    for g in range(N_GROUPS):
        inp['cache_win_g%d' % g] = nrm((N_A_LAYERS, DEC_BATCH, min(DIL_WINDOWS[g], PAST_LEN), 2,
                                        A_HEADS_PER_GROUP, HEAD_DIM))
    inp['cache_kv'] = nrm((n_phys, PAGE_SIZE, 2, B_HEADS, HEAD_DIM))
    perm = jax.random.permutation(next(ks), n_phys)[:in_use]
    inp['page_table'] = perm.reshape(DEC_BATCH, n_pages).astype(jnp.int32)
    inp['c_prompt'] = nrm((BATCH, D_MODEL))
    inp['c_sample'] = nrm((DEC_BATCH, D_MODEL))
    inp['w_mod'] = nrm((DEPTH, D_MODEL, N_MOD * D_MODEL), D_MODEL ** -0.5)
    inp['b_mod'] = nrm((DEPTH, N_MOD * D_MODEL), 0.02)
    inp['norm_g'] = 1.0 + nrm((DEPTH, 3, D_MODEL), 0.02)
    inp['ffn_w_in'] = nrm((DEPTH, 2, D_MODEL, 2 * D_FF), D_MODEL ** -0.5)
    inp['ffn_w_out'] = nrm((DEPTH, 2, D_FF, D_MODEL), D_FF ** -0.5)
    inp['a_w_qkv'] = nrm((N_A_LAYERS, D_MODEL, 3 * A_HEADS * HEAD_DIM), D_MODEL ** -0.5)
    inp['a_q_norm'] = 1.0 + nrm((N_A_LAYERS, HEAD_DIM), 0.02)
    inp['a_k_norm'] = 1.0 + nrm((N_A_LAYERS, HEAD_DIM), 0.02)
    inp['a_w_o'] = nrm((N_A_LAYERS, A_HEADS * HEAD_DIM, D_MODEL), (A_HEADS * HEAD_DIM) ** -0.5)
    inp['kv_norm'] = 1.0 + nrm((D_MODEL,), 0.02)
    inp['w_mod_kv'] = nrm((D_MODEL, 2 * D_MODEL), D_MODEL ** -0.5)
    inp['b_mod_kv'] = nrm((2 * D_MODEL,), 0.02)
    inp['w_kv'] = nrm((D_MODEL, 2 * B_HEADS * HEAD_DIM), D_MODEL ** -0.5)
    inp['sb_k_norm'] = 1.0 + nrm((HEAD_DIM,), 0.02)
    inp['b_w_q'] = nrm((N_B_LAYERS, D_MODEL, B_HEADS * HEAD_DIM), D_MODEL ** -0.5)
    inp['b_q_norm'] = 1.0 + nrm((N_B_LAYERS, HEAD_DIM), 0.02)
    inp['b_sb_bias'] = SB_BIAS_INIT + nrm((N_B_LAYERS, B_HEADS), 0.5)
    inp['b_w_o'] = nrm((N_B_LAYERS, B_HEADS * HEAD_DIM, D_MODEL), (B_HEADS * HEAD_DIM) ** -0.5)
    return inp


def reference(x_prompt, x_sample, cache_win_g0, cache_win_g1, cache_win_g2, cache_kv, page_table,
              c_prompt, c_sample, w_mod, b_mod, norm_g, ffn_w_in, ffn_w_out, a_w_qkv, a_q_norm,
              a_k_norm, a_w_o, kv_norm, w_mod_kv, b_mod_kv, w_kv, sb_k_norm, b_w_q, b_q_norm,
              b_sb_bias, b_w_o):
    slopes = alibi_slopes(A_HEADS)
    win_caches = (cache_win_g0, cache_win_g1, cache_win_g2)
    h_p, h_s = x_prompt, x_sample
    win_p = [[] for _ in range(N_GROUPS)]
    win_s = [[] for _ in range(N_GROUPS)]
    kv_p = kv_s = None
    k_p = v_p = k_s = v_s = None
    past_len = page_table.shape[1] * cache_kv.shape[1]
    for l in range(DEPTH):
        common = (w_mod[l], b_mod[l], norm_g[l], ffn_w_in[l], ffn_w_out[l])
        if l < N_A_LAYERS:
            a_par = (a_w_qkv[l], a_q_norm[l], a_k_norm[l], a_w_o[l], slopes)
            bufs = [wc[l] for wc in win_caches]
            h_p, st_p = layer(h_p, c_prompt, lambda u: dilated_mixer_prompt(u, *a_par), *common)
            h_s, st_s = layer(h_s, c_sample, lambda u: dilated_mixer_sample(u, bufs, *a_par), *common)
            for g in range(N_GROUPS):
                win_p[g].append(st_p[g])
                win_s[g].append(st_s[g])
            if l == N_A_LAYERS - 1:
                kv_p = shared_kv(h_p, c_prompt, kv_norm, w_mod_kv, b_mod_kv, w_kv, sb_k_norm)
                kv_s = shared_kv(h_s, c_sample, kv_norm, w_mod_kv, b_mod_kv, w_kv, sb_k_norm)
                past = cache_kv[page_table]
                past = past.reshape(past.shape[0], past_len, 2, B_HEADS, HEAD_DIM)
                kv_all = jnp.concatenate([past, kv_s.astype(past.dtype)], axis=1)
                k_p, v_p = kv_p[:, :, 0], kv_p[:, :, 1]
                k_s, v_s = kv_all[:, :, 0], kv_all[:, :, 1]
        else:
            j = l - N_A_LAYERS
            b_par = (b_w_q[j], b_q_norm[j], b_sb_bias[j], b_w_o[j])
            h_p, _ = layer(h_p, c_prompt, lambda u: sb_mixer_prompt(u, k_p, v_p, *b_par), *common)
            h_s, _ = layer(h_s, c_sample, lambda u: sb_mixer_sample(u, k_s, v_s, past_len, *b_par), *common)
    win_p_g0, win_p_g1, win_p_g2 = [jnp.stack(w, axis=0) for w in win_p]
    win_s_g0, win_s_g1, win_s_g2 = [jnp.stack(w, axis=0) for w in win_s]
    return (h_p, h_s, win_p_g0, win_p_g1, win_p_g2, kv_p, win_s_g0, win_s_g1, win_s_g2, kv_s)
```

```python
import functools
import math

import numpy as np
import jax
import jax.numpy as jnp
from jax import lax
from jax.experimental import pallas as pl
from jax.experimental.pallas import tpu as pltpu

F32 = jnp.float32
BF16 = jnp.bfloat16

EPS = 1e-6
HEAD_DIM = 128
LANES = 128
N_MOD = 9
DIL_WINDOWS = (128, 512, 2048)
DIL_RATES = (1, 4, 16)
N_GROUPS = 3
GROUP_HEADS = 5
A_HEADS = N_GROUPS * GROUP_HEADS
A_BLOCK = 128
GROUP_COLS = GROUP_HEADS * HEAD_DIM
QKV_COLS = 3 * A_HEADS * HEAD_DIM
NEG_BIG = -1e30
MIB = 1024 * 1024
MOD_ROWS = 16


def _alibi_slopes():
    return [float(2.0 ** (-8.0 * (i + 1) / A_HEADS)) for i in range(A_HEADS)]


def _params(semantics, vmem_mib):
    return pltpu.CompilerParams(dimension_semantics=semantics, vmem_limit_bytes=vmem_mib * MIB)


def _pick_tile(n, preferred):
    if n <= preferred:
        return n
    t = (preferred // LANES) * LANES
    while t >= LANES:
        if n % t == 0:
            return t
        t -= LANES
    return n


class _Rows:
    def __init__(self, n_rows, groups, cond_rows, tm):
        self.n_rows = n_rows
        self.groups = groups
        self.cond_rows = cond_rows
        self.tm = tm
        self.tiles_per_group = (n_rows // groups) // tm
        assert self.tiles_per_group * tm * groups == n_rows

    def cond_spec(self, chunk, width, col_of=None):
        g, tpg = self.groups, self.tiles_per_group
        if col_of is None:
            return pl.BlockSpec((1, self.cond_rows, width), lambda i, j: (chunk * g + i // tpg, 0, 0))
        return pl.BlockSpec((1, self.cond_rows, width), lambda i, j: (chunk * g + i // tpg, 0, col_of(j)))


def _norm_mod(x, gain, shift, scale):
    ms = jnp.mean(x * x, axis=-1, keepdims=True)
    y = x * lax.rsqrt(ms + EPS) * gain
    return y * (1.0 + scale) + shift


def _mod_kernel(c_ref, w_ref, b_ref, o_ref):
    c = c_ref[...]
    a = (c * jax.nn.sigmoid(c)).astype(BF16)
    o_ref[0, 0] = jnp.dot(a, w_ref[0].astype(BF16), preferred_element_type=F32) + b_ref[0]


def _modulation(c_all, w, b):
    n_layers, d, n_out = w.shape
    n_chunks = n_out // d
    tn = _pick_tile(d, 1024)
    per_chunk = d // tn
    return pl.pallas_call(
        _mod_kernel,
        grid=(n_layers, n_out // tn),
        in_specs=[
            pl.BlockSpec((MOD_ROWS, d), lambda l, j: (0, 0)),
            pl.BlockSpec((1, d, tn), lambda l, j: (l, 0, j)),
            pl.BlockSpec((1, 1, tn), lambda l, j: (l, 0, j)),
        ],
        out_specs=pl.BlockSpec((1, 1, MOD_ROWS, tn), lambda l, j: (l, j // per_chunk, 0, j % per_chunk)),
        out_shape=jax.ShapeDtypeStruct((n_layers, n_chunks, MOD_ROWS, d), F32),
        compiler_params=_params(("arbitrary", "arbitrary"), 40),
        name="modulation",
    )(c_all, w, b.reshape(n_layers, 1, n_out))


def _ffn_kernel(x_ref, sh_ref, sc_ref, gt_ref, ng_ref, wg_ref, wu_ref, wo_ref, o_ref, xn_ref):
    f = pl.program_id(1)

    @pl.when(f == 0)
    def _():
        u = _norm_mod(x_ref[...], ng_ref[0], sh_ref[0], sc_ref[0])
        xn_ref[...] = u.astype(BF16)
        o_ref[...] = jnp.zeros_like(o_ref)

    xn = xn_ref[...]
    gate = jnp.dot(xn, wg_ref[...].astype(BF16), preferred_element_type=F32)
    up = jnp.dot(xn, wu_ref[...].astype(BF16), preferred_element_type=F32)
    act = (gate * jax.nn.sigmoid(gate) * up).astype(BF16)
    o_ref[...] += jnp.dot(act, wo_ref[...].astype(BF16), preferred_element_type=F32)

    @pl.when(f == pl.num_programs(1) - 1)
    def _():
        o_ref[...] = x_ref[...] + 0.5 * gt_ref[0] * o_ref[...]


def _ffn(h, rows, mod, chunk0, norm_g, norm_idx, w_in, w_out, w_idx, tf):
    m, d = h.shape
    d_ff = w_out.shape[1]
    n_f = d_ff // tf
    tm = rows.tm
    return pl.pallas_call(
        _ffn_kernel,
        grid=(m // tm, n_f),
        in_specs=[
            pl.BlockSpec((tm, d), lambda i, f: (i, 0)),
            rows.cond_spec(chunk0, d),
            rows.cond_spec(chunk0 + 1, d),
            rows.cond_spec(chunk0 + 2, d),
            pl.BlockSpec((1, 1, d), lambda i, f: (norm_idx, 0, 0)),
            pl.BlockSpec((None, d, tf), lambda i, f: (w_idx, 0, f)),
            pl.BlockSpec((None, d, tf), lambda i, f: (w_idx, 0, n_f + f)),
            pl.BlockSpec((None, tf, d), lambda i, f: (w_idx, f, 0)),
        ],
        out_specs=pl.BlockSpec((tm, d), lambda i, f: (i, 0)),
        out_shape=jax.ShapeDtypeStruct((m, d), F32),
        scratch_shapes=[pltpu.VMEM((tm, d), BF16)],
        compiler_params=_params(("arbitrary", "arbitrary"), 58),
        name="ffn",
    )(h, mod, mod, mod, norm_g, w_in, w_in, w_out)


def _proj_kernel(x_ref, sh_ref, sc_ref, ng_ref, w_ref, hg_ref, *rest, n_norm_tiles, with_bf16):
    if with_bf16:
        o_ref, obf_ref, xn_ref = rest
    else:
        o_ref, xn_ref = rest
        obf_ref = None
    j = pl.program_id(1)

    @pl.when(j == 0)
    def _():
        xn_ref[...] = _norm_mod(x_ref[...], ng_ref[0], sh_ref[0], sc_ref[0]).astype(BF16)

    acc = jnp.dot(xn_ref[...], w_ref[...].astype(BF16), preferred_element_type=F32)
    tn = acc.shape[1]

    def store(val, sl):
        o_ref[:, sl] = val
        if obf_ref is not None:
            obf_ref[:, sl] = val.astype(BF16)

    @pl.when(j < n_norm_tiles)
    def _():
        for t in range(tn // HEAD_DIM):
            sl = slice(t * HEAD_DIM, (t + 1) * HEAD_DIM)
            a = acc[:, sl]
            ms = jnp.mean(a * a, axis=-1, keepdims=True)
            store(a * lax.rsqrt(ms + EPS) * hg_ref[:, sl], sl)

    @pl.when(j >= n_norm_tiles)
    def _():
        store(acc, slice(None))


def _proj(h, rows, mod, chunk0, norm_g, norm_idx, w, w_idx, head_gain, n_norm_cols, tn, with_bf16=False):
    m, d = h.shape
    n = w.shape[-1]
    tm = rows.tm
    assert n % tn == 0 and n_norm_cols % tn == 0 and tn % HEAD_DIM == 0
    out_shape = [jax.ShapeDtypeStruct((m, n), F32)]
    out_specs = [pl.BlockSpec((tm, tn), lambda i, j: (i, j))]
    if with_bf16:
        out_shape.append(jax.ShapeDtypeStruct((m, n), BF16))
        out_specs.append(pl.BlockSpec((tm, tn), lambda i, j: (i, j)))
    res = pl.pallas_call(
        functools.partial(_proj_kernel, n_norm_tiles=n_norm_cols // tn, with_bf16=with_bf16),
        grid=(m // tm, n // tn),
        in_specs=[
            pl.BlockSpec((tm, d), lambda i, j: (i, 0)),
            rows.cond_spec(chunk0, d),
            rows.cond_spec(chunk0 + 1, d),
            pl.BlockSpec((1, 1, d), lambda i, j: (norm_idx, 0, 0)),
            pl.BlockSpec((None, d, tn), lambda i, j: (w_idx, 0, j)),
            pl.BlockSpec((1, tn), lambda i, j: (0, j)),
        ],
        out_specs=out_specs,
        out_shape=out_shape,
        scratch_shapes=[pltpu.VMEM((tm, d), BF16)],
        compiler_params=_params(("arbitrary", "arbitrary"), 56),
        name="proj",
    )(h, mod, mod, norm_g, w, head_gain)
    return res if with_bf16 else res[0]


def _out_kernel(x_ref, w_ref, h_ref, gt_ref, o_ref):
    y = jnp.dot(x_ref[...].astype(BF16), w_ref[...].astype(BF16), preferred_element_type=F32)
    o_ref[...] = h_ref[...] + gt_ref[0] * y


def _out_proj(x, w, w_idx, h, rows, mod, chunk, tn):
    m, k = x.shape
    n = w.shape[-1]
    tm = rows.tm
    return pl.pallas_call(
        _out_kernel,
        grid=(m // tm, n // tn),
        in_specs=[
            pl.BlockSpec((tm, k), lambda i, j: (i, 0)),
            pl.BlockSpec((None, k, tn), lambda i, j: (w_idx, 0, j)),
            pl.BlockSpec((tm, tn), lambda i, j: (i, j)),
            rows.cond_spec(chunk, tn, col_of=lambda j: j),
        ],
        out_specs=pl.BlockSpec((tm, tn), lambda i, j: (i, j)),
        out_shape=jax.ShapeDtypeStruct((m, n), F32),
        compiler_params=_params(("arbitrary", "arbitrary"), 48),
        name="out_proj",
    )(x, w, h, mod)


def _dil_kernel(q_ref, kc_ref, kp_ref, vc_ref, vp_ref, o_ref, lse_ref, *, slopes, dil):
    n = pl.program_id(2)
    scale = 1.0 / math.sqrt(HEAD_DIM)
    qi = lax.broadcasted_iota(jnp.int32, (A_BLOCK, A_BLOCK), 0)
    ki = lax.broadcasted_iota(jnp.int32, (A_BLOCK, A_BLOCK), 1)
    step_c = qi - ki
    step_p = step_c + A_BLOCK
    valid_c = step_c >= 0
    valid_p = jnp.logical_and(step_p <= A_BLOCK, n > 0)
    dist_c = (step_c * dil).astype(F32)
    dist_p = (step_p * dil).astype(F32)
    nt = (((1,), (1,)), ((), ()))
    lses = []
    for h in range(GROUP_HEADS):
        sl = slice(h * HEAD_DIM, (h + 1) * HEAD_DIM)
        q = (q_ref[0, :, sl] * scale).astype(BF16)
        s_c = lax.dot_general(q, kc_ref[0, :, sl].astype(BF16), nt, preferred_element_type=F32)
        s_p = lax.dot_general(q, kp_ref[0, :, sl].astype(BF16), nt, preferred_element_type=F32)
        s_c = jnp.where(valid_c, s_c - slopes[h] * dist_c, NEG_BIG)
        s_p = jnp.where(valid_p, s_p - slopes[h] * dist_p, NEG_BIG)
        mx = jnp.maximum(jnp.max(s_c, axis=-1, keepdims=True), jnp.max(s_p, axis=-1, keepdims=True))
        p_c = jnp.exp(s_c - mx)
        p_p = jnp.exp(s_p - mx)
        den = jnp.sum(p_c, axis=-1, keepdims=True) + jnp.sum(p_p, axis=-1, keepdims=True)
        o = jnp.dot(p_c.astype(BF16), vc_ref[0, :, sl].astype(BF16), preferred_element_type=F32)
        o = o + jnp.dot(p_p.astype(BF16), vp_ref[0, :, sl].astype(BF16), preferred_element_type=F32)
        o_ref[0, :, sl] = o / den
        lses.append(mx + jnp.log(den))
    top = functools.reduce(jnp.maximum, lses)
    tot = functools.reduce(lambda a, b: a + b, [jnp.exp(l - top) for l in lses])
    group_lse = top + jnp.log(tot) - math.log(GROUP_HEADS)
    lse_ref[0] = jnp.broadcast_to(group_lse, (A_BLOCK, LANES))


def _dilated_prompt(qkv, batch, seq, g):
    dil = DIL_RATES[g]
    assert DIL_WINDOWS[g] // dil == A_BLOCK and seq % (dil * A_BLOCK) == 0
    m_len = seq // dil
    nb = m_len // A_BLOCK
    per_res = QKV_COLS // GROUP_COLS
    k_off = A_HEADS * HEAD_DIM // GROUP_COLS
    qkv3 = qkv.reshape(batch, m_len, dil * QKV_COLS)
    blk = (1, A_BLOCK, GROUP_COLS)
    slopes = tuple(_alibi_slopes()[g * GROUP_HEADS:(g + 1) * GROUP_HEADS])
    o, lse = pl.pallas_call(
        functools.partial(_dil_kernel, slopes=slopes, dil=dil),
        grid=(batch, dil, nb),
        in_specs=[
            pl.BlockSpec(blk, lambda b, r, n: (b, n, r * per_res + g)),
            pl.BlockSpec(blk, lambda b, r, n: (b, n, r * per_res + k_off + g)),
            pl.BlockSpec(blk, lambda b, r, n: (b, jnp.maximum(n - 1, 0), r * per_res + k_off + g)),
            pl.BlockSpec(blk, lambda b, r, n: (b, n, r * per_res + 2 * k_off + g)),
            pl.BlockSpec(blk, lambda b, r, n: (b, jnp.maximum(n - 1, 0), r * per_res + 2 * k_off + g)),
        ],
        out_specs=[
            pl.BlockSpec(blk, lambda b, r, n: (b, n, r)),
            pl.BlockSpec((1, A_BLOCK, LANES), lambda b, r, n: (b, n, r)),
        ],
        out_shape=[
            jax.ShapeDtypeStruct((batch, m_len, dil * GROUP_COLS), F32),
            jax.ShapeDtypeStruct((batch, m_len, dil * LANES), F32),
        ],
        compiler_params=_params(("arbitrary", "arbitrary", "arbitrary"), 32),
        name="dilated_prompt",
    )(qkv3, qkv3, qkv3, qkv3, qkv3)
    return o.reshape(batch * seq, GROUP_COLS), lse.reshape(batch * seq, LANES)


def _mix_out_kernel(o0_ref, o1_ref, o2_ref, l0_ref, l1_ref, l2_ref, w_ref, h_ref, gt_ref, out_ref, mix_ref):
    j = pl.program_id(1)

    @pl.when(j == 0)
    def _():
        ls = [l0_ref[...], l1_ref[...], l2_ref[...]]
        top = jnp.maximum(jnp.maximum(ls[0], ls[1]), ls[2])
        es = [jnp.exp(l - top) for l in ls]
        den = es[0] + es[1] + es[2]
        for g, o_ref in enumerate((o0_ref, o1_ref, o2_ref)):
            alpha = es[g] / den
            for hh in range(GROUP_HEADS):
                src = slice(hh * HEAD_DIM, (hh + 1) * HEAD_DIM)
                dst = slice(g * GROUP_COLS + hh * HEAD_DIM, g * GROUP_COLS + (hh + 1) * HEAD_DIM)
                mix_ref[:, dst] = (o_ref[:, src] * alpha).astype(BF16)

    y = jnp.dot(mix_ref[...], w_ref[...].astype(BF16), preferred_element_type=F32)
    out_ref[...] = h_ref[...] + gt_ref[0] * y


def _mix_out(outs, lses, w, w_idx, h, rows, mod, chunk, tn):
    m = h.shape[0]
    n = w.shape[-1]
    k = w.shape[-2]
    tm = rows.tm
    o_spec = pl.BlockSpec((tm, GROUP_COLS), lambda i, j: (i, 0))
    l_spec = pl.BlockSpec((tm, LANES), lambda i, j: (i, 0))
    return pl.pallas_call(
        _mix_out_kernel,
        grid=(m // tm, n // tn),
        in_specs=[
            o_spec, o_spec, o_spec, l_spec, l_spec, l_spec,
            pl.BlockSpec((None, k, tn), lambda i, j: (w_idx, 0, j)),
            pl.BlockSpec((tm, tn), lambda i, j: (i, j)),
            rows.cond_spec(chunk, tn, col_of=lambda j: j),
        ],
        out_specs=pl.BlockSpec((tm, tn), lambda i, j: (i, j)),
        out_shape=jax.ShapeDtypeStruct((m, n), F32),
        scratch_shapes=[pltpu.VMEM((tm, k), BF16)],
        compiler_params=_params(("arbitrary", "arbitrary"), 48),
        name="mix_out",
    )(*outs, *lses, w, h, mod)


def _dil_sample_kernel(qkv_ref, c0_ref, c1_ref, c2_ref, o_ref, *, slopes):
    scale = 1.0 / math.sqrt(HEAD_DIM)
    nt = (((1,), (1,)), ((), ()))
    kbase = A_HEADS * HEAD_DIM
    vbase = 2 * A_HEADS * HEAD_DIM
    steps = (A_BLOCK - lax.broadcasted_iota(jnp.int32, (1, A_BLOCK), 1)).astype(F32)
    outs, glses = [], []
    for g, c_ref in enumerate((c0_ref, c1_ref, c2_ref)):
        dil = DIL_RATES[g]
        head_outs, head_lses = [], []
        for hh in range(GROUP_HEADS):
            col = (g * GROUP_HEADS + hh) * HEAD_DIM
            q = qkv_ref[0, :, col:col + HEAD_DIM]
            k_new = qkv_ref[0, :, kbase + col:kbase + col + HEAD_DIM]
            v_new = qkv_ref[0, :, vbase + col:vbase + col + HEAD_DIM]
            k_buf = c_ref[0, 0, :, hh * HEAD_DIM:(hh + 1) * HEAD_DIM]
            v_buf = c_ref[0, 0, :, GROUP_COLS + hh * HEAD_DIM:GROUP_COLS + (hh + 1) * HEAD_DIM]
            q8 = jnp.broadcast_to(q * scale, (8, HEAD_DIM)).astype(BF16)
            s_buf = lax.dot_general(q8, k_buf.astype(BF16), nt, preferred_element_type=F32)[0:1]
            s_buf = s_buf - slopes[g * GROUP_HEADS + hh] * dil * steps
            s_new = jnp.sum((q * scale).astype(BF16).astype(F32) * k_new.astype(BF16).astype(F32),
                            axis=-1, keepdims=True)
            mx = jnp.maximum(jnp.max(s_buf, axis=-1, keepdims=True), s_new)
            p_buf = jnp.exp(s_buf - mx)
            p_new = jnp.exp(s_new - mx)
            den = jnp.sum(p_buf, axis=-1, keepdims=True) + p_new
            p8 = jnp.broadcast_to(p_buf, (8, A_BLOCK)).astype(BF16)
            o = jnp.dot(p8, v_buf.astype(BF16), preferred_element_type=F32)[0:1]
            o = (o + p_new.astype(BF16).astype(F32) * v_new.astype(BF16).astype(F32)) / den
            head_outs.append(o)
            head_lses.append(mx + jnp.log(den))
        top = functools.reduce(jnp.maximum, head_lses)
        tot = functools.reduce(lambda a, b: a + b, [jnp.exp(l - top) for l in head_lses])
        glses.append(top + jnp.log(tot) - math.log(GROUP_HEADS))
        outs.append(head_outs)
    top = functools.reduce(jnp.maximum, glses)
    es = [jnp.exp(l - top) for l in glses]
    den = es[0] + es[1] + es[2]
    for g in range(N_GROUPS):
        alpha = es[g] / den
        for hh in range(GROUP_HEADS):
            col = (g * GROUP_HEADS + hh) * HEAD_DIM
            o_ref[0, :, col:col + HEAD_DIM] = outs[g][hh] * alpha


def _dilated_sample(qkv_s, caches, layer):
    db = qkv_s.shape[0]
    row_cols = 2 * GROUP_COLS
    views, specs = [], []
    for g, c in enumerate(caches):
        dil = DIL_RATES[g]
        assert c.shape[2] == DIL_WINDOWS[g] and c.shape[2] // dil == A_BLOCK
        views.append(c.reshape(c.shape[0], db, A_BLOCK, dil * row_cols))
        specs.append(pl.BlockSpec((1, 1, A_BLOCK, row_cols), lambda b: (layer, b, 0, 0)))
    out = pl.pallas_call(
        functools.partial(_dil_sample_kernel, slopes=tuple(_alibi_slopes())),
        grid=(db,),
        in_specs=[pl.BlockSpec((1, 1, QKV_COLS), lambda b: (b, 0, 0))] + specs,
        out_specs=pl.BlockSpec((1, 1, A_HEADS * HEAD_DIM), lambda b: (b, 0, 0)),
        out_shape=jax.ShapeDtypeStruct((db, 1, A_HEADS * HEAD_DIM), F32),
        compiler_params=_params(("arbitrary",), 32),
        name="dilated_sample",
    )(qkv_s.reshape(db, 1, QKV_COLS), *views)
    return out.reshape(db, A_HEADS * HEAD_DIM)


def _softplus(z):
    return jnp.maximum(z, 0.0) + jnp.log1p(jnp.exp(-jnp.abs(z)))


def _split_bf16(x):
    hi = x.astype(BF16)
    lo = (x - hi.astype(F32)).astype(BF16)
    return hi, lo


def _sb_prompt_kernel(bias_ref, q_ref, k_ref, v_ref, o_ref, *, tq, sub):
    h = pl.program_id(1)
    i = pl.program_id(2)
    n_sub = tq // sub
    scale = 1.0 / math.sqrt(HEAD_DIM)
    bias = bias_ref[h]
    q = (q_ref[0] * scale).astype(BF16)
    nt = (((1,), (1,)), ((), ()))
    r = lax.broadcasted_iota(jnp.int32, (2 * sub, 2 * sub), 0) % sub
    c = lax.broadcasted_iota(jnp.int32, (2 * sub, 2 * sub), 1)
    u_ext = jnp.where(jnp.logical_or(c >= sub, r > c), 1.0, 0.0).astype(BF16)
    q_pos = lax.broadcasted_iota(jnp.int32, (tq, sub), 0)
    k_off = lax.broadcasted_iota(jnp.int32, (tq, sub), 1)

    def block(start, carry, acc, masked):
        k = k_ref[0, pl.ds(start, tq), :]
        v = v_ref[0, pl.ds(start, tq), :]
        z_all = lax.dot_general(q, k, nt, preferred_element_type=F32) + bias
        ws = [None] * n_sub
        for cidx in reversed(range(n_sub)):
            z = z_all[:, cidx * sub:(cidx + 1) * sub]
            log_stay = -_softplus(z)
            if masked:
                causal = (k_off + cidx * sub) < q_pos
                stay = jnp.where(causal, log_stay, 0.0)
            else:
                stay = log_stay
            hi, lo = _split_bf16(stay)
            cs = jnp.dot(jnp.concatenate([hi, lo], axis=1), u_ext, preferred_element_type=F32)
            w = jnp.exp(z + log_stay + cs[:, :sub] + carry)
            if masked:
                w = jnp.where(causal, w, 0.0)
            ws[cidx] = w.astype(BF16)
            carry = carry + cs[:, sub:]
        acc = acc + jnp.dot(jnp.concatenate(ws, axis=1), v, preferred_element_type=F32)
        return carry, acc

    zeros = jnp.zeros((tq, sub), F32)
    carry, acc = block(pl.multiple_of(i * tq, tq), zeros, zeros, True)

    def body(t, state):
        start = pl.multiple_of((i - 1 - t) * tq, tq)
        return block(start, state[0], state[1], False)

    carry, acc = lax.fori_loop(0, i, body, (carry, acc))
    o_ref[0] = acc.astype(o_ref.dtype)


def _sb_prompt(q, kv_bf16, sb_bias, batch, seq, tq):
    n_heads = q.shape[1] // HEAD_DIM
    q3 = q.reshape(batch, seq, n_heads * HEAD_DIM)
    kv3 = kv_bf16.reshape(batch, seq, 2 * n_heads * HEAD_DIM)
    out = pl.pallas_call(
        functools.partial(_sb_prompt_kernel, tq=tq, sub=HEAD_DIM),
        grid_spec=pltpu.PrefetchScalarGridSpec(
            num_scalar_prefetch=0,
            grid=(batch, n_heads, seq // tq),
            in_specs=[
                pl.BlockSpec(memory_space=pltpu.SMEM),
                pl.BlockSpec((1, tq, HEAD_DIM), lambda b, h, i: (b, i, h)),
                pl.BlockSpec((1, seq, HEAD_DIM), lambda b, h, i: (b, 0, h)),
                pl.BlockSpec((1, seq, HEAD_DIM), lambda b, h, i: (b, 0, n_heads + h)),
            ],
            out_specs=pl.BlockSpec((1, tq, HEAD_DIM), lambda b, h, i: (b, i, h)),
        ),
        out_shape=jax.ShapeDtypeStruct((batch, seq, n_heads * HEAD_DIM), BF16),
        compiler_params=_params(("arbitrary", "arbitrary", "arbitrary"), 40),
        name="sb_prompt",
    )(sb_bias, q3, kv3, kv3)
    return out.reshape(batch * seq, n_heads * HEAD_DIM)


def _sb_sample_kernel(pt_ref, qmat_ref, bias_ref, kv_ref, o_ref, acc_ref, carry_ref, *, n_heads):
    p = pl.program_id(1)
    hd = n_heads * HEAD_DIM

    @pl.when(p == 0)
    def _():
        acc_ref[...] = jnp.zeros_like(acc_ref)
        carry_ref[...] = jnp.zeros_like(carry_ref)

    k = kv_ref[0, :, :hd].astype(BF16)
    v = kv_ref[0, :, hd:].astype(BF16)
    z = jnp.dot(k, qmat_ref[0], preferred_element_type=F32) + bias_ref[...]
    log_stay = -_softplus(z)
    page = z.shape[0]
    r = lax.broadcasted_iota(jnp.int32, (page, page), 0)
    c = lax.broadcasted_iota(jnp.int32, (page, page), 1)
    later_mat = jnp.where(c > r, 1.0, 0.0).astype(BF16)
    hi, lo = _split_bf16(log_stay)
    later = jnp.dot(later_mat, hi, preferred_element_type=F32) + jnp.dot(later_mat, lo, preferred_element_type=F32)
    w = jnp.exp(z + log_stay + later + carry_ref[...])
    tn = (((0,), (0,)), ((), ()))
    acc_ref[...] += lax.dot_general(w.astype(BF16), v, tn, preferred_element_type=F32)
    carry_ref[...] += jnp.sum(log_stay, axis=0, keepdims=True)

    @pl.when(p == pl.num_programs(1) - 1)
    def _():
        for hh in range(n_heads):
            sl = slice(hh * HEAD_DIM, (hh + 1) * HEAD_DIM)
            o_ref[0, :, sl] = acc_ref[hh:hh + 1, sl]


def _sb_sample(q, cache_kv, page_table, sb_bias):
    db, hd = q.shape
    n_heads = hd // HEAD_DIM
    assert n_heads <= LANES
    n_phys, page = cache_kv.shape[0], cache_kv.shape[1]
    n_pages = page_table.shape[1]
    scale = 1.0 / math.sqrt(HEAD_DIM)
    head_of_row = jnp.arange(hd, dtype=jnp.int32) // HEAD_DIM
    sel = (head_of_row[:, None] == jnp.arange(LANES, dtype=jnp.int32)[None, :]).astype(F32)
    qmat = ((q * scale)[:, :, None] * sel[None]).astype(BF16)
    bias_row = jnp.zeros((1, LANES), F32).at[0, :n_heads].set(sb_bias.astype(F32))
    kv = cache_kv.reshape(n_phys, page, 2 * hd)
    out = pl.pallas_call(
        functools.partial(_sb_sample_kernel, n_heads=n_heads),
        grid_spec=pltpu.PrefetchScalarGridSpec(
            num_scalar_prefetch=1,
            grid=(db, n_pages),
            in_specs=[
                pl.BlockSpec((1, hd, LANES), lambda b, p, pt: (b, 0, 0)),
                pl.BlockSpec((1, LANES), lambda b, p, pt: (0, 0)),
                pl.BlockSpec((1, page, 2 * hd), lambda b, p, pt: (pt[b, n_pages - 1 - p], 0, 0)),
            ],
            out_specs=pl.BlockSpec((1, 1, hd), lambda b, p, pt: (b, 0, 0)),
            scratch_shapes=[pltpu.VMEM((LANES, hd), F32), pltpu.VMEM((1, LANES), F32)],
        ),
        out_shape=jax.ShapeDtypeStruct((db, 1, hd), F32),
        compiler_params=_params(("arbitrary", "arbitrary"), 32),
        name="sb_sample",
    )(page_table, qmat, bias_row, kv)
    return out.reshape(db, hd)


def kernel(x_prompt, x_sample, cache_win_g0, cache_win_g1, cache_win_g2, cache_kv, page_table, c_prompt, c_sample, w_mod, b_mod, norm_g, ffn_w_in, ffn_w_out, a_w_qkv, a_q_norm, a_k_norm, a_w_o, kv_norm, w_mod_kv, b_mod_kv, w_kv, sb_k_norm, b_w_q, b_q_norm, b_sb_bias, b_w_o):
    batch, seq, d = x_prompt.shape
    db, ds, _ = x_sample.shape
    depth = w_mod.shape[0]
    n_a = a_w_qkv.shape[0]
    d_ff = ffn_w_out.shape[2]
    n_b_heads = w_kv.shape[1] // (2 * HEAD_DIM)
    assert ds == 1 and db == 8 and db + batch <= MOD_ROWS
    caches = (cache_win_g0, cache_win_g1, cache_win_g2)

    c_all = jnp.zeros((MOD_ROWS, d), F32).at[:db].set(c_sample).at[db:db + batch].set(c_prompt)
    mod = _modulation(c_all, w_mod, b_mod)
    mod_kv = _modulation(c_all, w_mod_kv[None], b_mod_kv[None])
    mod_s = mod[:, :, :db].reshape(depth * N_MOD, db, d)
    mod_p = mod[:, :, db:db + batch].reshape(depth * N_MOD * batch, 1, d)
    modkv_s = mod_kv[:, :, :db].reshape(2, db, d)
    modkv_p = mod_kv[:, :, db:db + batch].reshape(2 * batch, 1, d)

    tm_p = _pick_tile(seq, 1024)
    rows_p = _Rows(batch * seq, batch, 1, tm_p)
    rows_s = _Rows(db, 1, db, db)
    paths = (
        dict(rows=rows_p, mod=mod_p, modkv=modkv_p, tf=_pick_tile(d_ff, 256)),
        dict(rows=rows_s, mod=mod_s, modkv=modkv_s, tf=_pick_tile(d_ff, 512)),
    )

    norm3 = norm_g.reshape(depth * 3, 1, d)
    kvn3 = kv_norm.reshape(1, 1, d)
    w_in = ffn_w_in.reshape(depth * 2, d, 2 * d_ff)
    w_out = ffn_w_out.reshape(depth * 2, d_ff, d)
    w_kv3 = w_kv[None]

    a_gain = [jnp.concatenate([jnp.tile(a_q_norm[l], A_HEADS), jnp.tile(a_k_norm[l], A_HEADS),
                               jnp.ones((A_HEADS * HEAD_DIM,), F32)])[None] for l in range(n_a)]
    kv_gain = jnp.concatenate([jnp.tile(sb_k_norm, n_b_heads), jnp.ones((n_b_heads * HEAD_DIM,), F32)])[None]
    b_gain = [jnp.tile(b_q_norm[j], n_b_heads)[None] for j in range(depth - n_a)]

    h_p = x_prompt.reshape(batch * seq, d)
    h_s = x_sample.reshape(db, d)
    tn_qkv = GROUP_COLS
    tn_d = _pick_tile(d, 512)
    win_p = [[] for _ in range(N_GROUPS)]
    win_s = [[] for _ in range(N_GROUPS)]
    kv_p = kv_s = kv_p_bf16 = None

    def ffn(h, path, l, which):
        return _ffn(h, path["rows"], path["mod"], (l * N_MOD + 6 * which), norm3, l * 3 + 2 * which,
                    w_in, w_out, l * 2 + which, path["tf"])

    for l in range(depth):
        h_p = ffn(h_p, paths[0], l, 0)
        h_s = ffn(h_s, paths[1], l, 0)
        if l < n_a:
            qkv_p = _proj(h_p, rows_p, mod_p, l * N_MOD + 3, norm3, l * 3 + 1, a_w_qkv, l, a_gain[l],
                          2 * A_HEADS * HEAD_DIM, tn_qkv)
            qkv_s = _proj(h_s, rows_s, mod_s, l * N_MOD + 3, norm3, l * 3 + 1, a_w_qkv, l, a_gain[l],
                          2 * A_HEADS * HEAD_DIM, tn_qkv)
            outs, lses = [], []
            for g in range(N_GROUPS):
                o_g, lse_g = _dilated_prompt(qkv_p, batch, seq, g)
                outs.append(o_g)
                lses.append(lse_g)
            h_p = _mix_out(outs, lses, a_w_o, l, h_p, rows_p, mod_p, l * N_MOD + 5, tn_d)
            mixed_s = _dilated_sample(qkv_s, caches, l)
            h_s = _out_proj(mixed_s, a_w_o, l, h_s, rows_s, mod_s, l * N_MOD + 5, tn_d)

            qkv_p5 = qkv_p.reshape(batch, seq, 3, A_HEADS, HEAD_DIM)
            qkv_s5 = qkv_s.reshape(db, 1, 3, A_HEADS, HEAD_DIM)
            for g in range(N_GROUPS):
                hs = slice(g * GROUP_HEADS, (g + 1) * GROUP_HEADS)
                keep = min(DIL_WINDOWS[g], seq)
                win_p[g].append(qkv_p5[:, seq - keep:, 1:3, hs])
                buf = caches[g][l]
                win_s[g].append(jnp.concatenate([buf[:, 1:], qkv_s5[:, :, 1:3, hs]], axis=1))
        else:
            j = l - n_a
            q_p = _proj(h_p, rows_p, mod_p, l * N_MOD + 3, norm3, l * 3 + 1, b_w_q, j, b_gain[j],
                        n_b_heads * HEAD_DIM, tn_d)
            q_s = _proj(h_s, rows_s, mod_s, l * N_MOD + 3, norm3, l * 3 + 1, b_w_q, j, b_gain[j],
                        n_b_heads * HEAD_DIM, tn_d)
            o_p = _sb_prompt(q_p, kv_p_bf16, b_sb_bias[j], batch, seq, _pick_tile(seq, 512))
            h_p = _out_proj(o_p, b_w_o, j, h_p, rows_p, mod_p, l * N_MOD + 5, tn_d)
            o_s = _sb_sample(q_s, cache_kv, page_table, b_sb_bias[j])
            h_s = _out_proj(o_s, b_w_o, j, h_s, rows_s, mod_s, l * N_MOD + 5, tn_d)
        h_p = ffn(h_p, paths[0], l, 1)
        h_s = ffn(h_s, paths[1], l, 1)
        if l == n_a - 1:
            kv_p, kv_p_bf16 = _proj(h_p, rows_p, modkv_p, 0, kvn3, 0, w_kv3, 0, kv_gain,
                                    n_b_heads * HEAD_DIM, tn_d, with_bf16=True)
            kv_s = _proj(h_s, rows_s, modkv_s, 0, kvn3, 0, w_kv3, 0, kv_gain, n_b_heads * HEAD_DIM, tn_d)

    y_p = h_p.reshape(batch, seq, d)
    y_s = h_s.reshape(db, 1, d)
    win_p = [jnp.stack(w, axis=0) for w in win_p]
    win_s = [jnp.stack(w, axis=0) for w in win_s]
    kv_p = kv_p.reshape(batch, seq, 2, n_b_heads, HEAD_DIM)
    kv_s = kv_s.reshape(db, 1, 2, n_b_heads, HEAD_DIM)
    return (y_p, y_s, win_p[0], win_p[1], win_p[2], kv_p, win_s[0], win_s[1], win_s[2], kv_s)
```

```python
import functools
import math

import numpy as np
import jax
import jax.numpy as jnp
from jax import lax
from jax.experimental import pallas as pl
from jax.experimental.pallas import tpu as pltpu

F32 = jnp.float32
BF16 = jnp.bfloat16

EPS = 1e-6
HEAD_DIM = 128
LANES = 128
N_MOD = 9
DIL_WINDOWS = (128, 512, 2048)
DIL_RATES = (1, 4, 16)
N_GROUPS = 3
GROUP_HEADS = 5
A_HEADS = N_GROUPS * GROUP_HEADS
A_BLOCK = 128
GROUP_COLS = GROUP_HEADS * HEAD_DIM
QKV_COLS = 3 * A_HEADS * HEAD_DIM
NEG_BIG = -1e30
MIB = 1024 * 1024
MOD_ROWS = 16


def _alibi_slopes():
    return [float(2.0 ** (-8.0 * (i + 1) / A_HEADS)) for i in range(A_HEADS)]


def _params(semantics, vmem_mib):
    return pltpu.CompilerParams(dimension_semantics=semantics, vmem_limit_bytes=vmem_mib * MIB)


def _pick_tile(n, preferred):
    if n <= preferred:
        return n
    t = (preferred // LANES) * LANES
    while t >= LANES:
        if n % t == 0:
            return t
        t -= LANES
    return n


class _Rows:
    def __init__(self, n_rows, groups, cond_rows, tm):
        self.n_rows = n_rows
        self.groups = groups
        self.cond_rows = cond_rows
        self.tm = tm
        self.tiles_per_group = (n_rows // groups) // tm
        assert self.tiles_per_group * tm * groups == n_rows

    def cond_spec(self, chunk, width, col_of=None):
        g, tpg = self.groups, self.tiles_per_group
        if col_of is None:
            return pl.BlockSpec((1, self.cond_rows, width), lambda i, j: (chunk * g + i // tpg, 0, 0))
        return pl.BlockSpec((1, self.cond_rows, width), lambda i, j: (chunk * g + i // tpg, 0, col_of(j)))


def _norm_mod(x, gain, shift, scale):
    ms = jnp.mean(x * x, axis=-1, keepdims=True)
    y = x * lax.rsqrt(ms + EPS) * gain
    return y * (1.0 + scale) + shift


def _mod_kernel(c_ref, w_ref, b_ref, o_ref):
    c = c_ref[...]
    a = (c * jax.nn.sigmoid(c)).astype(BF16)
    o_ref[0, 0] = jnp.dot(a, w_ref[0].astype(BF16), preferred_element_type=F32) + b_ref[0]


def _modulation(c_all, w, b):
    n_layers, d, n_out = w.shape
    n_chunks = n_out // d
    tn = _pick_tile(d, 1024)
    per_chunk = d // tn
    return pl.pallas_call(
        _mod_kernel,
        grid=(n_layers, n_out // tn),
        in_specs=[
            pl.BlockSpec((MOD_ROWS, d), lambda l, j: (0, 0)),
            pl.BlockSpec((1, d, tn), lambda l, j: (l, 0, j)),
            pl.BlockSpec((1, 1, tn), lambda l, j: (l, 0, j)),
        ],
        out_specs=pl.BlockSpec((1, 1, MOD_ROWS, tn), lambda l, j: (l, j // per_chunk, 0, j % per_chunk)),
        out_shape=jax.ShapeDtypeStruct((n_layers, n_chunks, MOD_ROWS, d), F32),
        compiler_params=_params(("arbitrary", "arbitrary"), 40),
        name="modulation",
    )(c_all, w, b.reshape(n_layers, 1, n_out))


def _ffn_kernel(x_ref, sh_ref, sc_ref, gt_ref, ng_ref, wg_ref, wu_ref, wo_ref, o_ref, xn_ref):
    f = pl.program_id(1)

    @pl.when(f == 0)
    def _():
        u = _norm_mod(x_ref[...], ng_ref[0], sh_ref[0], sc_ref[0])
        xn_ref[...] = u.astype(BF16)
        o_ref[...] = jnp.zeros_like(o_ref)

    xn = xn_ref[...]
    gate = jnp.dot(xn, wg_ref[...].astype(BF16), preferred_element_type=F32)
    up = jnp.dot(xn, wu_ref[...].astype(BF16), preferred_element_type=F32)
    act = (gate * jax.nn.sigmoid(gate) * up).astype(BF16)
    o_ref[...] += jnp.dot(act, wo_ref[...].astype(BF16), preferred_element_type=F32)

    @pl.when(f == pl.num_programs(1) - 1)
    def _():
        o_ref[...] = x_ref[...] + 0.5 * gt_ref[0] * o_ref[...]


def _ffn(h, rows, mod, chunk0, norm_g, norm_idx, w_in, w_out, w_idx, tf):
    m, d = h.shape
    d_ff = w_out.shape[1]
    n_f = d_ff // tf
    tm = rows.tm
    return pl.pallas_call(
        _ffn_kernel,
        grid=(m // tm, n_f),
        in_specs=[
            pl.BlockSpec((tm, d), lambda i, f: (i, 0)),
            rows.cond_spec(chunk0, d),
            rows.cond_spec(chunk0 + 1, d),
            rows.cond_spec(chunk0 + 2, d),
            pl.BlockSpec((1, 1, d), lambda i, f: (norm_idx, 0, 0)),
            pl.BlockSpec((None, d, tf), lambda i, f: (w_idx, 0, f)),
            pl.BlockSpec((None, d, tf), lambda i, f: (w_idx, 0, n_f + f)),
            pl.BlockSpec((None, tf, d), lambda i, f: (w_idx, f, 0)),
        ],
        out_specs=pl.BlockSpec((tm, d), lambda i, f: (i, 0)),
        out_shape=jax.ShapeDtypeStruct((m, d), F32),
        scratch_shapes=[pltpu.VMEM((tm, d), BF16)],
        compiler_params=_params(("arbitrary", "arbitrary"), 58),
        name="ffn",
    )(h, mod, mod, mod, norm_g, w_in, w_in, w_out)


def _proj_kernel(x_ref, sh_ref, sc_ref, ng_ref, w_ref, hg_ref, *rest, n_norm_tiles, with_bf16):
    if with_bf16:
        o_ref, obf_ref, xn_ref = rest
    else:
        o_ref, xn_ref = rest
        obf_ref = None
    j = pl.program_id(1)

    @pl.when(j == 0)
    def _():
        xn_ref[...] = _norm_mod(x_ref[...], ng_ref[0], sh_ref[0], sc_ref[0]).astype(BF16)

    acc = jnp.dot(xn_ref[...], w_ref[...].astype(BF16), preferred_element_type=F32)
    tn = acc.shape[1]

    def store(val, sl):
        o_ref[:, sl] = val
        if obf_ref is not None:
            obf_ref[:, sl] = val.astype(BF16)

    @pl.when(j < n_norm_tiles)
    def _():
        for t in range(tn // HEAD_DIM):
            sl = slice(t * HEAD_DIM, (t + 1) * HEAD_DIM)
            a = acc[:, sl]
            ms = jnp.mean(a * a, axis=-1, keepdims=True)
            store(a * lax.rsqrt(ms + EPS) * hg_ref[:, sl], sl)

    @pl.when(j >= n_norm_tiles)
    def _():
        store(acc, slice(None))


def _proj(h, rows, mod, chunk0, norm_g, norm_idx, w, w_idx, head_gain, n_norm_cols, tn, with_bf16=False):
    m, d = h.shape
    n = w.shape[-1]
    tm = rows.tm
    assert n % tn == 0 and n_norm_cols % tn == 0 and tn % HEAD_DIM == 0
    out_shape = [jax.ShapeDtypeStruct((m, n), F32)]
    out_specs = [pl.BlockSpec((tm, tn), lambda i, j: (i, j))]
    if with_bf16:
        out_shape.append(jax.ShapeDtypeStruct((m, n), BF16))
        out_specs.append(pl.BlockSpec((tm, tn), lambda i, j: (i, j)))
    res = pl.pallas_call(
        functools.partial(_proj_kernel, n_norm_tiles=n_norm_cols // tn, with_bf16=with_bf16),
        grid=(m // tm, n // tn),
        in_specs=[
            pl.BlockSpec((tm, d), lambda i, j: (i, 0)),
            rows.cond_spec(chunk0, d),
            rows.cond_spec(chunk0 + 1, d),
            pl.BlockSpec((1, 1, d), lambda i, j: (norm_idx, 0, 0)),
            pl.BlockSpec((None, d, tn), lambda i, j: (w_idx, 0, j)),
            pl.BlockSpec((1, tn), lambda i, j: (0, j)),
        ],
        out_specs=out_specs,
        out_shape=out_shape,
        scratch_shapes=[pltpu.VMEM((tm, d), BF16)],
        compiler_params=_params(("arbitrary", "arbitrary"), 56),
        name="proj",
    )(h, mod, mod, norm_g, w, head_gain)
    return res if with_bf16 else res[0]


def _out_kernel(x_ref, w_ref, h_ref, gt_ref, o_ref):
    y = jnp.dot(x_ref[...].astype(BF16), w_ref[...].astype(BF16), preferred_element_type=F32)
    o_ref[...] = h_ref[...] + gt_ref[0] * y


def _out_proj(x, w, w_idx, h, rows, mod, chunk, tn):
    m, k = x.shape
    n = w.shape[-1]
    tm = rows.tm
    return pl.pallas_call(
        _out_kernel,
        grid=(m // tm, n // tn),
        in_specs=[
            pl.BlockSpec((tm, k), lambda i, j: (i, 0)),
            pl.BlockSpec((None, k, tn), lambda i, j: (w_idx, 0, j)),
            pl.BlockSpec((tm, tn), lambda i, j: (i, j)),
            rows.cond_spec(chunk, tn, col_of=lambda j: j),
        ],
        out_specs=pl.BlockSpec((tm, tn), lambda i, j: (i, j)),
        out_shape=jax.ShapeDtypeStruct((m, n), F32),
        compiler_params=_params(("arbitrary", "arbitrary"), 48),
        name="out_proj",
    )(x, w, h, mod)


def _dil_kernel(q_ref, kc_ref, kp_ref, vc_ref, vp_ref, o_ref, lse_ref, *, slopes, dil):
    n = pl.program_id(2)
    scale = 1.0 / math.sqrt(HEAD_DIM)
    qi = lax.broadcasted_iota(jnp.int32, (A_BLOCK, A_BLOCK), 0)
    ki = lax.broadcasted_iota(jnp.int32, (A_BLOCK, A_BLOCK), 1)
    step_c = qi - ki
    step_p = step_c + A_BLOCK
    valid_c = step_c >= 0
    valid_p = jnp.logical_and(step_p <= A_BLOCK, n > 0)
    dist_c = (step_c * dil).astype(F32)
    dist_p = (step_p * dil).astype(F32)
    nt = (((1,), (1,)), ((), ()))
    lses = []
    for h in range(GROUP_HEADS):
        sl = slice(h * HEAD_DIM, (h + 1) * HEAD_DIM)
        q = (q_ref[0, :, sl] * scale).astype(BF16)
        s_c = lax.dot_general(q, kc_ref[0, :, sl].astype(BF16), nt, preferred_element_type=F32)
        s_p = lax.dot_general(q, kp_ref[0, :, sl].astype(BF16), nt, preferred_element_type=F32)
        s_c = jnp.where(valid_c, s_c - slopes[h] * dist_c, NEG_BIG)
        s_p = jnp.where(valid_p, s_p - slopes[h] * dist_p, NEG_BIG)
        mx = jnp.maximum(jnp.max(s_c, axis=-1, keepdims=True), jnp.max(s_p, axis=-1, keepdims=True))
        p_c = jnp.exp(s_c - mx)
        p_p = jnp.exp(s_p - mx)
        den = jnp.sum(p_c, axis=-1, keepdims=True) + jnp.sum(p_p, axis=-1, keepdims=True)
        o = jnp.dot(p_c.astype(BF16), vc_ref[0, :, sl].astype(BF16), preferred_element_type=F32)
        o = o + jnp.dot(p_p.astype(BF16), vp_ref[0, :, sl].astype(BF16), preferred_element_type=F32)
        o_ref[0, :, sl] = o / den
        lses.append(mx + jnp.log(den))
    top = functools.reduce(jnp.maximum, lses)
    tot = functools.reduce(lambda a, b: a + b, [jnp.exp(l - top) for l in lses])
    group_lse = top + jnp.log(tot) - math.log(GROUP_HEADS)
    lse_ref[0] = jnp.broadcast_to(group_lse, (A_BLOCK, LANES))


def _dilated_prompt(qkv, batch, seq, g):
    dil = DIL_RATES[g]
    assert DIL_WINDOWS[g] // dil == A_BLOCK and seq % (dil * A_BLOCK) == 0
    m_len = seq // dil
    nb = m_len // A_BLOCK
    per_res = QKV_COLS // GROUP_COLS
    k_off = A_HEADS * HEAD_DIM // GROUP_COLS
    qkv3 = qkv.reshape(batch, m_len, dil * QKV_COLS)
    blk = (1, A_BLOCK, GROUP_COLS)
    slopes = tuple(_alibi_slopes()[g * GROUP_HEADS:(g + 1) * GROUP_HEADS])
    o, lse = pl.pallas_call(
        functools.partial(_dil_kernel, slopes=slopes, dil=dil),
        grid=(batch, dil, nb),
        in_specs=[
            pl.BlockSpec(blk, lambda b, r, n: (b, n, r * per_res + g)),
            pl.BlockSpec(blk, lambda b, r, n: (b, n, r * per_res + k_off + g)),
            pl.BlockSpec(blk, lambda b, r, n: (b, jnp.maximum(n - 1, 0), r * per_res + k_off + g)),
            pl.BlockSpec(blk, lambda b, r, n: (b, n, r * per_res + 2 * k_off + g)),
            pl.BlockSpec(blk, lambda b, r, n: (b, jnp.maximum(n - 1, 0), r * per_res + 2 * k_off + g)),
        ],
        out_specs=[
            pl.BlockSpec(blk, lambda b, r, n: (b, n, r)),
            pl.BlockSpec((1, A_BLOCK, LANES), lambda b, r, n: (b, n, r)),
        ],
        out_shape=[
            jax.ShapeDtypeStruct((batch, m_len, dil * GROUP_COLS), F32),
            jax.ShapeDtypeStruct((batch, m_len, dil * LANES), F32),
        ],
        compiler_params=_params(("arbitrary", "arbitrary", "arbitrary"), 32),
        name="dilated_prompt",
    )(qkv3, qkv3, qkv3, qkv3, qkv3)
    return o.reshape(batch * seq, GROUP_COLS), lse.reshape(batch * seq, LANES)


def _mix_out_kernel(o0_ref, o1_ref, o2_ref, l0_ref, l1_ref, l2_ref, w_ref, h_ref, gt_ref, out_ref, mix_ref):
    j = pl.program_id(1)

    @pl.when(j == 0)
    def _():
        ls = [l0_ref[...], l1_ref[...], l2_ref[...]]
        top = jnp.maximum(jnp.maximum(ls[0], ls[1]), ls[2])
        es = [jnp.exp(l - top) for l in ls]
        den = es[0] + es[1] + es[2]
        for g, o_ref in enumerate((o0_ref, o1_ref, o2_ref)):
            alpha = es[g] / den
            for hh in range(GROUP_HEADS):
                src = slice(hh * HEAD_DIM, (hh + 1) * HEAD_DIM)
                dst = slice(g * GROUP_COLS + hh * HEAD_DIM, g * GROUP_COLS + (hh + 1) * HEAD_DIM)
                mix_ref[:, dst] = (o_ref[:, src] * alpha).astype(BF16)

    y = jnp.dot(mix_ref[...], w_ref[...].astype(BF16), preferred_element_type=F32)
    out_ref[...] = h_ref[...] + gt_ref[0] * y


def _mix_out(outs, lses, w, w_idx, h, rows, mod, chunk, tn):
    m = h.shape[0]
    n = w.shape[-1]
    k = w.shape[-2]
    tm = rows.tm
    o_spec = pl.BlockSpec((tm, GROUP_COLS), lambda i, j: (i, 0))
    l_spec = pl.BlockSpec((tm, LANES), lambda i, j: (i, 0))
    return pl.pallas_call(
        _mix_out_kernel,
        grid=(m // tm, n // tn),
        in_specs=[
            o_spec, o_spec, o_spec, l_spec, l_spec, l_spec,
            pl.BlockSpec((None, k, tn), lambda i, j: (w_idx, 0, j)),
            pl.BlockSpec((tm, tn), lambda i, j: (i, j)),
            rows.cond_spec(chunk, tn, col_of=lambda j: j),
        ],
        out_specs=pl.BlockSpec((tm, tn), lambda i, j: (i, j)),
        out_shape=jax.ShapeDtypeStruct((m, n), F32),
        scratch_shapes=[pltpu.VMEM((tm, k), BF16)],
        compiler_params=_params(("arbitrary", "arbitrary"), 48),
        name="mix_out",
    )(*outs, *lses, w, h, mod)


def _dil_sample_kernel(qkv_ref, c0_ref, c1_ref, c2_ref, o_ref, *, slopes):
    scale = 1.0 / math.sqrt(HEAD_DIM)
    nt = (((1,), (1,)), ((), ()))
    kbase = A_HEADS * HEAD_DIM
    vbase = 2 * A_HEADS * HEAD_DIM
    steps = (A_BLOCK - lax.broadcasted_iota(jnp.int32, (1, A_BLOCK), 1)).astype(F32)
    outs, glses = [], []
    for g, c_ref in enumerate((c0_ref, c1_ref, c2_ref)):
        dil = DIL_RATES[g]
        head_outs, head_lses = [], []
        for hh in range(GROUP_HEADS):
            col = (g * GROUP_HEADS + hh) * HEAD_DIM
            q = qkv_ref[0, :, col:col + HEAD_DIM]
            k_new = qkv_ref[0, :, kbase + col:kbase + col + HEAD_DIM]
            v_new = qkv_ref[0, :, vbase + col:vbase + col + HEAD_DIM]
            k_buf = c_ref[0, 0, :, hh * HEAD_DIM:(hh + 1) * HEAD_DIM]
            v_buf = c_ref[0, 0, :, GROUP_COLS + hh * HEAD_DIM:GROUP_COLS + (hh + 1) * HEAD_DIM]
            q8 = jnp.broadcast_to(q * scale, (8, HEAD_DIM)).astype(BF16)
            s_buf = lax.dot_general(q8, k_buf.astype(BF16), nt, preferred_element_type=F32)[0:1]
            s_buf = s_buf - slopes[g * GROUP_HEADS + hh] * dil * steps
            s_new = jnp.sum((q * scale).astype(BF16).astype(F32) * k_new.astype(BF16).astype(F32),
                            axis=-1, keepdims=True)
            mx = jnp.maximum(jnp.max(s_buf, axis=-1, keepdims=True), s_new)
            p_buf = jnp.exp(s_buf - mx)
            p_new = jnp.exp(s_new - mx)
            den = jnp.sum(p_buf, axis=-1, keepdims=True) + p_new
            p8 = jnp.broadcast_to(p_buf, (8, A_BLOCK)).astype(BF16)
            o = jnp.dot(p8, v_buf.astype(BF16), preferred_element_type=F32)[0:1]
            o = (o + p_new.astype(BF16).astype(F32) * v_new.astype(BF16).astype(F32)) / den
            head_outs.append(o)
            head_lses.append(mx + jnp.log(den))
        top = functools.reduce(jnp.maximum, head_lses)
        tot = functools.reduce(lambda a, b: a + b, [jnp.exp(l - top) for l in head_lses])
        glses.append(top + jnp.log(tot) - math.log(GROUP_HEADS))
        outs.append(head_outs)
    top = functools.reduce(jnp.maximum, glses)
    es = [jnp.exp(l - top) for l in glses]
    den = es[0] + es[1] + es[2]
    for g in range(N_GROUPS):
        alpha = es[g] / den
        for hh in range(GROUP_HEADS):
            col = (g * GROUP_HEADS + hh) * HEAD_DIM
            o_ref[0, :, col:col + HEAD_DIM] = outs[g][hh] * alpha


def _dilated_sample(qkv_s, caches, layer):
    db = qkv_s.shape[0]
    row_cols = 2 * GROUP_COLS
    views, specs = [], []
    for g, c in enumerate(caches):
        dil = DIL_RATES[g]
        assert c.shape[2] == DIL_WINDOWS[g] and c.shape[2] // dil == A_BLOCK
        views.append(c.reshape(c.shape[0], db, A_BLOCK, dil * row_cols))
        specs.append(pl.BlockSpec((1, 1, A_BLOCK, row_cols), lambda b: (layer, b, 0, 0)))
    out = pl.pallas_call(
        functools.partial(_dil_sample_kernel, slopes=tuple(_alibi_slopes())),
        grid=(db,),
        in_specs=[pl.BlockSpec((1, 1, QKV_COLS), lambda b: (b, 0, 0))] + specs,
        out_specs=pl.BlockSpec((1, 1, A_HEADS * HEAD_DIM), lambda b: (b, 0, 0)),
        out_shape=jax.ShapeDtypeStruct((db, 1, A_HEADS * HEAD_DIM), F32),
        compiler_params=_params(("arbitrary",), 32),
        name="dilated_sample",
    )(qkv_s.reshape(db, 1, QKV_COLS), *views)
    return out.reshape(db, A_HEADS * HEAD_DIM)


def _stick_terms(z):
    lo = jnp.minimum(z, 0.0)
    hi = jnp.maximum(z, 0.0)
    l = jnp.log(1.0 + jnp.exp(lo - hi))
    return lo - l, hi + l


def _sb_prompt_kernel(bias_ref, q_ref, k_ref, v_ref, o_ref, *, tq, sub):
    h = pl.program_id(1)
    i = pl.program_id(2)
    n_sub = tq // sub
    bias = bias_ref[h]
    q = (q_ref[0] * (1.0 / math.sqrt(HEAD_DIM))).astype(BF16)
    nt = (((1,), (1,)), ((), ()))
    r = lax.broadcasted_iota(jnp.int32, (sub, sub), 0)
    c = lax.broadcasted_iota(jnp.int32, (sub, sub), 1)
    later_mat = jnp.where(r > c, 1.0, 0.0).astype(BF16)
    q_pos = lax.broadcasted_iota(jnp.int32, (tq, sub), 0)
    k_off = lax.broadcasted_iota(jnp.int32, (tq, sub), 1)

    def block(start, carry, acc, masked):
        k = k_ref[0, pl.ds(start, tq), :]
        v = v_ref[0, pl.ds(start, tq), :]
        z_all = lax.dot_general(q, k, nt, preferred_element_type=F32) + bias
        ws = [None] * n_sub
        for cidx in reversed(range(n_sub)):
            log_take, cost = _stick_terms(z_all[:, cidx * sub:(cidx + 1) * sub])
            if masked:
                causal = (k_off + cidx * sub) < q_pos
                cost = jnp.where(causal, cost, 0.0)
            later = jnp.dot(cost.astype(BF16), later_mat, preferred_element_type=F32)
            w = jnp.exp(log_take - later - carry)
            if masked:
                w = jnp.where(causal, w, 0.0)
            ws[cidx] = w.astype(BF16)
            carry = carry + jnp.sum(cost, axis=-1, keepdims=True)
        acc = acc + jnp.dot(jnp.concatenate(ws, axis=1), v, preferred_element_type=F32)
        return carry, acc

    carry, acc = block(pl.multiple_of(i * tq, tq), jnp.zeros((tq, 1), F32), jnp.zeros((tq, HEAD_DIM), F32), True)

    def body(t, state):
        start = pl.multiple_of((i - 1 - t) * tq, tq)
        return block(start, state[0], state[1], False)

    carry, acc = lax.fori_loop(0, i, body, (carry, acc))
    o_ref[0] = acc.astype(o_ref.dtype)


def _sb_prompt(q, kv_bf16, sb_bias, batch, seq, tq):
    n_heads = q.shape[1] // HEAD_DIM
    q3 = q.reshape(batch, seq, n_heads * HEAD_DIM)
    kv3 = kv_bf16.reshape(batch, seq, 2 * n_heads * HEAD_DIM)
    out = pl.pallas_call(
        functools.partial(_sb_prompt_kernel, tq=tq, sub=min(tq, 2 * LANES)),
        grid=(batch, n_heads, seq // tq),
        in_specs=[
            pl.BlockSpec(memory_space=pltpu.SMEM),
            pl.BlockSpec((1, tq, HEAD_DIM), lambda b, h, i: (b, i, h)),
            pl.BlockSpec((1, seq, HEAD_DIM), lambda b, h, i: (b, 0, h)),
            pl.BlockSpec((1, seq, HEAD_DIM), lambda b, h, i: (b, 0, n_heads + h)),
        ],
        out_specs=pl.BlockSpec((1, tq, HEAD_DIM), lambda b, h, i: (b, i, h)),
        out_shape=jax.ShapeDtypeStruct((batch, seq, n_heads * HEAD_DIM), BF16),
        compiler_params=_params(("arbitrary", "arbitrary", "arbitrary"), 40),
        name="sb_prompt",
    )(sb_bias, q3, kv3, kv3)
    return out.reshape(batch * seq, n_heads * HEAD_DIM)


def _sb_sample_kernel(pt_ref, qmat_ref, bias_ref, *rest, n_heads, pages_per_step):
    kv_refs = rest[:pages_per_step]
    o_ref, acc_ref, carry_ref = rest[pages_per_step:]
    p = pl.program_id(1)
    hd = n_heads * HEAD_DIM

    @pl.when(p == 0)
    def _():
        acc_ref[...] = jnp.zeros_like(acc_ref)
        carry_ref[...] = jnp.zeros_like(carry_ref)

    page = kv_refs[0].shape[1]
    r = lax.broadcasted_iota(jnp.int32, (page, page), 0)
    c = lax.broadcasted_iota(jnp.int32, (page, page), 1)
    later_mat = jnp.where(c > r, 1.0, 0.0).astype(BF16)
    carry = carry_ref[...]
    ws = []
    for kv_ref in kv_refs:
        z2 = jnp.dot(kv_ref[0, :, :hd], qmat_ref[0], preferred_element_type=F32) + bias_ref[...]
        log_take, cost = _stick_terms(z2)
        later = jnp.dot(later_mat, cost.astype(BF16), preferred_element_type=F32)
        w = jnp.exp(log_take - later - carry)
        ws.append(w.T[:MOD_ROWS].astype(BF16))
        carry = carry + jnp.sum(cost, axis=0, keepdims=True)
    carry_ref[...] = carry
    w_all = jnp.concatenate(ws, axis=1)
    v_all = jnp.concatenate([kv_ref[0, :, hd:] for kv_ref in kv_refs], axis=0)
    acc_ref[...] += jnp.dot(w_all, v_all, preferred_element_type=F32)

    @pl.when(p == pl.num_programs(1) - 1)
    def _():
        for hh in range(n_heads):
            sl = slice(hh * HEAD_DIM, (hh + 1) * HEAD_DIM)
            o_ref[0, :, sl] = acc_ref[hh:hh + 1, sl]


def _sb_sample(q, kv_pages, page_table, sb_bias, pages_per_step):
    db, hd = q.shape
    n_heads = hd // HEAD_DIM
    assert n_heads <= MOD_ROWS
    page = kv_pages.shape[1]
    n_pages = page_table.shape[1]
    assert n_pages % pages_per_step == 0
    head_of_row = jnp.arange(hd, dtype=jnp.int32) // HEAD_DIM
    sel = (head_of_row[:, None] == jnp.arange(LANES, dtype=jnp.int32)[None, :]).astype(F32)
    qmat = ((q * (1.0 / math.sqrt(HEAD_DIM)))[:, :, None] * sel[None]).astype(BF16)
    bias_row = jnp.zeros((1, LANES), F32).at[0, :n_heads].set(sb_bias.astype(F32))

    def page_spec(t):
        return pl.BlockSpec((1, page, 2 * hd),
                            lambda b, p, pt: (pt[b, n_pages - 1 - (p * pages_per_step + t)], 0, 0))

    out = pl.pallas_call(
        functools.partial(_sb_sample_kernel, n_heads=n_heads, pages_per_step=pages_per_step),
        grid_spec=pltpu.PrefetchScalarGridSpec(
            num_scalar_prefetch=1,
            grid=(db, n_pages // pages_per_step),
            in_specs=[
                pl.BlockSpec((1, hd, LANES), lambda b, p, pt: (b, 0, 0)),
                pl.BlockSpec((1, LANES), lambda b, p, pt: (0, 0)),
            ] + [page_spec(t) for t in range(pages_per_step)],
            out_specs=pl.BlockSpec((1, 1, hd), lambda b, p, pt: (b, 0, 0)),
            scratch_shapes=[pltpu.VMEM((MOD_ROWS, hd), F32), pltpu.VMEM((1, LANES), F32)],
        ),
        out_shape=jax.ShapeDtypeStruct((db, 1, hd), F32),
        compiler_params=_params(("arbitrary", "arbitrary"), 32),
        name="sb_sample",
    )(page_table, qmat, bias_row, *([kv_pages] * pages_per_step))
    return out.reshape(db, hd)


def kernel(x_prompt, x_sample, cache_win_g0, cache_win_g1, cache_win_g2, cache_kv, page_table, c_prompt, c_sample, w_mod, b_mod, norm_g, ffn_w_in, ffn_w_out, a_w_qkv, a_q_norm, a_k_norm, a_w_o, kv_norm, w_mod_kv, b_mod_kv, w_kv, sb_k_norm, b_w_q, b_q_norm, b_sb_bias, b_w_o):
    batch, seq, d = x_prompt.shape
    db, ds, _ = x_sample.shape
    depth = w_mod.shape[0]
    n_a = a_w_qkv.shape[0]
    d_ff = ffn_w_out.shape[2]
    n_b_heads = w_kv.shape[1] // (2 * HEAD_DIM)
    assert ds == 1 and db == 8 and db + batch <= MOD_ROWS
    caches = (cache_win_g0, cache_win_g1, cache_win_g2)

    c_all = jnp.zeros((MOD_ROWS, d), F32).at[:db].set(c_sample).at[db:db + batch].set(c_prompt)
    mod = _modulation(c_all, w_mod, b_mod)
    mod_kv = _modulation(c_all, w_mod_kv[None], b_mod_kv[None])
    mod_s = mod[:, :, :db].reshape(depth * N_MOD, db, d)
    mod_p = mod[:, :, db:db + batch].reshape(depth * N_MOD * batch, 1, d)
    modkv_s = mod_kv[:, :, :db].reshape(2, db, d)
    modkv_p = mod_kv[:, :, db:db + batch].reshape(2 * batch, 1, d)

    tm_p = _pick_tile(seq, 1024)
    rows_p = _Rows(batch * seq, batch, 1, tm_p)
    rows_s = _Rows(db, 1, db, db)
    paths = (
        dict(rows=rows_p, mod=mod_p, modkv=modkv_p, tf=_pick_tile(d_ff, 256)),
        dict(rows=rows_s, mod=mod_s, modkv=modkv_s, tf=_pick_tile(d_ff, 512)),
    )

    norm3 = norm_g.reshape(depth * 3, 1, d)
    kvn3 = kv_norm.reshape(1, 1, d)
    w_in = ffn_w_in.reshape(depth * 2, d, 2 * d_ff)
    w_out = ffn_w_out.reshape(depth * 2, d_ff, d)
    w_kv3 = w_kv[None]

    a_gain = [jnp.concatenate([jnp.tile(a_q_norm[l], A_HEADS), jnp.tile(a_k_norm[l], A_HEADS),
                               jnp.ones((A_HEADS * HEAD_DIM,), F32)])[None] for l in range(n_a)]
    kv_gain = jnp.concatenate([jnp.tile(sb_k_norm, n_b_heads), jnp.ones((n_b_heads * HEAD_DIM,), F32)])[None]
    b_gain = [jnp.tile(b_q_norm[j], n_b_heads)[None] for j in range(depth - n_a)]

    h_p = x_prompt.reshape(batch * seq, d)
    h_s = x_sample.reshape(db, d)
    tn_qkv = GROUP_COLS
    tn_d = _pick_tile(d, 512)
    win_p = [[] for _ in range(N_GROUPS)]
    win_s = [[] for _ in range(N_GROUPS)]
    kv_p = kv_s = kv_p_bf16 = None
    kv_pages = cache_kv.astype(BF16).reshape(cache_kv.shape[0], cache_kv.shape[1], 2 * n_b_heads * HEAD_DIM)
    n_pages = page_table.shape[1]
    pages_per_step = 4 if n_pages % 4 == 0 else 1

    def ffn(h, path, l, which):
        return _ffn(h, path["rows"], path["mod"], (l * N_MOD + 6 * which), norm3, l * 3 + 2 * which,
                    w_in, w_out, l * 2 + which, path["tf"])

    for l in range(depth):
        h_p = ffn(h_p, paths[0], l, 0)
        h_s = ffn(h_s, paths[1], l, 0)
        if l < n_a:
            qkv_p = _proj(h_p, rows_p, mod_p, l * N_MOD + 3, norm3, l * 3 + 1, a_w_qkv, l, a_gain[l],
                          2 * A_HEADS * HEAD_DIM, tn_qkv)
            qkv_s = _proj(h_s, rows_s, mod_s, l * N_MOD + 3, norm3, l * 3 + 1, a_w_qkv, l, a_gain[l],
                          2 * A_HEADS * HEAD_DIM, tn_qkv)
            outs, lses = [], []
            for g in range(N_GROUPS):
                o_g, lse_g = _dilated_prompt(qkv_p, batch, seq, g)
                outs.append(o_g)
                lses.append(lse_g)
            h_p = _mix_out(outs, lses, a_w_o, l, h_p, rows_p, mod_p, l * N_MOD + 5, tn_d)
            mixed_s = _dilated_sample(qkv_s, caches, l)
            h_s = _out_proj(mixed_s, a_w_o, l, h_s, rows_s, mod_s, l * N_MOD + 5, tn_d)

            qkv_p5 = qkv_p.reshape(batch, seq, 3, A_HEADS, HEAD_DIM)
            qkv_s5 = qkv_s.reshape(db, 1, 3, A_HEADS, HEAD_DIM)
            for g in range(N_GROUPS):
                hs = slice(g * GROUP_HEADS, (g + 1) * GROUP_HEADS)
                keep = min(DIL_WINDOWS[g], seq)
                win_p[g].append(qkv_p5[:, seq - keep:, 1:3, hs])
                buf = caches[g][l]
                win_s[g].append(jnp.concatenate([buf[:, 1:], qkv_s5[:, :, 1:3, hs]], axis=1))
        else:
            j = l - n_a
            q_p = _proj(h_p, rows_p, mod_p, l * N_MOD + 3, norm3, l * 3 + 1, b_w_q, j, b_gain[j],
                        n_b_heads * HEAD_DIM, tn_d)
            q_s = _proj(h_s, rows_s, mod_s, l * N_MOD + 3, norm3, l * 3 + 1, b_w_q, j, b_gain[j],
                        n_b_heads * HEAD_DIM, tn_d)
            o_p = _sb_prompt(q_p, kv_p_bf16, b_sb_bias[j], batch, seq, _pick_tile(seq, 512))
            h_p = _out_proj(o_p, b_w_o, j, h_p, rows_p, mod_p, l * N_MOD + 5, tn_d)
            o_s = _sb_sample(q_s, kv_pages, page_table, b_sb_bias[j], pages_per_step)
            h_s = _out_proj(o_s, b_w_o, j, h_s, rows_s, mod_s, l * N_MOD + 5, tn_d)
        h_p = ffn(h_p, paths[0], l, 1)
        h_s = ffn(h_s, paths[1], l, 1)
        if l == n_a - 1:
            kv_p, kv_p_bf16 = _proj(h_p, rows_p, modkv_p, 0, kvn3, 0, w_kv3, 0, kv_gain,
                                    n_b_heads * HEAD_DIM, tn_d, with_bf16=True)
            kv_s = _proj(h_s, rows_s, modkv_s, 0, kvn3, 0, w_kv3, 0, kv_gain, n_b_heads * HEAD_DIM, tn_d)

    y_p = h_p.reshape(batch, seq, d)
    y_s = h_s.reshape(db, 1, d)
    win_p = [jnp.stack(w, axis=0) for w in win_p]
    win_s = [jnp.stack(w, axis=0) for w in win_s]
    kv_p = kv_p.reshape(batch, seq, 2, n_b_heads, HEAD_DIM)
    kv_s = kv_s.reshape(db, 1, 2, n_b_heads, HEAD_DIM)
    return (y_p, y_s, win_p[0], win_p[1], win_p[2], kv_p, win_s[0], win_s[1], win_s[2], kv_s)
```

```python
import functools
import math

import numpy as np
import jax
import jax.numpy as jnp
from jax import lax
from jax.experimental import pallas as pl
from jax.experimental.pallas import tpu as pltpu

F32 = jnp.float32
BF16 = jnp.bfloat16

EPS = 1e-6
HEAD_DIM = 128
LANES = 128
N_MOD = 9
DIL_WINDOWS = (128, 512, 2048)
DIL_RATES = (1, 4, 16)
N_GROUPS = 3
GROUP_HEADS = 5
A_HEADS = N_GROUPS * GROUP_HEADS
A_BLOCK = 128
GROUP_COLS = GROUP_HEADS * HEAD_DIM
QKV_COLS = 3 * A_HEADS * HEAD_DIM
NEG_BIG = -1e30
MIB = 1024 * 1024
MOD_ROWS = 16


def _alibi_slopes():
    return [float(2.0 ** (-8.0 * (i + 1) / A_HEADS)) for i in range(A_HEADS)]


def _params(semantics, vmem_mib):
    return pltpu.CompilerParams(dimension_semantics=semantics, vmem_limit_bytes=vmem_mib * MIB)


def _pick_tile(n, preferred):
    if n <= preferred:
        return n
    t = (preferred // LANES) * LANES
    while t >= LANES:
        if n % t == 0:
            return t
        t -= LANES
    return n


class _Rows:
    def __init__(self, n_rows, groups, cond_rows, tm):
        self.n_rows = n_rows
        self.groups = groups
        self.cond_rows = cond_rows
        self.tm = tm
        self.tiles_per_group = (n_rows // groups) // tm
        assert self.tiles_per_group * tm * groups == n_rows

    def cond_spec(self, chunk, width, col_of=None):
        g, tpg = self.groups, self.tiles_per_group
        if col_of is None:
            return pl.BlockSpec((1, self.cond_rows, width), lambda i, j: (chunk * g + i // tpg, 0, 0))
        return pl.BlockSpec((1, self.cond_rows, width), lambda i, j: (chunk * g + i // tpg, 0, col_of(j)))


def _norm_mod(x, gain, shift, scale):
    ms = jnp.mean(x * x, axis=-1, keepdims=True)
    y = x * lax.rsqrt(ms + EPS) * gain
    return y * (1.0 + scale) + shift


def _mod_kernel(c_ref, w_ref, b_ref, o_ref):
    c = c_ref[...]
    a = (c * jax.nn.sigmoid(c)).astype(BF16)
    o_ref[0, 0] = jnp.dot(a, w_ref[0].astype(BF16), preferred_element_type=F32) + b_ref[0]


def _modulation(c_all, w, b):
    n_layers, d, n_out = w.shape
    n_chunks = n_out // d
    tn = _pick_tile(d, 1024)
    per_chunk = d // tn
    return pl.pallas_call(
        _mod_kernel,
        grid=(n_layers, n_out // tn),
        in_specs=[
            pl.BlockSpec((MOD_ROWS, d), lambda l, j: (0, 0)),
            pl.BlockSpec((1, d, tn), lambda l, j: (l, 0, j)),
            pl.BlockSpec((1, 1, tn), lambda l, j: (l, 0, j)),
        ],
        out_specs=pl.BlockSpec((1, 1, MOD_ROWS, tn), lambda l, j: (l, j // per_chunk, 0, j % per_chunk)),
        out_shape=jax.ShapeDtypeStruct((n_layers, n_chunks, MOD_ROWS, d), F32),
        compiler_params=_params(("arbitrary", "arbitrary"), 40),
        name="modulation",
    )(c_all, w, b.reshape(n_layers, 1, n_out))


def _ffn_kernel(x_ref, sh_ref, sc_ref, gt_ref, ng_ref, wg_ref, wu_ref, wo_ref, o_ref, xn_ref):
    f = pl.program_id(1)

    @pl.when(f == 0)
    def _():
        u = _norm_mod(x_ref[...], ng_ref[0], sh_ref[0], sc_ref[0])
        xn_ref[...] = u.astype(BF16)
        o_ref[...] = jnp.zeros_like(o_ref)

    xn = xn_ref[...]
    gate = jnp.dot(xn, wg_ref[...].astype(BF16), preferred_element_type=F32)
    up = jnp.dot(xn, wu_ref[...].astype(BF16), preferred_element_type=F32)
    act = (gate * jax.nn.sigmoid(gate) * up).astype(BF16)
    o_ref[...] += jnp.dot(act, wo_ref[...].astype(BF16), preferred_element_type=F32)

    @pl.when(f == pl.num_programs(1) - 1)
    def _():
        o_ref[...] = x_ref[...] + 0.5 * gt_ref[0] * o_ref[...]


def _ffn(h, rows, mod, chunk0, norm_g, norm_idx, w_in, w_out, w_idx, tf):
    m, d = h.shape
    d_ff = w_out.shape[1]
    n_f = d_ff // tf
    tm = rows.tm
    return pl.pallas_call(
        _ffn_kernel,
        grid=(m // tm, n_f),
        in_specs=[
            pl.BlockSpec((tm, d), lambda i, f: (i, 0)),
            rows.cond_spec(chunk0, d),
            rows.cond_spec(chunk0 + 1, d),
            rows.cond_spec(chunk0 + 2, d),
            pl.BlockSpec((1, 1, d), lambda i, f: (norm_idx, 0, 0)),
            pl.BlockSpec((None, d, tf), lambda i, f: (w_idx, 0, f)),
            pl.BlockSpec((None, d, tf), lambda i, f: (w_idx, 0, n_f + f)),
            pl.BlockSpec((None, tf, d), lambda i, f: (w_idx, f, 0)),
        ],
        out_specs=pl.BlockSpec((tm, d), lambda i, f: (i, 0)),
        out_shape=jax.ShapeDtypeStruct((m, d), F32),
        scratch_shapes=[pltpu.VMEM((tm, d), BF16)],
        compiler_params=_params(("arbitrary", "arbitrary"), 58),
        name="ffn",
    )(h, mod, mod, mod, norm_g, w_in, w_in, w_out)


def _proj_kernel(x_ref, sh_ref, sc_ref, ng_ref, w_ref, hg_ref, *rest, n_norm_tiles, with_bf16):
    if with_bf16:
        o_ref, obf_ref, xn_ref = rest
    else:
        o_ref, xn_ref = rest
        obf_ref = None
    j = pl.program_id(1)

    @pl.when(j == 0)
    def _():
        xn_ref[...] = _norm_mod(x_ref[...], ng_ref[0], sh_ref[0], sc_ref[0]).astype(BF16)

    acc = jnp.dot(xn_ref[...], w_ref[...].astype(BF16), preferred_element_type=F32)
    tn = acc.shape[1]

    def store(val, sl):
        o_ref[:, sl] = val
        if obf_ref is not None:
            obf_ref[:, sl] = val.astype(BF16)

    @pl.when(j < n_norm_tiles)
    def _():
        for t in range(tn // HEAD_DIM):
            sl = slice(t * HEAD_DIM, (t + 1) * HEAD_DIM)
            a = acc[:, sl]
            ms = jnp.mean(a * a, axis=-1, keepdims=True)
            store(a * lax.rsqrt(ms + EPS) * hg_ref[:, sl], sl)

    @pl.when(j >= n_norm_tiles)
    def _():
        store(acc, slice(None))


def _proj(h, rows, mod, chunk0, norm_g, norm_idx, w, w_idx, head_gain, n_norm_cols, tn, with_bf16=False):
    m, d = h.shape
    n = w.shape[-1]
    tm = rows.tm
    assert n % tn == 0 and n_norm_cols % tn == 0 and tn % HEAD_DIM == 0
    out_shape = [jax.ShapeDtypeStruct((m, n), F32)]
    out_specs = [pl.BlockSpec((tm, tn), lambda i, j: (i, j))]
    if with_bf16:
        out_shape.append(jax.ShapeDtypeStruct((m, n), BF16))
        out_specs.append(pl.BlockSpec((tm, tn), lambda i, j: (i, j)))
    res = pl.pallas_call(
        functools.partial(_proj_kernel, n_norm_tiles=n_norm_cols // tn, with_bf16=with_bf16),
        grid=(m // tm, n // tn),
        in_specs=[
            pl.BlockSpec((tm, d), lambda i, j: (i, 0)),
            rows.cond_spec(chunk0, d),
            rows.cond_spec(chunk0 + 1, d),
            pl.BlockSpec((1, 1, d), lambda i, j: (norm_idx, 0, 0)),
            pl.BlockSpec((None, d, tn), lambda i, j: (w_idx, 0, j)),
            pl.BlockSpec((1, tn), lambda i, j: (0, j)),
        ],
        out_specs=out_specs,
        out_shape=out_shape,
        scratch_shapes=[pltpu.VMEM((tm, d), BF16)],
        compiler_params=_params(("arbitrary", "arbitrary"), 56),
        name="proj",
    )(h, mod, mod, norm_g, w, head_gain)
    return res if with_bf16 else res[0]


def _out_kernel(x_ref, w_ref, h_ref, gt_ref, o_ref):
    y = jnp.dot(x_ref[...].astype(BF16), w_ref[...].astype(BF16), preferred_element_type=F32)
    o_ref[...] = h_ref[...] + gt_ref[0] * y


def _out_proj(x, w, w_idx, h, rows, mod, chunk, tn):
    m, k = x.shape
    n = w.shape[-1]
    tm = rows.tm
    return pl.pallas_call(
        _out_kernel,
        grid=(m // tm, n // tn),
        in_specs=[
            pl.BlockSpec((tm, k), lambda i, j: (i, 0)),
            pl.BlockSpec((None, k, tn), lambda i, j: (w_idx, 0, j)),
            pl.BlockSpec((tm, tn), lambda i, j: (i, j)),
            rows.cond_spec(chunk, tn, col_of=lambda j: j),
        ],
        out_specs=pl.BlockSpec((tm, tn), lambda i, j: (i, j)),
        out_shape=jax.ShapeDtypeStruct((m, n), F32),
        compiler_params=_params(("arbitrary", "arbitrary"), 48),
        name="out_proj",
    )(x, w, h, mod)


def _dil_kernel(slopes_ref, q_ref, kc_ref, kp_ref, vc_ref, vp_ref, o_ref, lse_ref, *, dil, heads, head0):
    n = pl.program_id(1)
    hb = pl.program_id(2)
    scale = 1.0 / math.sqrt(HEAD_DIM)
    qi = lax.broadcasted_iota(jnp.int32, (A_BLOCK, A_BLOCK), 0)
    ki = lax.broadcasted_iota(jnp.int32, (A_BLOCK, A_BLOCK), 1)
    step_c = qi - ki
    step_p = step_c + A_BLOCK
    valid_c = step_c >= 0
    valid_p = jnp.logical_and(step_p <= A_BLOCK, n > 0)
    dist_c = (step_c * dil).astype(F32)
    dist_p = (step_p * dil).astype(F32)
    nt = (((1,), (1,)), ((), ()))
    for hh in range(heads):
        slope = slopes_ref[head0 + hb * heads + hh]
        bias_c = jnp.where(valid_c, -slope * dist_c, NEG_BIG)
        bias_p = jnp.where(valid_p, -slope * dist_p, NEG_BIG)
        sl = slice(hh * HEAD_DIM, (hh + 1) * HEAD_DIM)
        for res in range(dil):
            rows = pl.ds(res, A_BLOCK, stride=dil) if dil > 1 else slice(None)
            q = (q_ref[rows, sl] * scale).astype(BF16)
            s_c = lax.dot_general(q, kc_ref[rows, sl].astype(BF16), nt, preferred_element_type=F32) + bias_c
            s_p = lax.dot_general(q, kp_ref[rows, sl].astype(BF16), nt, preferred_element_type=F32) + bias_p
            mx = jnp.maximum(jnp.max(s_c, axis=-1, keepdims=True), jnp.max(s_p, axis=-1, keepdims=True))
            p_c = jnp.exp(s_c - mx)
            p_p = jnp.exp(s_p - mx)
            den = jnp.sum(p_c, axis=-1, keepdims=True) + jnp.sum(p_p, axis=-1, keepdims=True)
            o = jnp.dot(p_c.astype(BF16), vc_ref[rows, sl].astype(BF16), preferred_element_type=F32)
            o = o + jnp.dot(p_p.astype(BF16), vp_ref[rows, sl].astype(BF16), preferred_element_type=F32)
            o_ref[rows, sl] = o / den
            lse_ref[rows, sl] = jnp.broadcast_to(mx + jnp.log(den), (A_BLOCK, HEAD_DIM))


def _dilated_prompt(qkv, slopes, batch, seq, g, heads):
    dil = DIL_RATES[g]
    span = dil * A_BLOCK
    assert DIL_WINDOWS[g] // dil == A_BLOCK and seq % span == 0 and GROUP_HEADS % heads == 0
    nb = seq // span
    cw = heads * HEAD_DIM
    q_col = g * GROUP_COLS // cw
    k_col = (A_HEADS * HEAD_DIM + g * GROUP_COLS) // cw
    v_col = (2 * A_HEADS * HEAD_DIM + g * GROUP_COLS) // cw
    blk = (span, cw)

    def cur(col):
        return pl.BlockSpec(blk, lambda b, n, hb: (b * nb + n, col + hb))

    def prev(col):
        return pl.BlockSpec(blk, lambda b, n, hb: (b * nb + jnp.maximum(n - 1, 0), col + hb))

    out_spec = pl.BlockSpec(blk, lambda b, n, hb: (b * nb + n, hb))
    return pl.pallas_call(
        functools.partial(_dil_kernel, dil=dil, heads=heads, head0=g * GROUP_HEADS),
        grid=(batch, nb, GROUP_HEADS // heads),
        in_specs=[pl.BlockSpec(memory_space=pltpu.SMEM), cur(q_col), cur(k_col), prev(k_col), cur(v_col), prev(v_col)],
        out_specs=[out_spec, out_spec],
        out_shape=[jax.ShapeDtypeStruct((batch * seq, GROUP_COLS), F32)] * 2,
        compiler_params=_params(("arbitrary", "arbitrary", "arbitrary"), 40),
        name="dilated_prompt",
    )(slopes, qkv, qkv, qkv, qkv, qkv)


def _mix_out_kernel(o0_ref, o1_ref, o2_ref, l0_ref, l1_ref, l2_ref, w_ref, h_ref, gt_ref, out_ref, mix_ref):
    j = pl.program_id(1)

    @pl.when(j == 0)
    def _():
        ls = []
        for l_ref in (l0_ref, l1_ref, l2_ref):
            heads = [l_ref[:, hh * HEAD_DIM:(hh + 1) * HEAD_DIM] for hh in range(GROUP_HEADS)]
            htop = functools.reduce(jnp.maximum, heads)
            htot = functools.reduce(lambda a, b: a + b, [jnp.exp(l - htop) for l in heads])
            ls.append(htop + jnp.log(htot) - math.log(GROUP_HEADS))
        top = jnp.maximum(jnp.maximum(ls[0], ls[1]), ls[2])
        es = [jnp.exp(l - top) for l in ls]
        den = es[0] + es[1] + es[2]
        for g, o_ref in enumerate((o0_ref, o1_ref, o2_ref)):
            alpha = es[g] / den
            for hh in range(GROUP_HEADS):
                src = slice(hh * HEAD_DIM, (hh + 1) * HEAD_DIM)
                dst = slice(g * GROUP_COLS + hh * HEAD_DIM, g * GROUP_COLS + (hh + 1) * HEAD_DIM)
                mix_ref[:, dst] = (o_ref[:, src] * alpha).astype(BF16)

    y = jnp.dot(mix_ref[...], w_ref[...].astype(BF16), preferred_element_type=F32)
    out_ref[...] = h_ref[...] + gt_ref[0] * y


def _mix_out(outs, lses, w, w_idx, h, rows, mod, chunk, tn):
    m = h.shape[0]
    n = w.shape[-1]
    k = w.shape[-2]
    tm = rows.tm
    o_spec = pl.BlockSpec((tm, GROUP_COLS), lambda i, j: (i, 0))
    return pl.pallas_call(
        _mix_out_kernel,
        grid=(m // tm, n // tn),
        in_specs=[
            o_spec, o_spec, o_spec, o_spec, o_spec, o_spec,
            pl.BlockSpec((None, k, tn), lambda i, j: (w_idx, 0, j)),
            pl.BlockSpec((tm, tn), lambda i, j: (i, j)),
            rows.cond_spec(chunk, tn, col_of=lambda j: j),
        ],
        out_specs=pl.BlockSpec((tm, tn), lambda i, j: (i, j)),
        out_shape=jax.ShapeDtypeStruct((m, n), F32),
        scratch_shapes=[pltpu.VMEM((tm, k), BF16)],
        compiler_params=_params(("arbitrary", "arbitrary"), 58),
        name="mix_out",
    )(*outs, *lses, w, h, mod)


def _dil_sample_kernel(qkv_ref, c0_ref, c1_ref, c2_ref, o_ref, *, slopes):
    scale = 1.0 / math.sqrt(HEAD_DIM)
    nt = (((1,), (1,)), ((), ()))
    kbase = A_HEADS * HEAD_DIM
    vbase = 2 * A_HEADS * HEAD_DIM
    steps = (A_BLOCK - lax.broadcasted_iota(jnp.int32, (1, A_BLOCK), 1)).astype(F32)
    outs, glses = [], []
    for g, c_ref in enumerate((c0_ref, c1_ref, c2_ref)):
        dil = DIL_RATES[g]
        head_outs, head_lses = [], []
        for hh in range(GROUP_HEADS):
            col = (g * GROUP_HEADS + hh) * HEAD_DIM
            q = qkv_ref[0, :, col:col + HEAD_DIM]
            k_new = qkv_ref[0, :, kbase + col:kbase + col + HEAD_DIM]
            v_new = qkv_ref[0, :, vbase + col:vbase + col + HEAD_DIM]
            k_buf = c_ref[0, 0, :, hh * HEAD_DIM:(hh + 1) * HEAD_DIM]
            v_buf = c_ref[0, 0, :, GROUP_COLS + hh * HEAD_DIM:GROUP_COLS + (hh + 1) * HEAD_DIM]
            q8 = jnp.broadcast_to(q * scale, (8, HEAD_DIM)).astype(BF16)
            s_buf = lax.dot_general(q8, k_buf.astype(BF16), nt, preferred_element_type=F32)[0:1]
            s_buf = s_buf - slopes[g * GROUP_HEADS + hh] * dil * steps
            s_new = jnp.sum((q * scale).astype(BF16).astype(F32) * k_new.astype(BF16).astype(F32),
                            axis=-1, keepdims=True)
            mx = jnp.maximum(jnp.max(s_buf, axis=-1, keepdims=True), s_new)
            p_buf = jnp.exp(s_buf - mx)
            p_new = jnp.exp(s_new - mx)
            den = jnp.sum(p_buf, axis=-1, keepdims=True) + p_new
            p8 = jnp.broadcast_to(p_buf, (8, A_BLOCK)).astype(BF16)
            o = jnp.dot(p8, v_buf.astype(BF16), preferred_element_type=F32)[0:1]
            o = (o + p_new.astype(BF16).astype(F32) * v_new.astype(BF16).astype(F32)) / den
            head_outs.append(o)
            head_lses.append(mx + jnp.log(den))
        top = functools.reduce(jnp.maximum, head_lses)
        tot = functools.reduce(lambda a, b: a + b, [jnp.exp(l - top) for l in head_lses])
        glses.append(top + jnp.log(tot) - math.log(GROUP_HEADS))
        outs.append(head_outs)
    top = functools.reduce(jnp.maximum, glses)
    es = [jnp.exp(l - top) for l in glses]
    den = es[0] + es[1] + es[2]
    for g in range(N_GROUPS):
        alpha = es[g] / den
        for hh in range(GROUP_HEADS):
            col = (g * GROUP_HEADS + hh) * HEAD_DIM
            o_ref[0, :, col:col + HEAD_DIM] = outs[g][hh] * alpha


def _dilated_sample(qkv_s, caches, layer):
    db = qkv_s.shape[0]
    row_cols = 2 * GROUP_COLS
    views, specs = [], []
    for g, c in enumerate(caches):
        dil = DIL_RATES[g]
        assert c.shape[2] == DIL_WINDOWS[g] and c.shape[2] // dil == A_BLOCK
        views.append(c[:, :, ::dil].reshape(c.shape[0], db, A_BLOCK, row_cols))
        specs.append(pl.BlockSpec((1, 1, A_BLOCK, row_cols), lambda b: (layer, b, 0, 0)))
    out = pl.pallas_call(
        functools.partial(_dil_sample_kernel, slopes=tuple(_alibi_slopes())),
        grid=(db,),
        in_specs=[pl.BlockSpec((1, 1, QKV_COLS), lambda b: (b, 0, 0))] + specs,
        out_specs=pl.BlockSpec((1, 1, A_HEADS * HEAD_DIM), lambda b: (b, 0, 0)),
        out_shape=jax.ShapeDtypeStruct((db, 1, A_HEADS * HEAD_DIM), F32),
        compiler_params=_params(("arbitrary",), 32),
        name="dilated_sample",
    )(qkv_s.reshape(db, 1, QKV_COLS), *views)
    return out.reshape(db, A_HEADS * HEAD_DIM)


def _stick_terms(z):
    lo = jnp.minimum(z, 0.0)
    hi = jnp.maximum(z, 0.0)
    l = jnp.log(1.0 + jnp.exp(lo - hi))
    return lo - l, hi + l


def _sb_prompt_kernel(bias_ref, q_ref, k_ref, v_ref, o_ref, *, tq, sub):
    h = pl.program_id(1)
    i = pl.program_id(2)
    n_sub = tq // sub
    bias = bias_ref[h]
    q = (q_ref[0] * (1.0 / math.sqrt(HEAD_DIM))).astype(BF16)
    nt = (((1,), (1,)), ((), ()))
    r = lax.broadcasted_iota(jnp.int32, (sub, sub), 0)
    c = lax.broadcasted_iota(jnp.int32, (sub, sub), 1)
    later_mat = jnp.where(r > c, 1.0, 0.0).astype(BF16)
    q_pos = lax.broadcasted_iota(jnp.int32, (tq, sub), 0)
    k_off = lax.broadcasted_iota(jnp.int32, (tq, sub), 1)

    def block(start, carry, acc, masked):
        k = k_ref[0, pl.ds(start, tq), :]
        v = v_ref[0, pl.ds(start, tq), :]
        z_all = lax.dot_general(q, k, nt, preferred_element_type=F32) + bias
        ws = [None] * n_sub
        for cidx in reversed(range(n_sub)):
            log_take, cost = _stick_terms(z_all[:, cidx * sub:(cidx + 1) * sub])
            if masked:
                causal = (k_off + cidx * sub) < q_pos
                cost = jnp.where(causal, cost, 0.0)
            later = jnp.dot(cost.astype(BF16), later_mat, preferred_element_type=F32)
            w = jnp.exp(log_take - later - carry)
            if masked:
                w = jnp.where(causal, w, 0.0)
            ws[cidx] = w.astype(BF16)
            carry = carry + jnp.sum(cost, axis=-1, keepdims=True)
        acc = acc + jnp.dot(jnp.concatenate(ws, axis=1), v, preferred_element_type=F32)
        return carry, acc

    carry, acc = block(pl.multiple_of(i * tq, tq), jnp.zeros((tq, 1), F32), jnp.zeros((tq, HEAD_DIM), F32), True)

    def body(t, state):
        start = pl.multiple_of((i - 1 - t) * tq, tq)
        return block(start, state[0], state[1], False)

    carry, acc = lax.fori_loop(0, i, body, (carry, acc))
    o_ref[0] = acc.astype(o_ref.dtype)


def _sb_prompt(q, kv_bf16, sb_bias, batch, seq, tq):
    n_heads = q.shape[1] // HEAD_DIM
    q3 = q.reshape(batch, seq, n_heads * HEAD_DIM)
    kv3 = kv_bf16.reshape(batch, seq, 2 * n_heads * HEAD_DIM)
    out = pl.pallas_call(
        functools.partial(_sb_prompt_kernel, tq=tq, sub=min(tq, 2 * LANES)),
        grid=(batch, n_heads, seq // tq),
        in_specs=[
            pl.BlockSpec(memory_space=pltpu.SMEM),
            pl.BlockSpec((1, tq, HEAD_DIM), lambda b, h, i: (b, i, h)),
            pl.BlockSpec((1, seq, HEAD_DIM), lambda b, h, i: (b, 0, h)),
            pl.BlockSpec((1, seq, HEAD_DIM), lambda b, h, i: (b, 0, n_heads + h)),
        ],
        out_specs=pl.BlockSpec((1, tq, HEAD_DIM), lambda b, h, i: (b, i, h)),
        out_shape=jax.ShapeDtypeStruct((batch, seq, n_heads * HEAD_DIM), BF16),
        compiler_params=_params(("arbitrary", "arbitrary", "arbitrary"), 40),
        name="sb_prompt",
    )(sb_bias, q3, kv3, kv3)
    return out.reshape(batch * seq, n_heads * HEAD_DIM)


def _kv_gather_kernel(pt_ref, *refs, n_heads, pages_per_step):
    x_refs, o_ref = refs[:pages_per_step], refs[pages_per_step]
    page = o_ref.shape[1]
    keys = 2 * LANES // n_heads
    n = keys * n_heads
    i = lax.broadcasted_iota(jnp.int32, (n, n), 0)
    j = lax.broadcasted_iota(jnp.int32, (n, n), 1)
    perm = jnp.where(jnp.logical_and(i // keys == j % n_heads, i % keys == j // n_heads), 1.0, 0.0).astype(BF16)
    for t, x_ref in enumerate(x_refs):
        for c in range(page // keys):
            k_rows = x_ref[c * keys:(c + 1) * keys, 0:n_heads, :].reshape(n, HEAD_DIM)
            v_rows = x_ref[c * keys:(c + 1) * keys, n_heads:2 * n_heads, :].reshape(n, HEAD_DIM)
            both = jnp.concatenate([k_rows, v_rows], axis=1).astype(BF16)
            y = jnp.dot(perm, both, preferred_element_type=F32)
            for hh in range(n_heads):
                blk = y[hh * keys:(hh + 1) * keys].astype(BF16)
                o_ref[t, c * keys:(c + 1) * keys, hh * HEAD_DIM:(hh + 1) * HEAD_DIM] = blk[:, :HEAD_DIM]
                o_ref[t, c * keys:(c + 1) * keys, (n_heads + hh) * HEAD_DIM:(n_heads + hh + 1) * HEAD_DIM] = (
                    blk[:, HEAD_DIM:])


def _kv_gather(cache_kv, page_table, pages_per_step):
    n_phys, page, _, n_heads, _ = cache_kv.shape
    db, n_pages = page_table.shape
    slabs = 2 * n_heads
    assert (2 * LANES) % n_heads == 0 and page % (2 * LANES // n_heads) == 0 and n_heads % 8 == 0
    rows = cache_kv.reshape(n_phys, page, slabs, HEAD_DIM)
    steps = n_pages // pages_per_step

    def page_spec(t):
        return pl.BlockSpec((None, page, slabs, HEAD_DIM), lambda b, p, pt: (pt[b, p * pages_per_step + t], 0, 0, 0))

    return pl.pallas_call(
        functools.partial(_kv_gather_kernel, n_heads=n_heads, pages_per_step=pages_per_step),
        grid_spec=pltpu.PrefetchScalarGridSpec(
            num_scalar_prefetch=1,
            grid=(db, steps),
            in_specs=[page_spec(t) for t in range(pages_per_step)],
            out_specs=pl.BlockSpec((pages_per_step, page, slabs * HEAD_DIM), lambda b, p, pt: (b * steps + p, 0, 0)),
        ),
        out_shape=jax.ShapeDtypeStruct((db * n_pages, page, slabs * HEAD_DIM), BF16),
        compiler_params=_params(("arbitrary", "arbitrary"), 32),
        name="kv_gather",
    )(page_table, *([rows] * pages_per_step))


def _sb_sample_kernel(qmat_ref, bias_ref, kv_ref, o_ref, acc_ref, carry_ref, *, n_heads):
    p = pl.program_id(1)
    hd = n_heads * HEAD_DIM
    pages_per_step, page = kv_ref.shape[0], kv_ref.shape[1]

    @pl.when(p == 0)
    def _():
        acc_ref[...] = jnp.zeros_like(acc_ref)
        carry_ref[...] = jnp.zeros_like(carry_ref)

    r = lax.broadcasted_iota(jnp.int32, (page, page), 0)
    c = lax.broadcasted_iota(jnp.int32, (page, page), 1)
    later_mat = jnp.where(c > r, 1.0, 0.0).astype(BF16)
    carry = carry_ref[...]
    ws = [None] * pages_per_step
    for t in reversed(range(pages_per_step)):
        z = jnp.dot(kv_ref[t, :, :hd], qmat_ref[0], preferred_element_type=F32) + bias_ref[...]
        log_take, cost = _stick_terms(z)
        later = jnp.dot(later_mat, cost.astype(BF16), preferred_element_type=F32)
        w = jnp.exp(log_take - later - carry)
        ws[t] = w.T[:MOD_ROWS].astype(BF16)
        carry = carry + jnp.sum(cost, axis=0, keepdims=True)
    carry_ref[...] = carry
    w_all = jnp.concatenate(ws, axis=1)
    v_all = jnp.concatenate([kv_ref[t, :, hd:] for t in range(pages_per_step)], axis=0)
    acc_ref[...] += jnp.dot(w_all, v_all, preferred_element_type=F32)

    @pl.when(p == pl.num_programs(1) - 1)
    def _():
        for hh in range(n_heads):
            sl = slice(hh * HEAD_DIM, (hh + 1) * HEAD_DIM)
            o_ref[0, :, sl] = acc_ref[hh:hh + 1, sl]


def _sb_sample(q, kv_pages, sb_bias, pages_per_step):
    db, hd = q.shape
    n_heads = hd // HEAD_DIM
    assert n_heads <= MOD_ROWS
    page = kv_pages.shape[1]
    steps = kv_pages.shape[0] // (db * pages_per_step)
    head_of_row = jnp.arange(hd, dtype=jnp.int32) // HEAD_DIM
    sel = (head_of_row[:, None] == jnp.arange(LANES, dtype=jnp.int32)[None, :]).astype(F32)
    qmat = ((q * (1.0 / math.sqrt(HEAD_DIM)))[:, :, None] * sel[None]).astype(BF16)
    bias_row = jnp.zeros((1, LANES), F32).at[0, :n_heads].set(sb_bias.astype(F32))
    out = pl.pallas_call(
        functools.partial(_sb_sample_kernel, n_heads=n_heads),
        grid=(db, steps),
        in_specs=[
            pl.BlockSpec((1, hd, LANES), lambda b, p: (b, 0, 0)),
            pl.BlockSpec((1, LANES), lambda b, p: (0, 0)),
            pl.BlockSpec((pages_per_step, page, 2 * hd), lambda b, p: (b * steps + steps - 1 - p, 0, 0)),
        ],
        out_specs=pl.BlockSpec((1, 1, hd), lambda b, p: (b, 0, 0)),
        out_shape=jax.ShapeDtypeStruct((db, 1, hd), F32),
        scratch_shapes=[pltpu.VMEM((MOD_ROWS, hd), F32), pltpu.VMEM((1, LANES), F32)],
        compiler_params=_params(("arbitrary", "arbitrary"), 32),
        name="sb_sample",
    )(qmat, bias_row, kv_pages)
    return out.reshape(db, hd)


def kernel(x_prompt, x_sample, cache_win_g0, cache_win_g1, cache_win_g2, cache_kv, page_table, c_prompt, c_sample, w_mod, b_mod, norm_g, ffn_w_in, ffn_w_out, a_w_qkv, a_q_norm, a_k_norm, a_w_o, kv_norm, w_mod_kv, b_mod_kv, w_kv, sb_k_norm, b_w_q, b_q_norm, b_sb_bias, b_w_o):
    batch, seq, d = x_prompt.shape
    db, ds, _ = x_sample.shape
    depth = w_mod.shape[0]
    n_a = a_w_qkv.shape[0]
    d_ff = ffn_w_out.shape[2]
    n_b_heads = w_kv.shape[1] // (2 * HEAD_DIM)
    assert ds == 1 and db == 8 and db + batch <= MOD_ROWS
    caches = (cache_win_g0, cache_win_g1, cache_win_g2)

    c_all = jnp.zeros((MOD_ROWS, d), F32).at[:db].set(c_sample).at[db:db + batch].set(c_prompt)
    mod = _modulation(c_all, w_mod, b_mod)
    mod_kv = _modulation(c_all, w_mod_kv[None], b_mod_kv[None])
    mod_s = mod[:, :, :db].reshape(depth * N_MOD, db, d)
    mod_p = mod[:, :, db:db + batch].reshape(depth * N_MOD * batch, 1, d)
    modkv_s = mod_kv[:, :, :db].reshape(2, db, d)
    modkv_p = mod_kv[:, :, db:db + batch].reshape(2 * batch, 1, d)

    tm_p = _pick_tile(seq, 1024)
    rows_p = _Rows(batch * seq, batch, 1, tm_p)
    rows_s = _Rows(db, 1, db, db)
    paths = (
        dict(rows=rows_p, mod=mod_p, modkv=modkv_p, tf=_pick_tile(d_ff, 256)),
        dict(rows=rows_s, mod=mod_s, modkv=modkv_s, tf=_pick_tile(d_ff, 512)),
    )

    norm3 = norm_g.reshape(depth * 3, 1, d)
    kvn3 = kv_norm.reshape(1, 1, d)
    w_in = ffn_w_in.reshape(depth * 2, d, 2 * d_ff)
    w_out = ffn_w_out.reshape(depth * 2, d_ff, d)
    w_kv3 = w_kv[None]

    a_gain = [jnp.concatenate([jnp.tile(a_q_norm[l], A_HEADS), jnp.tile(a_k_norm[l], A_HEADS),
                               jnp.ones((A_HEADS * HEAD_DIM,), F32)])[None] for l in range(n_a)]
    kv_gain = jnp.concatenate([jnp.tile(sb_k_norm, n_b_heads), jnp.ones((n_b_heads * HEAD_DIM,), F32)])[None]
    b_gain = [jnp.tile(b_q_norm[j], n_b_heads)[None] for j in range(depth - n_a)]

    h_p = x_prompt.reshape(batch * seq, d)
    h_s = x_sample.reshape(db, d)
    tn_qkv = GROUP_COLS
    tn_d = _pick_tile(d, 512)
    win_p = [[] for _ in range(N_GROUPS)]
    win_s = [[] for _ in range(N_GROUPS)]
    kv_p = kv_s = kv_p_bf16 = None
    n_pages = page_table.shape[1]
    pages_per_step = 4 if n_pages % 4 == 0 else 1
    kv_pages = _kv_gather(cache_kv, page_table, 2 if n_pages % 2 == 0 else 1)
    slopes = jnp.asarray(_alibi_slopes(), F32)

    def ffn(h, path, l, which):
        return _ffn(h, path["rows"], path["mod"], (l * N_MOD + 6 * which), norm3, l * 3 + 2 * which,
                    w_in, w_out, l * 2 + which, path["tf"])

    for l in range(depth):
        h_p = ffn(h_p, paths[0], l, 0)
        h_s = ffn(h_s, paths[1], l, 0)
        if l < n_a:
            qkv_p = _proj(h_p, rows_p, mod_p, l * N_MOD + 3, norm3, l * 3 + 1, a_w_qkv, l, a_gain[l],
                          2 * A_HEADS * HEAD_DIM, tn_qkv)
            qkv_s = _proj(h_s, rows_s, mod_s, l * N_MOD + 3, norm3, l * 3 + 1, a_w_qkv, l, a_gain[l],
                          2 * A_HEADS * HEAD_DIM, tn_qkv)
            outs, lses = [], []
            for g in range(N_GROUPS):
                o_g, lse_g = _dilated_prompt(qkv_p, slopes, batch, seq, g, GROUP_HEADS if DIL_RATES[g] == 1 else 1)
                outs.append(o_g)
                lses.append(lse_g)
            h_p = _mix_out(outs, lses, a_w_o, l, h_p, rows_p, mod_p, l * N_MOD + 5, tn_d)
            mixed_s = _dilated_sample(qkv_s, caches, l)
            h_s = _out_proj(mixed_s, a_w_o, l, h_s, rows_s, mod_s, l * N_MOD + 5, tn_d)

            qkv_p5 = qkv_p.reshape(batch, seq, 3, A_HEADS, HEAD_DIM)
            qkv_s5 = qkv_s.reshape(db, 1, 3, A_HEADS, HEAD_DIM)
            for g in range(N_GROUPS):
                hs = slice(g * GROUP_HEADS, (g + 1) * GROUP_HEADS)
                keep = min(DIL_WINDOWS[g], seq)
                win_p[g].append(qkv_p5[:, seq - keep:, 1:3, hs])
                buf = caches[g][l]
                win_s[g].append(jnp.concatenate([buf[:, 1:], qkv_s5[:, :, 1:3, hs]], axis=1))
        else:
            j = l - n_a
            q_p = _proj(h_p, rows_p, mod_p, l * N_MOD + 3, norm3, l * 3 + 1, b_w_q, j, b_gain[j],
                        n_b_heads * HEAD_DIM, tn_d)
            q_s = _proj(h_s, rows_s, mod_s, l * N_MOD + 3, norm3, l * 3 + 1, b_w_q, j, b_gain[j],
                        n_b_heads * HEAD_DIM, tn_d)
            o_p = _sb_prompt(q_p, kv_p_bf16, b_sb_bias[j], batch, seq, _pick_tile(seq, 512))
            h_p = _out_proj(o_p, b_w_o, j, h_p, rows_p, mod_p, l * N_MOD + 5, tn_d)
            o_s = _sb_sample(q_s, kv_pages, b_sb_bias[j], pages_per_step)
            h_s = _out_proj(o_s, b_w_o, j, h_s, rows_s, mod_s, l * N_MOD + 5, tn_d)
        h_p = ffn(h_p, paths[0], l, 1)
        h_s = ffn(h_s, paths[1], l, 1)
        if l == n_a - 1:
            kv_p, kv_p_bf16 = _proj(h_p, rows_p, modkv_p, 0, kvn3, 0, w_kv3, 0, kv_gain,
                                    n_b_heads * HEAD_DIM, tn_d, with_bf16=True)
            kv_s = _proj(h_s, rows_s, modkv_s, 0, kvn3, 0, w_kv3, 0, kv_gain, n_b_heads * HEAD_DIM, tn_d)

    y_p = h_p.reshape(batch, seq, d)
    y_s = h_s.reshape(db, 1, d)
    win_p = [jnp.stack(w, axis=0) for w in win_p]
    win_s = [jnp.stack(w, axis=0) for w in win_s]
    kv_p = kv_p.reshape(batch, seq, 2, n_b_heads, HEAD_DIM)
    kv_s = kv_s.reshape(db, 1, 2, n_b_heads, HEAD_DIM)
    return (y_p, y_s, win_p[0], win_p[1], win_p[2], kv_p, win_s[0], win_s[1], win_s[2], kv_s)
```

```python
import functools
import math

import numpy as np
import jax
import jax.numpy as jnp
from jax import lax
from jax.experimental import pallas as pl
from jax.experimental.pallas import tpu as pltpu

F32 = jnp.float32
BF16 = jnp.bfloat16

EPS = 1e-6
HEAD_DIM = 128
LANES = 128
N_MOD = 9
DIL_WINDOWS = (128, 512, 2048)
DIL_RATES = (1, 4, 16)
N_GROUPS = 3
GROUP_HEADS = 5
A_HEADS = N_GROUPS * GROUP_HEADS
A_BLOCK = 128
GROUP_COLS = GROUP_HEADS * HEAD_DIM
QKV_COLS = 3 * A_HEADS * HEAD_DIM
NEG_BIG = -1e30
MIB = 1024 * 1024
MOD_ROWS = 16


def _alibi_slopes():
    return [float(2.0 ** (-8.0 * (i + 1) / A_HEADS)) for i in range(A_HEADS)]


def _params(semantics, vmem_mib):
    return pltpu.CompilerParams(dimension_semantics=semantics, vmem_limit_bytes=vmem_mib * MIB)


def _pick_tile(n, preferred):
    if n <= preferred:
        return n
    t = (preferred // LANES) * LANES
    while t >= LANES:
        if n % t == 0:
            return t
        t -= LANES
    return n


class _Rows:
    def __init__(self, n_rows, groups, cond_rows, tm):
        self.n_rows = n_rows
        self.groups = groups
        self.cond_rows = cond_rows
        self.tm = tm
        self.tiles_per_group = (n_rows // groups) // tm
        assert self.tiles_per_group * tm * groups == n_rows

    def cond_spec(self, chunk, width, col_of=None):
        g, tpg = self.groups, self.tiles_per_group
        if col_of is None:
            return pl.BlockSpec((1, self.cond_rows, width), lambda i, j: (chunk * g + i // tpg, 0, 0))
        return pl.BlockSpec((1, self.cond_rows, width), lambda i, j: (chunk * g + i // tpg, 0, col_of(j)))


def _norm_mod(x, gain, shift, scale):
    ms = jnp.mean(x * x, axis=-1, keepdims=True)
    y = x * lax.rsqrt(ms + EPS) * gain
    return y * (1.0 + scale) + shift


def _mod_kernel(c_ref, w_ref, b_ref, o_ref):
    c = c_ref[...]
    a = (c * jax.nn.sigmoid(c)).astype(BF16)
    o_ref[0, 0] = jnp.dot(a, w_ref[0].astype(BF16), preferred_element_type=F32) + b_ref[0]


def _modulation(c_all, w, b):
    n_layers, d, n_out = w.shape
    n_chunks = n_out // d
    tn = _pick_tile(d, 1024)
    per_chunk = d // tn
    return pl.pallas_call(
        _mod_kernel,
        grid=(n_layers, n_out // tn),
        in_specs=[
            pl.BlockSpec((MOD_ROWS, d), lambda l, j: (0, 0)),
            pl.BlockSpec((1, d, tn), lambda l, j: (l, 0, j)),
            pl.BlockSpec((1, 1, tn), lambda l, j: (l, 0, j)),
        ],
        out_specs=pl.BlockSpec((1, 1, MOD_ROWS, tn), lambda l, j: (l, j // per_chunk, 0, j % per_chunk)),
        out_shape=jax.ShapeDtypeStruct((n_layers, n_chunks, MOD_ROWS, d), F32),
        compiler_params=_params(("arbitrary", "arbitrary"), 40),
        name="modulation",
    )(c_all, w, b.reshape(n_layers, 1, n_out))


def _ffn_kernel(x_ref, sh_ref, sc_ref, gt_ref, ng_ref, wg_ref, wu_ref, wo_ref, o_ref, xn_ref):
    f = pl.program_id(1)

    @pl.when(f == 0)
    def _():
        u = _norm_mod(x_ref[...], ng_ref[0], sh_ref[0], sc_ref[0])
        xn_ref[...] = u.astype(BF16)
        o_ref[...] = jnp.zeros_like(o_ref)

    xn = xn_ref[...]
    gate = jnp.dot(xn, wg_ref[...].astype(BF16), preferred_element_type=F32)
    up = jnp.dot(xn, wu_ref[...].astype(BF16), preferred_element_type=F32)
    act = (gate * jax.nn.sigmoid(gate) * up).astype(BF16)
    o_ref[...] += jnp.dot(act, wo_ref[...].astype(BF16), preferred_element_type=F32)

    @pl.when(f == pl.num_programs(1) - 1)
    def _():
        o_ref[...] = x_ref[...] + 0.5 * gt_ref[0] * o_ref[...]


def _ffn(h, rows, mod, chunk0, norm_g, norm_idx, w_in, w_out, w_idx, tf):
    m, d = h.shape
    d_ff = w_out.shape[1]
    n_f = d_ff // tf
    tm = rows.tm
    return pl.pallas_call(
        _ffn_kernel,
        grid=(m // tm, n_f),
        in_specs=[
            pl.BlockSpec((tm, d), lambda i, f: (i, 0)),
            rows.cond_spec(chunk0, d),
            rows.cond_spec(chunk0 + 1, d),
            rows.cond_spec(chunk0 + 2, d),
            pl.BlockSpec((1, 1, d), lambda i, f: (norm_idx, 0, 0)),
            pl.BlockSpec((None, d, tf), lambda i, f: (w_idx, 0, f)),
            pl.BlockSpec((None, d, tf), lambda i, f: (w_idx, 0, n_f + f)),
            pl.BlockSpec((None, tf, d), lambda i, f: (w_idx, f, 0)),
        ],
        out_specs=pl.BlockSpec((tm, d), lambda i, f: (i, 0)),
        out_shape=jax.ShapeDtypeStruct((m, d), F32),
        scratch_shapes=[pltpu.VMEM((tm, d), BF16)],
        compiler_params=_params(("arbitrary", "arbitrary"), 58),
        name="ffn",
    )(h, mod, mod, mod, norm_g, w_in, w_in, w_out)


def _proj_kernel(x_ref, sh_ref, sc_ref, ng_ref, w_ref, hg_ref, *rest, n_norm_tiles, with_bf16):
    if with_bf16:
        o_ref, obf_ref, xn_ref = rest
    else:
        o_ref, xn_ref = rest
        obf_ref = None
    j = pl.program_id(1)

    @pl.when(j == 0)
    def _():
        xn_ref[...] = _norm_mod(x_ref[...], ng_ref[0], sh_ref[0], sc_ref[0]).astype(BF16)

    acc = jnp.dot(xn_ref[...], w_ref[...].astype(BF16), preferred_element_type=F32)
    tn = acc.shape[1]

    def store(val, sl):
        o_ref[:, sl] = val
        if obf_ref is not None:
            obf_ref[:, sl] = val.astype(BF16)

    @pl.when(j < n_norm_tiles)
    def _():
        for t in range(tn // HEAD_DIM):
            sl = slice(t * HEAD_DIM, (t + 1) * HEAD_DIM)
            a = acc[:, sl]
            ms = jnp.mean(a * a, axis=-1, keepdims=True)
            store(a * lax.rsqrt(ms + EPS) * hg_ref[:, sl], sl)

    @pl.when(j >= n_norm_tiles)
    def _():
        store(acc, slice(None))


def _proj(h, rows, mod, chunk0, norm_g, norm_idx, w, w_idx, head_gain, n_norm_cols, tn, with_bf16=False):
    m, d = h.shape
    n = w.shape[-1]
    tm = rows.tm
    assert n % tn == 0 and n_norm_cols % tn == 0 and tn % HEAD_DIM == 0
    out_shape = [jax.ShapeDtypeStruct((m, n), F32)]
    out_specs = [pl.BlockSpec((tm, tn), lambda i, j: (i, j))]
    if with_bf16:
        out_shape.append(jax.ShapeDtypeStruct((m, n), BF16))
        out_specs.append(pl.BlockSpec((tm, tn), lambda i, j: (i, j)))
    res = pl.pallas_call(
        functools.partial(_proj_kernel, n_norm_tiles=n_norm_cols // tn, with_bf16=with_bf16),
        grid=(m // tm, n // tn),
        in_specs=[
            pl.BlockSpec((tm, d), lambda i, j: (i, 0)),
            rows.cond_spec(chunk0, d),
            rows.cond_spec(chunk0 + 1, d),
            pl.BlockSpec((1, 1, d), lambda i, j: (norm_idx, 0, 0)),
            pl.BlockSpec((None, d, tn), lambda i, j: (w_idx, 0, j)),
            pl.BlockSpec((1, tn), lambda i, j: (0, j)),
        ],
        out_specs=out_specs,
        out_shape=out_shape,
        scratch_shapes=[pltpu.VMEM((tm, d), BF16)],
        compiler_params=_params(("arbitrary", "arbitrary"), 56),
        name="proj",
    )(h, mod, mod, norm_g, w, head_gain)
    return res if with_bf16 else res[0]


def _out_kernel(x_ref, w_ref, h_ref, gt_ref, o_ref):
    y = jnp.dot(x_ref[...].astype(BF16), w_ref[...].astype(BF16), preferred_element_type=F32)
    o_ref[...] = h_ref[...] + gt_ref[0] * y


def _out_proj(x, w, w_idx, h, rows, mod, chunk, tn):
    m, k = x.shape
    n = w.shape[-1]
    tm = rows.tm
    return pl.pallas_call(
        _out_kernel,
        grid=(m // tm, n // tn),
        in_specs=[
            pl.BlockSpec((tm, k), lambda i, j: (i, 0)),
            pl.BlockSpec((None, k, tn), lambda i, j: (w_idx, 0, j)),
            pl.BlockSpec((tm, tn), lambda i, j: (i, j)),
            rows.cond_spec(chunk, tn, col_of=lambda j: j),
        ],
        out_specs=pl.BlockSpec((tm, tn), lambda i, j: (i, j)),
        out_shape=jax.ShapeDtypeStruct((m, n), F32),
        compiler_params=_params(("arbitrary", "arbitrary"), 48),
        name="out_proj",
    )(x, w, h, mod)


def _dil_kernel(slopes_ref, q_ref, kc_ref, kp_ref, vc_ref, vp_ref, o_ref, lse_ref, *, dil, heads, head0):
    n = pl.program_id(1)
    hb = pl.program_id(2)
    scale = 1.0 / math.sqrt(HEAD_DIM)
    qi = lax.broadcasted_iota(jnp.int32, (A_BLOCK, A_BLOCK), 0)
    ki = lax.broadcasted_iota(jnp.int32, (A_BLOCK, A_BLOCK), 1)
    step_c = qi - ki
    step_p = step_c + A_BLOCK
    valid_c = step_c >= 0
    valid_p = jnp.logical_and(step_p <= A_BLOCK, n > 0)
    dist_c = (step_c * dil).astype(F32)
    dist_p = (step_p * dil).astype(F32)
    nt = (((1,), (1,)), ((), ()))
    for hh in range(heads):
        slope = slopes_ref[head0 + hb * heads + hh]
        bias_c = jnp.where(valid_c, -slope * dist_c, NEG_BIG)
        bias_p = jnp.where(valid_p, -slope * dist_p, NEG_BIG)
        sl = slice(hh * HEAD_DIM, (hh + 1) * HEAD_DIM)
        for res in range(dil):
            rows = pl.ds(res, A_BLOCK, stride=dil) if dil > 1 else slice(None)
            q = (q_ref[rows, sl] * scale).astype(BF16)
            s_c = lax.dot_general(q, kc_ref[rows, sl].astype(BF16), nt, preferred_element_type=F32) + bias_c
            s_p = lax.dot_general(q, kp_ref[rows, sl].astype(BF16), nt, preferred_element_type=F32) + bias_p
            mx = jnp.maximum(jnp.max(s_c, axis=-1, keepdims=True), jnp.max(s_p, axis=-1, keepdims=True))
            p_c = jnp.exp(s_c - mx)
            p_p = jnp.exp(s_p - mx)
            den = jnp.sum(p_c, axis=-1, keepdims=True) + jnp.sum(p_p, axis=-1, keepdims=True)
            o = jnp.dot(p_c.astype(BF16), vc_ref[rows, sl].astype(BF16), preferred_element_type=F32)
            o = o + jnp.dot(p_p.astype(BF16), vp_ref[rows, sl].astype(BF16), preferred_element_type=F32)
            o_ref[rows, sl] = o / den
            lse_ref[rows, sl] = jnp.broadcast_to(mx + jnp.log(den), (A_BLOCK, HEAD_DIM))


def _dilated_prompt(qkv, slopes, batch, seq, g, heads):
    dil = DIL_RATES[g]
    span = dil * A_BLOCK
    assert DIL_WINDOWS[g] // dil == A_BLOCK and seq % span == 0 and GROUP_HEADS % heads == 0
    nb = seq // span
    cw = heads * HEAD_DIM
    q_col = g * GROUP_COLS // cw
    k_col = (A_HEADS * HEAD_DIM + g * GROUP_COLS) // cw
    v_col = (2 * A_HEADS * HEAD_DIM + g * GROUP_COLS) // cw
    blk = (span, cw)

    def cur(col):
        return pl.BlockSpec(blk, lambda b, n, hb: (b * nb + n, col + hb))

    def prev(col):
        return pl.BlockSpec(blk, lambda b, n, hb: (b * nb + jnp.maximum(n - 1, 0), col + hb))

    out_spec = pl.BlockSpec(blk, lambda b, n, hb: (b * nb + n, hb))
    return pl.pallas_call(
        functools.partial(_dil_kernel, dil=dil, heads=heads, head0=g * GROUP_HEADS),
        grid=(batch, nb, GROUP_HEADS // heads),
        in_specs=[pl.BlockSpec(memory_space=pltpu.SMEM), cur(q_col), cur(k_col), prev(k_col), cur(v_col), prev(v_col)],
        out_specs=[out_spec, out_spec],
        out_shape=[jax.ShapeDtypeStruct((batch * seq, GROUP_COLS), F32)] * 2,
        compiler_params=_params(("arbitrary", "arbitrary", "arbitrary"), 40),
        name="dilated_prompt",
    )(slopes, qkv, qkv, qkv, qkv, qkv)


def _mix_out_kernel(o0_ref, o1_ref, o2_ref, l0_ref, l1_ref, l2_ref, w_ref, h_ref, gt_ref, out_ref, mix_ref):
    j = pl.program_id(1)

    @pl.when(j == 0)
    def _():
        ls = []
        for l_ref in (l0_ref, l1_ref, l2_ref):
            heads = [l_ref[:, hh * HEAD_DIM:(hh + 1) * HEAD_DIM] for hh in range(GROUP_HEADS)]
            htop = functools.reduce(jnp.maximum, heads)
            htot = functools.reduce(lambda a, b: a + b, [jnp.exp(l - htop) for l in heads])
            ls.append(htop + jnp.log(htot) - math.log(GROUP_HEADS))
        top = jnp.maximum(jnp.maximum(ls[0], ls[1]), ls[2])
        es = [jnp.exp(l - top) for l in ls]
        den = es[0] + es[1] + es[2]
        for g, o_ref in enumerate((o0_ref, o1_ref, o2_ref)):
            alpha = es[g] / den
            for hh in range(GROUP_HEADS):
                src = slice(hh * HEAD_DIM, (hh + 1) * HEAD_DIM)
                dst = slice(g * GROUP_COLS + hh * HEAD_DIM, g * GROUP_COLS + (hh + 1) * HEAD_DIM)
                mix_ref[:, dst] = (o_ref[:, src] * alpha).astype(BF16)

    y = jnp.dot(mix_ref[...], w_ref[...].astype(BF16), preferred_element_type=F32)
    out_ref[...] = h_ref[...] + gt_ref[0] * y


def _mix_out(outs, lses, w, w_idx, h, rows, mod, chunk, tn):
    m = h.shape[0]
    n = w.shape[-1]
    k = w.shape[-2]
    tm = rows.tm
    o_spec = pl.BlockSpec((tm, GROUP_COLS), lambda i, j: (i, 0))
    return pl.pallas_call(
        _mix_out_kernel,
        grid=(m // tm, n // tn),
        in_specs=[
            o_spec, o_spec, o_spec, o_spec, o_spec, o_spec,
            pl.BlockSpec((None, k, tn), lambda i, j: (w_idx, 0, j)),
            pl.BlockSpec((tm, tn), lambda i, j: (i, j)),
            rows.cond_spec(chunk, tn, col_of=lambda j: j),
        ],
        out_specs=pl.BlockSpec((tm, tn), lambda i, j: (i, j)),
        out_shape=jax.ShapeDtypeStruct((m, n), F32),
        scratch_shapes=[pltpu.VMEM((tm, k), BF16)],
        compiler_params=_params(("arbitrary", "arbitrary"), 58),
        name="mix_out",
    )(*outs, *lses, w, h, mod)


def _dil_sample_kernel(qkv_ref, c0_ref, c1_ref, c2_ref, o_ref, *, slopes):
    scale = 1.0 / math.sqrt(HEAD_DIM)
    nt = (((1,), (1,)), ((), ()))
    kbase = A_HEADS * HEAD_DIM
    vbase = 2 * A_HEADS * HEAD_DIM
    steps = (A_BLOCK - lax.broadcasted_iota(jnp.int32, (1, A_BLOCK), 1)).astype(F32)
    outs, glses = [], []
    for g, c_ref in enumerate((c0_ref, c1_ref, c2_ref)):
        dil = DIL_RATES[g]
        head_outs, head_lses = [], []
        for hh in range(GROUP_HEADS):
            col = (g * GROUP_HEADS + hh) * HEAD_DIM
            q = qkv_ref[0, :, col:col + HEAD_DIM]
            k_new = qkv_ref[0, :, kbase + col:kbase + col + HEAD_DIM]
            v_new = qkv_ref[0, :, vbase + col:vbase + col + HEAD_DIM]
            k_buf = c_ref[0, 0, :, hh * HEAD_DIM:(hh + 1) * HEAD_DIM]
            v_buf = c_ref[0, 0, :, GROUP_COLS + hh * HEAD_DIM:GROUP_COLS + (hh + 1) * HEAD_DIM]
            q8 = jnp.broadcast_to(q * scale, (8, HEAD_DIM)).astype(BF16)
            s_buf = lax.dot_general(q8, k_buf.astype(BF16), nt, preferred_element_type=F32)[0:1]
            s_buf = s_buf - slopes[g * GROUP_HEADS + hh] * dil * steps
            s_new = jnp.sum((q * scale).astype(BF16).astype(F32) * k_new.astype(BF16).astype(F32),
                            axis=-1, keepdims=True)
            mx = jnp.maximum(jnp.max(s_buf, axis=-1, keepdims=True), s_new)
            p_buf = jnp.exp(s_buf - mx)
            p_new = jnp.exp(s_new - mx)
            den = jnp.sum(p_buf, axis=-1, keepdims=True) + p_new
            p8 = jnp.broadcast_to(p_buf, (8, A_BLOCK)).astype(BF16)
            o = jnp.dot(p8, v_buf.astype(BF16), preferred_element_type=F32)[0:1]
            o = (o + p_new.astype(BF16).astype(F32) * v_new.astype(BF16).astype(F32)) / den
            head_outs.append(o)
            head_lses.append(mx + jnp.log(den))
        top = functools.reduce(jnp.maximum, head_lses)
        tot = functools.reduce(lambda a, b: a + b, [jnp.exp(l - top) for l in head_lses])
        glses.append(top + jnp.log(tot) - math.log(GROUP_HEADS))
        outs.append(head_outs)
    top = functools.reduce(jnp.maximum, glses)
    es = [jnp.exp(l - top) for l in glses]
    den = es[0] + es[1] + es[2]
    for g in range(N_GROUPS):
        alpha = es[g] / den
        for hh in range(GROUP_HEADS):
            col = (g * GROUP_HEADS + hh) * HEAD_DIM
            o_ref[0, :, col:col + HEAD_DIM] = outs[g][hh] * alpha


def _dilated_sample(qkv_s, caches, layer):
    db = qkv_s.shape[0]
    row_cols = 2 * GROUP_COLS
    views, specs = [], []
    for g, c in enumerate(caches):
        dil = DIL_RATES[g]
        assert c.shape[2] == DIL_WINDOWS[g] and c.shape[2] // dil == A_BLOCK
        views.append(c[:, :, ::dil].reshape(c.shape[0], db, A_BLOCK, row_cols))
        specs.append(pl.BlockSpec((1, 1, A_BLOCK, row_cols), lambda b: (layer, b, 0, 0)))
    out = pl.pallas_call(
        functools.partial(_dil_sample_kernel, slopes=tuple(_alibi_slopes())),
        grid=(db,),
        in_specs=[pl.BlockSpec((1, 1, QKV_COLS), lambda b: (b, 0, 0))] + specs,
        out_specs=pl.BlockSpec((1, 1, A_HEADS * HEAD_DIM), lambda b: (b, 0, 0)),
        out_shape=jax.ShapeDtypeStruct((db, 1, A_HEADS * HEAD_DIM), F32),
        compiler_params=_params(("arbitrary",), 32),
        name="dilated_sample",
    )(qkv_s.reshape(db, 1, QKV_COLS), *views)
    return out.reshape(db, A_HEADS * HEAD_DIM)


def _stick_terms(z):
    lo = jnp.minimum(z, 0.0)
    hi = jnp.maximum(z, 0.0)
    l = jnp.log(1.0 + jnp.exp(lo - hi))
    return lo - l, hi + l


def _sb_prompt_kernel(bias_ref, q_ref, k_ref, v_ref, o_ref, *, tq, tk, sub):
    h = pl.program_id(1)
    i = pl.program_id(2)
    n_sub = tk // sub
    bias = bias_ref[h]
    q = (q_ref[0] * (1.0 / math.sqrt(HEAD_DIM))).astype(BF16)
    nt = (((1,), (1,)), ((), ()))
    r = lax.broadcasted_iota(jnp.int32, (sub, sub), 0)
    c = lax.broadcasted_iota(jnp.int32, (sub, sub), 1)
    later_mat = jnp.where(r > c, 1.0, 0.0).astype(BF16)
    n_full = (i * tq) // tk
    q_pos = lax.broadcasted_iota(jnp.int32, (tq, sub), 0) + (i * tq - n_full * tk)
    k_off = lax.broadcasted_iota(jnp.int32, (tq, sub), 1)

    def block(start, carry, acc, masked):
        k = k_ref[0, pl.ds(start, tk), :]
        v = v_ref[0, pl.ds(start, tk), :]
        z_all = lax.dot_general(q, k, nt, preferred_element_type=F32) + bias
        ws = [None] * n_sub
        for cidx in reversed(range(n_sub)):
            log_take, cost = _stick_terms(z_all[:, cidx * sub:(cidx + 1) * sub])
            if masked:
                causal = (k_off + cidx * sub) < q_pos
                cost = jnp.where(causal, cost, 0.0)
            later = jnp.dot(cost.astype(BF16), later_mat, preferred_element_type=F32)
            w = jnp.exp(log_take - later - carry)
            if masked:
                w = jnp.where(causal, w, 0.0)
            ws[cidx] = w.astype(BF16)
            carry = carry + jnp.sum(cost, axis=-1, keepdims=True)
        acc = acc + jnp.dot(jnp.concatenate(ws, axis=1), v, preferred_element_type=F32)
        return carry, acc

    zeros = (jnp.zeros((tq, 1), F32), jnp.zeros((tq, HEAD_DIM), F32))
    carry, acc = block(pl.multiple_of(n_full * tk, tk), *zeros, True)

    def body(t, state):
        start = pl.multiple_of((n_full - 1 - t) * tk, tk)
        return block(start, state[0], state[1], False)

    carry, acc = lax.fori_loop(0, n_full, body, (carry, acc))
    o_ref[0] = acc.astype(o_ref.dtype)


def _sb_prompt(q, kv_bf16, sb_bias, batch, seq, tq, tk):
    n_heads = q.shape[1] // HEAD_DIM
    assert tk % tq == 0 and seq % tk == 0
    q3 = q.reshape(batch, seq, n_heads * HEAD_DIM)
    kv3 = kv_bf16.reshape(batch, seq, 2 * n_heads * HEAD_DIM)
    out = pl.pallas_call(
        functools.partial(_sb_prompt_kernel, tq=tq, tk=tk, sub=min(tk, 2 * LANES)),
        grid=(batch, n_heads, seq // tq),
        in_specs=[
            pl.BlockSpec(memory_space=pltpu.SMEM),
            pl.BlockSpec((1, tq, HEAD_DIM), lambda b, h, i: (b, i, h)),
            pl.BlockSpec((1, seq, HEAD_DIM), lambda b, h, i: (b, 0, h)),
            pl.BlockSpec((1, seq, HEAD_DIM), lambda b, h, i: (b, 0, n_heads + h)),
        ],
        out_specs=pl.BlockSpec((1, tq, HEAD_DIM), lambda b, h, i: (b, i, h)),
        out_shape=jax.ShapeDtypeStruct((batch, seq, n_heads * HEAD_DIM), BF16),
        compiler_params=_params(("arbitrary", "arbitrary", "arbitrary"), 40),
        name="sb_prompt",
    )(sb_bias, q3, kv3, kv3)
    return out.reshape(batch * seq, n_heads * HEAD_DIM)


def _relayout_page(x_ref, o_ref, t, n_heads):
    page = o_ref.shape[1]
    keys = 2 * LANES // n_heads
    n = keys * n_heads
    i = lax.broadcasted_iota(jnp.int32, (n, n), 0)
    j = lax.broadcasted_iota(jnp.int32, (n, n), 1)
    perm = jnp.where(jnp.logical_and(i // keys == j % n_heads, i % keys == j // n_heads), 1.0, 0.0).astype(BF16)
    for c in range(page // keys):
        k_rows = x_ref[c * keys:(c + 1) * keys, 0:n_heads, :].reshape(n, HEAD_DIM)
        v_rows = x_ref[c * keys:(c + 1) * keys, n_heads:2 * n_heads, :].reshape(n, HEAD_DIM)
        both = jnp.concatenate([k_rows, v_rows], axis=1).astype(BF16)
        y = jnp.dot(perm, both, preferred_element_type=F32)
        for hh in range(n_heads):
            blk = y[hh * keys:(hh + 1) * keys].astype(BF16)
            o_ref[t, c * keys:(c + 1) * keys, hh * HEAD_DIM:(hh + 1) * HEAD_DIM] = blk[:, :HEAD_DIM]
            o_ref[t, c * keys:(c + 1) * keys, (n_heads + hh) * HEAD_DIM:(n_heads + hh + 1) * HEAD_DIM] = (
                blk[:, HEAD_DIM:])


def _sb_sample_step(qmat_ref, bias_ref, kv_ref, o_ref, acc_ref, carry_ref, n_heads):
    p = pl.program_id(1)
    hd = n_heads * HEAD_DIM
    pages_per_step, page = kv_ref.shape[0], kv_ref.shape[1]

    @pl.when(p == 0)
    def _():
        acc_ref[...] = jnp.zeros_like(acc_ref)
        carry_ref[...] = jnp.zeros_like(carry_ref)

    r = lax.broadcasted_iota(jnp.int32, (page, page), 0)
    c = lax.broadcasted_iota(jnp.int32, (page, page), 1)
    later_mat = jnp.where(c > r, 1.0, 0.0).astype(BF16)
    k_all = kv_ref[:, :, :hd].reshape(pages_per_step * page, hd)
    half = hd // 2
    z = (jnp.dot(k_all[:, :half], qmat_ref[0, :half, :], preferred_element_type=F32)
         + jnp.dot(k_all[:, half:], qmat_ref[0, half:, :], preferred_element_type=F32) + bias_ref[...])
    log_take, cost = _stick_terms(z)
    carry = carry_ref[...]
    ws = [None] * pages_per_step
    for t in reversed(range(pages_per_step)):
        rows = slice(t * page, (t + 1) * page)
        later = jnp.dot(later_mat, cost[rows].astype(BF16), preferred_element_type=F32)
        w = jnp.exp(log_take[rows] - later - carry)
        ws[t] = w.T[:MOD_ROWS].astype(BF16)
        carry = carry + jnp.sum(cost[rows], axis=0, keepdims=True)
    carry_ref[...] = carry
    w_all = jnp.concatenate(ws, axis=1)
    v_all = kv_ref[:, :, hd:].reshape(pages_per_step * page, hd)
    acc_ref[...] += jnp.dot(w_all, v_all, preferred_element_type=F32)

    @pl.when(p == pl.num_programs(1) - 1)
    def _():
        for hh in range(n_heads):
            sl = slice(hh * HEAD_DIM, (hh + 1) * HEAD_DIM)
            o_ref[0, :, sl] = acc_ref[hh:hh + 1, sl]


def _sb_sample_kernel(qmat_ref, bias_ref, kv_ref, o_ref, acc_ref, carry_ref, *, n_heads):
    _sb_sample_step(qmat_ref, bias_ref, kv_ref, o_ref, acc_ref, carry_ref, n_heads)


def _sb_sample_gather_kernel(pt_ref, qmat_ref, bias_ref, *refs, n_heads, pages_per_step):
    x_refs = refs[:pages_per_step]
    o_ref, kv_ref, acc_ref, carry_ref = refs[pages_per_step:]
    for t, x_ref in enumerate(x_refs):
        _relayout_page(x_ref, kv_ref, t, n_heads)
    _sb_sample_step(qmat_ref, bias_ref, kv_ref, o_ref, acc_ref, carry_ref, n_heads)


def _sb_sample_operands(q, sb_bias):
    hd = q.shape[1]
    n_heads = hd // HEAD_DIM
    assert n_heads <= MOD_ROWS
    head_of_row = jnp.arange(hd, dtype=jnp.int32) // HEAD_DIM
    sel = (head_of_row[:, None] == jnp.arange(LANES, dtype=jnp.int32)[None, :]).astype(F32)
    qmat = ((q * (1.0 / math.sqrt(HEAD_DIM)))[:, :, None] * sel[None]).astype(BF16)
    bias_row = jnp.zeros((1, LANES), F32).at[0, :n_heads].set(sb_bias.astype(F32))
    return qmat, bias_row


def _sb_sample_gather(q, cache_kv, page_table, sb_bias, pages_per_step):
    db, hd = q.shape
    n_phys, page, _, n_heads, _ = cache_kv.shape
    n_pages = page_table.shape[1]
    slabs = 2 * n_heads
    assert (2 * LANES) % n_heads == 0 and page % (2 * LANES // n_heads) == 0 and n_heads % 8 == 0
    steps = n_pages // pages_per_step
    qmat, bias_row = _sb_sample_operands(q, sb_bias)
    rows = cache_kv.reshape(n_phys, page, slabs, HEAD_DIM)

    def page_spec(t):
        return pl.BlockSpec((None, page, slabs, HEAD_DIM),
                            lambda b, p, pt: (pt[b, (steps - 1 - p) * pages_per_step + t], 0, 0, 0))

    out, kv_pages = pl.pallas_call(
        functools.partial(_sb_sample_gather_kernel, n_heads=n_heads, pages_per_step=pages_per_step),
        grid_spec=pltpu.PrefetchScalarGridSpec(
            num_scalar_prefetch=1,
            grid=(db, steps),
            in_specs=[
                pl.BlockSpec((1, hd, LANES), lambda b, p, pt: (b, 0, 0)),
                pl.BlockSpec((1, LANES), lambda b, p, pt: (0, 0)),
            ] + [page_spec(t) for t in range(pages_per_step)],
            out_specs=[
                pl.BlockSpec((1, 1, hd), lambda b, p, pt: (b, 0, 0)),
                pl.BlockSpec((pages_per_step, page, 2 * hd), lambda b, p, pt: (b * steps + steps - 1 - p, 0, 0)),
            ],
            scratch_shapes=[pltpu.VMEM((MOD_ROWS, hd), F32), pltpu.VMEM((1, LANES), F32)],
        ),
        out_shape=[jax.ShapeDtypeStruct((db, 1, hd), F32),
                   jax.ShapeDtypeStruct((db * n_pages, page, 2 * hd), BF16)],
        compiler_params=_params(("arbitrary", "arbitrary"), 48),
        name="sb_sample_gather",
    )(page_table, qmat, bias_row, *([rows] * pages_per_step))
    return out.reshape(db, hd), kv_pages


def _sb_sample(q, kv_pages, sb_bias, pages_per_step):
    db, hd = q.shape
    n_heads = hd // HEAD_DIM
    page = kv_pages.shape[1]
    steps = kv_pages.shape[0] // (db * pages_per_step)
    qmat, bias_row = _sb_sample_operands(q, sb_bias)
    out = pl.pallas_call(
        functools.partial(_sb_sample_kernel, n_heads=n_heads),
        grid=(db, steps),
        in_specs=[
            pl.BlockSpec((1, hd, LANES), lambda b, p: (b, 0, 0)),
            pl.BlockSpec((1, LANES), lambda b, p: (0, 0)),
            pl.BlockSpec((pages_per_step, page, 2 * hd), lambda b, p: (b * steps + steps - 1 - p, 0, 0)),
        ],
        out_specs=pl.BlockSpec((1, 1, hd), lambda b, p: (b, 0, 0)),
        out_shape=jax.ShapeDtypeStruct((db, 1, hd), F32),
        scratch_shapes=[pltpu.VMEM((MOD_ROWS, hd), F32), pltpu.VMEM((1, LANES), F32)],
        compiler_params=_params(("arbitrary", "arbitrary"), 32),
        name="sb_sample",
    )(qmat, bias_row, kv_pages)
    return out.reshape(db, hd)


def kernel(x_prompt, x_sample, cache_win_g0, cache_win_g1, cache_win_g2, cache_kv, page_table, c_prompt, c_sample, w_mod, b_mod, norm_g, ffn_w_in, ffn_w_out, a_w_qkv, a_q_norm, a_k_norm, a_w_o, kv_norm, w_mod_kv, b_mod_kv, w_kv, sb_k_norm, b_w_q, b_q_norm, b_sb_bias, b_w_o):
    batch, seq, d = x_prompt.shape
    db, ds, _ = x_sample.shape
    depth = w_mod.shape[0]
    n_a = a_w_qkv.shape[0]
    d_ff = ffn_w_out.shape[2]
    n_b_heads = w_kv.shape[1] // (2 * HEAD_DIM)
    assert ds == 1 and db == 8 and db + batch <= MOD_ROWS
    caches = (cache_win_g0, cache_win_g1, cache_win_g2)

    c_all = jnp.zeros((MOD_ROWS, d), F32).at[:db].set(c_sample).at[db:db + batch].set(c_prompt)
    mod = _modulation(c_all, w_mod, b_mod)
    mod_kv = _modulation(c_all, w_mod_kv[None], b_mod_kv[None])
    mod_s = mod[:, :, :db].reshape(depth * N_MOD, db, d)
    mod_p = mod[:, :, db:db + batch].reshape(depth * N_MOD * batch, 1, d)
    modkv_s = mod_kv[:, :, :db].reshape(2, db, d)
    modkv_p = mod_kv[:, :, db:db + batch].reshape(2 * batch, 1, d)

    tm_p = _pick_tile(seq, 1024)
    rows_p = _Rows(batch * seq, batch, 1, tm_p)
    rows_s = _Rows(db, 1, db, db)
    paths = (
        dict(rows=rows_p, mod=mod_p, modkv=modkv_p, tf=_pick_tile(d_ff, 256)),
        dict(rows=rows_s, mod=mod_s, modkv=modkv_s, tf=_pick_tile(d_ff, 512)),
    )

    norm3 = norm_g.reshape(depth * 3, 1, d)
    kvn3 = kv_norm.reshape(1, 1, d)
    w_in = ffn_w_in.reshape(depth * 2, d, 2 * d_ff)
    w_out = ffn_w_out.reshape(depth * 2, d_ff, d)
    w_kv3 = w_kv[None]

    a_gain = [jnp.concatenate([jnp.tile(a_q_norm[l], A_HEADS), jnp.tile(a_k_norm[l], A_HEADS),
                               jnp.ones((A_HEADS * HEAD_DIM,), F32)])[None] for l in range(n_a)]
    kv_gain = jnp.concatenate([jnp.tile(sb_k_norm, n_b_heads), jnp.ones((n_b_heads * HEAD_DIM,), F32)])[None]
    b_gain = [jnp.tile(b_q_norm[j], n_b_heads)[None] for j in range(depth - n_a)]

    h_p = x_prompt.reshape(batch * seq, d)
    h_s = x_sample.reshape(db, d)
    tn_qkv = GROUP_COLS
    tn_d = _pick_tile(d, 512)
    win_p = [[] for _ in range(N_GROUPS)]
    win_s = [[] for _ in range(N_GROUPS)]
    kv_p = kv_s = kv_p_bf16 = None
    n_pages = page_table.shape[1]
    pages_per_step = 4 if n_pages % 4 == 0 else 1
    kv_pages = None
    slopes = jnp.asarray(_alibi_slopes(), F32)

    def ffn(h, path, l, which):
        return _ffn(h, path["rows"], path["mod"], (l * N_MOD + 6 * which), norm3, l * 3 + 2 * which,
                    w_in, w_out, l * 2 + which, path["tf"])

    for l in range(depth):
        h_p = ffn(h_p, paths[0], l, 0)
        h_s = ffn(h_s, paths[1], l, 0)
        if l < n_a:
            qkv_p = _proj(h_p, rows_p, mod_p, l * N_MOD + 3, norm3, l * 3 + 1, a_w_qkv, l, a_gain[l],
                          2 * A_HEADS * HEAD_DIM, tn_qkv)
            qkv_s = _proj(h_s, rows_s, mod_s, l * N_MOD + 3, norm3, l * 3 + 1, a_w_qkv, l, a_gain[l],
                          2 * A_HEADS * HEAD_DIM, tn_qkv)
            outs, lses = [], []
            for g in range(N_GROUPS):
                o_g, lse_g = _dilated_prompt(qkv_p, slopes, batch, seq, g, GROUP_HEADS if DIL_RATES[g] == 1 else 1)
                outs.append(o_g)
                lses.append(lse_g)
            h_p = _mix_out(outs, lses, a_w_o, l, h_p, rows_p, mod_p, l * N_MOD + 5, tn_d)
            mixed_s = _dilated_sample(qkv_s, caches, l)
            h_s = _out_proj(mixed_s, a_w_o, l, h_s, rows_s, mod_s, l * N_MOD + 5, tn_d)

            qkv_p3 = qkv_p.reshape(batch, seq, QKV_COLS)
            qkv_s5 = qkv_s.reshape(db, 1, 3, A_HEADS, HEAD_DIM)
            for g in range(N_GROUPS):
                hs = slice(g * GROUP_HEADS, (g + 1) * GROUP_HEADS)
                keep = min(DIL_WINDOWS[g], seq)
                kv_rows = [qkv_p3[:, seq - keep:, part * A_HEADS * HEAD_DIM + g * GROUP_COLS:
                                  part * A_HEADS * HEAD_DIM + (g + 1) * GROUP_COLS] for part in (1, 2)]
                win_p[g].append(jnp.stack(kv_rows, axis=2).reshape(batch, keep, 2, GROUP_HEADS, HEAD_DIM))
                buf = caches[g][l]
                win_s[g].append(jnp.concatenate([buf[:, 1:], qkv_s5[:, :, 1:3, hs]], axis=1))
        else:
            j = l - n_a
            q_p = _proj(h_p, rows_p, mod_p, l * N_MOD + 3, norm3, l * 3 + 1, b_w_q, j, b_gain[j],
                        n_b_heads * HEAD_DIM, tn_d)
            q_s = _proj(h_s, rows_s, mod_s, l * N_MOD + 3, norm3, l * 3 + 1, b_w_q, j, b_gain[j],
                        n_b_heads * HEAD_DIM, tn_d)
            o_p = _sb_prompt(q_p, kv_p_bf16, b_sb_bias[j], batch, seq, _pick_tile(seq, 512), _pick_tile(seq, 512))
            h_p = _out_proj(o_p, b_w_o, j, h_p, rows_p, mod_p, l * N_MOD + 5, tn_d)
            if kv_pages is None:
                o_s, kv_pages = _sb_sample_gather(q_s, cache_kv, page_table, b_sb_bias[j], pages_per_step)
            else:
                o_s = _sb_sample(q_s, kv_pages, b_sb_bias[j], pages_per_step)
            h_s = _out_proj(o_s, b_w_o, j, h_s, rows_s, mod_s, l * N_MOD + 5, tn_d)
        h_p = ffn(h_p, paths[0], l, 1)
        h_s = ffn(h_s, paths[1], l, 1)
        if l == n_a - 1:
            kv_p, kv_p_bf16 = _proj(h_p, rows_p, modkv_p, 0, kvn3, 0, w_kv3, 0, kv_gain,
                                    n_b_heads * HEAD_DIM, tn_d, with_bf16=True)
            kv_s = _proj(h_s, rows_s, modkv_s, 0, kvn3, 0, w_kv3, 0, kv_gain, n_b_heads * HEAD_DIM, tn_d)

    y_p = h_p.reshape(batch, seq, d)
    y_s = h_s.reshape(db, 1, d)
    win_p = [jnp.stack(w, axis=0) for w in win_p]
    win_s = [jnp.stack(w, axis=0) for w in win_s]
    kv_p = kv_p.reshape(batch, seq, 2, n_b_heads, HEAD_DIM)
    kv_s = kv_s.reshape(db, 1, 2, n_b_heads, HEAD_DIM)
    return (y_p, y_s, win_p[0], win_p[1], win_p[2], kv_p, win_s[0], win_s[1], win_s[2], kv_s)
```

```python
import functools
import math

import numpy as np
import jax
import jax.numpy as jnp
from jax import lax
from jax.experimental import pallas as pl
from jax.experimental.pallas import tpu as pltpu

F32 = jnp.float32
BF16 = jnp.bfloat16

EPS = 1e-6
HEAD_DIM = 128
LANES = 128
N_MOD = 9
DIL_WINDOWS = (128, 512, 2048)
DIL_RATES = (1, 4, 16)
N_GROUPS = 3
GROUP_HEADS = 5
A_HEADS = N_GROUPS * GROUP_HEADS
A_BLOCK = 128
GROUP_COLS = GROUP_HEADS * HEAD_DIM
QKV_COLS = 3 * A_HEADS * HEAD_DIM
NEG_BIG = -1e30
MIB = 1024 * 1024
MOD_ROWS = 16


def _alibi_slopes():
    return [float(2.0 ** (-8.0 * (i + 1) / A_HEADS)) for i in range(A_HEADS)]


def _params(semantics, vmem_mib):
    return pltpu.CompilerParams(dimension_semantics=semantics, vmem_limit_bytes=vmem_mib * MIB)


def _pick_tile(n, preferred):
    if n <= preferred:
        return n
    t = (preferred // LANES) * LANES
    while t >= LANES:
        if n % t == 0:
            return t
        t -= LANES
    return n


class _Rows:
    def __init__(self, n_rows, groups, cond_rows, tm):
        self.n_rows = n_rows
        self.groups = groups
        self.cond_rows = cond_rows
        self.tm = tm
        self.tiles_per_group = (n_rows // groups) // tm
        assert self.tiles_per_group * tm * groups == n_rows

    def cond_spec(self, chunk, width, col_of=None):
        g, tpg = self.groups, self.tiles_per_group
        if col_of is None:
            return pl.BlockSpec((1, self.cond_rows, width), lambda i, j: (chunk * g + i // tpg, 0, 0))
        return pl.BlockSpec((1, self.cond_rows, width), lambda i, j: (chunk * g + i // tpg, 0, col_of(j)))


def _norm_mod(x, gain, shift, scale):
    ms = jnp.mean(x * x, axis=-1, keepdims=True)
    y = x * lax.rsqrt(ms + EPS) * gain
    return y * (1.0 + scale) + shift


def _mod_kernel(c_ref, w_ref, b_ref, o_ref):
    c = c_ref[...]
    a = (c * jax.nn.sigmoid(c)).astype(BF16)
    o_ref[0, 0] = jnp.dot(a, w_ref[0].astype(BF16), preferred_element_type=F32) + b_ref[0]


def _modulation(c_all, w, b):
    n_layers, d, n_out = w.shape
    n_chunks = n_out // d
    tn = _pick_tile(d, 1024)
    per_chunk = d // tn
    return pl.pallas_call(
        _mod_kernel,
        grid=(n_layers, n_out // tn),
        in_specs=[
            pl.BlockSpec((MOD_ROWS, d), lambda l, j: (0, 0)),
            pl.BlockSpec((1, d, tn), lambda l, j: (l, 0, j)),
            pl.BlockSpec((1, 1, tn), lambda l, j: (l, 0, j)),
        ],
        out_specs=pl.BlockSpec((1, 1, MOD_ROWS, tn), lambda l, j: (l, j // per_chunk, 0, j % per_chunk)),
        out_shape=jax.ShapeDtypeStruct((n_layers, n_chunks, MOD_ROWS, d), F32),
        compiler_params=_params(("arbitrary", "arbitrary"), 40),
        name="modulation",
    )(c_all, w, b.reshape(n_layers, 1, n_out))


def _ffn_kernel(x_ref, sh_ref, sc_ref, gt_ref, ng_ref, wg_ref, wu_ref, wo_ref, o_ref, xn_ref):
    f = pl.program_id(1)

    @pl.when(f == 0)
    def _():
        u = _norm_mod(x_ref[...], ng_ref[0], sh_ref[0], sc_ref[0])
        xn_ref[...] = u.astype(BF16)
        o_ref[...] = jnp.zeros_like(o_ref)

    xn = xn_ref[...]
    gate = jnp.dot(xn, wg_ref[...].astype(BF16), preferred_element_type=F32)
    up = jnp.dot(xn, wu_ref[...].astype(BF16), preferred_element_type=F32)
    act = (gate * jax.nn.sigmoid(gate) * up).astype(BF16)
    o_ref[...] += jnp.dot(act, wo_ref[...].astype(BF16), preferred_element_type=F32)

    @pl.when(f == pl.num_programs(1) - 1)
    def _():
        o_ref[...] = x_ref[...] + 0.5 * gt_ref[0] * o_ref[...]


def _ffn(h, rows, mod, chunk0, norm_g, norm_idx, w_in, w_out, w_idx, tf):
    m, d = h.shape
    d_ff = w_out.shape[1]
    n_f = d_ff // tf
    tm = rows.tm
    return pl.pallas_call(
        _ffn_kernel,
        grid=(m // tm, n_f),
        in_specs=[
            pl.BlockSpec((tm, d), lambda i, f: (i, 0)),
            rows.cond_spec(chunk0, d),
            rows.cond_spec(chunk0 + 1, d),
            rows.cond_spec(chunk0 + 2, d),
            pl.BlockSpec((1, 1, d), lambda i, f: (norm_idx, 0, 0)),
            pl.BlockSpec((None, d, tf), lambda i, f: (w_idx, 0, f)),
            pl.BlockSpec((None, d, tf), lambda i, f: (w_idx, 0, n_f + f)),
            pl.BlockSpec((None, tf, d), lambda i, f: (w_idx, f, 0)),
        ],
        out_specs=pl.BlockSpec((tm, d), lambda i, f: (i, 0)),
        out_shape=jax.ShapeDtypeStruct((m, d), F32),
        scratch_shapes=[pltpu.VMEM((tm, d), BF16)],
        compiler_params=_params(("arbitrary", "arbitrary"), 58),
        name="ffn",
    )(h, mod, mod, mod, norm_g, w_in, w_in, w_out)


def _proj_kernel(x_ref, sh_ref, sc_ref, ng_ref, w_ref, hg_ref, *rest, n_norm_tiles, with_bf16):
    if with_bf16:
        o_ref, obf_ref, xn_ref = rest
    else:
        o_ref, xn_ref = rest
        obf_ref = None
    j = pl.program_id(1)

    @pl.when(j == 0)
    def _():
        xn_ref[...] = _norm_mod(x_ref[...], ng_ref[0], sh_ref[0], sc_ref[0]).astype(BF16)

    acc = jnp.dot(xn_ref[...], w_ref[...].astype(BF16), preferred_element_type=F32)
    tn = acc.shape[1]

    def store(val, sl):
        o_ref[:, sl] = val
        if obf_ref is not None:
            obf_ref[:, sl] = val.astype(BF16)

    @pl.when(j < n_norm_tiles)
    def _():
        for t in range(tn // HEAD_DIM):
            sl = slice(t * HEAD_DIM, (t + 1) * HEAD_DIM)
            a = acc[:, sl]
            ms = jnp.mean(a * a, axis=-1, keepdims=True)
            store(a * lax.rsqrt(ms + EPS) * hg_ref[:, sl], sl)

    @pl.when(j >= n_norm_tiles)
    def _():
        store(acc, slice(None))


def _proj(h, rows, mod, chunk0, norm_g, norm_idx, w, w_idx, head_gain, n_norm_cols, tn, with_bf16=False):
    m, d = h.shape
    n = w.shape[-1]
    tm = rows.tm
    assert n % tn == 0 and n_norm_cols % tn == 0 and tn % HEAD_DIM == 0
    out_shape = [jax.ShapeDtypeStruct((m, n), F32)]
    out_specs = [pl.BlockSpec((tm, tn), lambda i, j: (i, j))]
    if with_bf16:
        out_shape.append(jax.ShapeDtypeStruct((m, n), BF16))
        out_specs.append(pl.BlockSpec((tm, tn), lambda i, j: (i, j)))
    res = pl.pallas_call(
        functools.partial(_proj_kernel, n_norm_tiles=n_norm_cols // tn, with_bf16=with_bf16),
        grid=(m // tm, n // tn),
        in_specs=[
            pl.BlockSpec((tm, d), lambda i, j: (i, 0)),
            rows.cond_spec(chunk0, d),
            rows.cond_spec(chunk0 + 1, d),
            pl.BlockSpec((1, 1, d), lambda i, j: (norm_idx, 0, 0)),
            pl.BlockSpec((None, d, tn), lambda i, j: (w_idx, 0, j)),
            pl.BlockSpec((1, tn), lambda i, j: (0, j)),
        ],
        out_specs=out_specs,
        out_shape=out_shape,
        scratch_shapes=[pltpu.VMEM((tm, d), BF16)],
        compiler_params=_params(("arbitrary", "arbitrary"), 62),
        name="proj",
    )(h, mod, mod, norm_g, w, head_gain)
    return res if with_bf16 else res[0]


def _out_kernel(x_ref, w_ref, h_ref, gt_ref, o_ref):
    y = jnp.dot(x_ref[...].astype(BF16), w_ref[...].astype(BF16), preferred_element_type=F32)
    o_ref[...] = h_ref[...] + gt_ref[0] * y


def _out_proj(x, w, w_idx, h, rows, mod, chunk, tn):
    m, k = x.shape
    n = w.shape[-1]
    tm = rows.tm
    return pl.pallas_call(
        _out_kernel,
        grid=(m // tm, n // tn),
        in_specs=[
            pl.BlockSpec((tm, k), lambda i, j: (i, 0)),
            pl.BlockSpec((None, k, tn), lambda i, j: (w_idx, 0, j)),
            pl.BlockSpec((tm, tn), lambda i, j: (i, j)),
            rows.cond_spec(chunk, tn, col_of=lambda j: j),
        ],
        out_specs=pl.BlockSpec((tm, tn), lambda i, j: (i, j)),
        out_shape=jax.ShapeDtypeStruct((m, n), F32),
        compiler_params=_params(("arbitrary", "arbitrary"), 48),
        name="out_proj",
    )(x, w, h, mod)


def _dil_kernel(slopes_ref, q_ref, kc_ref, kp_ref, vc_ref, vp_ref, o_ref, lse_ref, *, dil, heads, head0):
    n = pl.program_id(1)
    hb = pl.program_id(2)
    scale = 1.0 / math.sqrt(HEAD_DIM)
    qi = lax.broadcasted_iota(jnp.int32, (A_BLOCK, A_BLOCK), 0)
    ki = lax.broadcasted_iota(jnp.int32, (A_BLOCK, A_BLOCK), 1)
    step_c = qi - ki
    step_p = step_c + A_BLOCK
    valid_c = step_c >= 0
    valid_p = jnp.logical_and(step_p <= A_BLOCK, n > 0)
    dist_c = (step_c * dil).astype(F32)
    dist_p = (step_p * dil).astype(F32)
    nt = (((1,), (1,)), ((), ()))
    for hh in range(heads):
        slope = slopes_ref[head0 + hb * heads + hh]
        bias_c = jnp.where(valid_c, -slope * dist_c, NEG_BIG)
        bias_p = jnp.where(valid_p, -slope * dist_p, NEG_BIG)
        sl = slice(hh * HEAD_DIM, (hh + 1) * HEAD_DIM)
        for res in range(dil):
            rows = pl.ds(res, A_BLOCK, stride=dil) if dil > 1 else slice(None)
            q = (q_ref[rows, sl] * scale).astype(BF16)
            s_c = lax.dot_general(q, kc_ref[rows, sl].astype(BF16), nt, preferred_element_type=F32) + bias_c
            s_p = lax.dot_general(q, kp_ref[rows, sl].astype(BF16), nt, preferred_element_type=F32) + bias_p
            mx = jnp.maximum(jnp.max(s_c, axis=-1, keepdims=True), jnp.max(s_p, axis=-1, keepdims=True))
            p_c = jnp.exp(s_c - mx)
            p_p = jnp.exp(s_p - mx)
            den = jnp.sum(p_c, axis=-1, keepdims=True) + jnp.sum(p_p, axis=-1, keepdims=True)
            o = jnp.dot(p_c.astype(BF16), vc_ref[rows, sl].astype(BF16), preferred_element_type=F32)
            o = o + jnp.dot(p_p.astype(BF16), vp_ref[rows, sl].astype(BF16), preferred_element_type=F32)
            o_ref[rows, sl] = o / den
            lse_ref[rows, sl] = jnp.broadcast_to(mx + jnp.log(den), (A_BLOCK, HEAD_DIM))


def _dilated_prompt(qkv, slopes, batch, seq, g, heads):
    dil = DIL_RATES[g]
    span = dil * A_BLOCK
    assert DIL_WINDOWS[g] // dil == A_BLOCK and seq % span == 0 and GROUP_HEADS % heads == 0
    nb = seq // span
    cw = heads * HEAD_DIM
    q_col = g * GROUP_COLS // cw
    k_col = (A_HEADS * HEAD_DIM + g * GROUP_COLS) // cw
    v_col = (2 * A_HEADS * HEAD_DIM + g * GROUP_COLS) // cw
    blk = (span, cw)

    def cur(col):
        return pl.BlockSpec(blk, lambda b, n, hb: (b * nb + n, col + hb))

    def prev(col):
        return pl.BlockSpec(blk, lambda b, n, hb: (b * nb + jnp.maximum(n - 1, 0), col + hb))

    out_spec = pl.BlockSpec(blk, lambda b, n, hb: (b * nb + n, hb))
    return pl.pallas_call(
        functools.partial(_dil_kernel, dil=dil, heads=heads, head0=g * GROUP_HEADS),
        grid=(batch, nb, GROUP_HEADS // heads),
        in_specs=[pl.BlockSpec(memory_space=pltpu.SMEM), cur(q_col), cur(k_col), prev(k_col), cur(v_col), prev(v_col)],
        out_specs=[out_spec, out_spec],
        out_shape=[jax.ShapeDtypeStruct((batch * seq, GROUP_COLS), F32)] * 2,
        compiler_params=_params(("arbitrary", "arbitrary", "arbitrary"), 40),
        name="dilated_prompt",
    )(slopes, qkv, qkv, qkv, qkv, qkv)


def _mix_out_kernel(o0_ref, o1_ref, o2_ref, l0_ref, l1_ref, l2_ref, w_ref, h_ref, gt_ref, out_ref, mix_ref):
    j = pl.program_id(1)

    @pl.when(j == 0)
    def _():
        ls = []
        for l_ref in (l0_ref, l1_ref, l2_ref):
            heads = [l_ref[:, hh * HEAD_DIM:(hh + 1) * HEAD_DIM] for hh in range(GROUP_HEADS)]
            htop = functools.reduce(jnp.maximum, heads)
            htot = functools.reduce(lambda a, b: a + b, [jnp.exp(l - htop) for l in heads])
            ls.append(htop + jnp.log(htot) - math.log(GROUP_HEADS))
        top = jnp.maximum(jnp.maximum(ls[0], ls[1]), ls[2])
        es = [jnp.exp(l - top) for l in ls]
        den = es[0] + es[1] + es[2]
        for g, o_ref in enumerate((o0_ref, o1_ref, o2_ref)):
            alpha = es[g] / den
            for hh in range(GROUP_HEADS):
                src = slice(hh * HEAD_DIM, (hh + 1) * HEAD_DIM)
                dst = slice(g * GROUP_COLS + hh * HEAD_DIM, g * GROUP_COLS + (hh + 1) * HEAD_DIM)
                mix_ref[:, dst] = (o_ref[:, src] * alpha).astype(BF16)

    y = jnp.dot(mix_ref[...], w_ref[...].astype(BF16), preferred_element_type=F32)
    out_ref[...] = h_ref[...] + gt_ref[0] * y


def _mix_out(outs, lses, w, w_idx, h, rows, mod, chunk, tn):
    m = h.shape[0]
    n = w.shape[-1]
    k = w.shape[-2]
    tm = rows.tm
    o_spec = pl.BlockSpec((tm, GROUP_COLS), lambda i, j: (i, 0))
    return pl.pallas_call(
        _mix_out_kernel,
        grid=(m // tm, n // tn),
        in_specs=[
            o_spec, o_spec, o_spec, o_spec, o_spec, o_spec,
            pl.BlockSpec((None, k, tn), lambda i, j: (w_idx, 0, j)),
            pl.BlockSpec((tm, tn), lambda i, j: (i, j)),
            rows.cond_spec(chunk, tn, col_of=lambda j: j),
        ],
        out_specs=pl.BlockSpec((tm, tn), lambda i, j: (i, j)),
        out_shape=jax.ShapeDtypeStruct((m, n), F32),
        scratch_shapes=[pltpu.VMEM((tm, k), BF16)],
        compiler_params=_params(("arbitrary", "arbitrary"), 58),
        name="mix_out",
    )(*outs, *lses, w, h, mod)


def _dil_sample_kernel(qkv_ref, c0_ref, c1_ref, c2_ref, o_ref, *, slopes):
    scale = 1.0 / math.sqrt(HEAD_DIM)
    nt = (((1,), (1,)), ((), ()))
    kbase = A_HEADS * HEAD_DIM
    vbase = 2 * A_HEADS * HEAD_DIM
    steps = (A_BLOCK - lax.broadcasted_iota(jnp.int32, (1, A_BLOCK), 1)).astype(F32)
    outs, glses = [], []
    for g, c_ref in enumerate((c0_ref, c1_ref, c2_ref)):
        dil = DIL_RATES[g]
        head_outs, head_lses = [], []
        for hh in range(GROUP_HEADS):
            col = (g * GROUP_HEADS + hh) * HEAD_DIM
            q = qkv_ref[0, :, col:col + HEAD_DIM]
            k_new = qkv_ref[0, :, kbase + col:kbase + col + HEAD_DIM]
            v_new = qkv_ref[0, :, vbase + col:vbase + col + HEAD_DIM]
            k_buf = c_ref[0, 0, :, hh * HEAD_DIM:(hh + 1) * HEAD_DIM]
            v_buf = c_ref[0, 0, :, GROUP_COLS + hh * HEAD_DIM:GROUP_COLS + (hh + 1) * HEAD_DIM]
            q8 = jnp.broadcast_to(q * scale, (8, HEAD_DIM)).astype(BF16)
            s_buf = lax.dot_general(q8, k_buf.astype(BF16), nt, preferred_element_type=F32)[0:1]
            s_buf = s_buf - slopes[g * GROUP_HEADS + hh] * dil * steps
            s_new = jnp.sum((q * scale).astype(BF16).astype(F32) * k_new.astype(BF16).astype(F32),
                            axis=-1, keepdims=True)
            mx = jnp.maximum(jnp.max(s_buf, axis=-1, keepdims=True), s_new)
            p_buf = jnp.exp(s_buf - mx)
            p_new = jnp.exp(s_new - mx)
            den = jnp.sum(p_buf, axis=-1, keepdims=True) + p_new
            p8 = jnp.broadcast_to(p_buf, (8, A_BLOCK)).astype(BF16)
            o = jnp.dot(p8, v_buf.astype(BF16), preferred_element_type=F32)[0:1]
            o = (o + p_new.astype(BF16).astype(F32) * v_new.astype(BF16).astype(F32)) / den
            head_outs.append(o)
            head_lses.append(mx + jnp.log(den))
        top = functools.reduce(jnp.maximum, head_lses)
        tot = functools.reduce(lambda a, b: a + b, [jnp.exp(l - top) for l in head_lses])
        glses.append(top + jnp.log(tot) - math.log(GROUP_HEADS))
        outs.append(head_outs)
    top = functools.reduce(jnp.maximum, glses)
    es = [jnp.exp(l - top) for l in glses]
    den = es[0] + es[1] + es[2]
    for g in range(N_GROUPS):
        alpha = es[g] / den
        for hh in range(GROUP_HEADS):
            col = (g * GROUP_HEADS + hh) * HEAD_DIM
            o_ref[0, :, col:col + HEAD_DIM] = outs[g][hh] * alpha


def _dilated_sample(qkv_s, caches, layer):
    db = qkv_s.shape[0]
    row_cols = 2 * GROUP_COLS
    views, specs = [], []
    for g, c in enumerate(caches):
        dil = DIL_RATES[g]
        assert c.shape[2] == DIL_WINDOWS[g] and c.shape[2] // dil == A_BLOCK
        views.append(c[:, :, ::dil].reshape(c.shape[0], db, A_BLOCK, row_cols))
        specs.append(pl.BlockSpec((1, 1, A_BLOCK, row_cols), lambda b: (layer, b, 0, 0)))
    out = pl.pallas_call(
        functools.partial(_dil_sample_kernel, slopes=tuple(_alibi_slopes())),
        grid=(db,),
        in_specs=[pl.BlockSpec((1, 1, QKV_COLS), lambda b: (b, 0, 0))] + specs,
        out_specs=pl.BlockSpec((1, 1, A_HEADS * HEAD_DIM), lambda b: (b, 0, 0)),
        out_shape=jax.ShapeDtypeStruct((db, 1, A_HEADS * HEAD_DIM), F32),
        compiler_params=_params(("arbitrary",), 32),
        name="dilated_sample",
    )(qkv_s.reshape(db, 1, QKV_COLS), *views)
    return out.reshape(db, A_HEADS * HEAD_DIM)


def _stick_terms(z):
    lo = jnp.minimum(z, 0.0)
    hi = jnp.maximum(z, 0.0)
    l = jnp.log(1.0 + jnp.exp(lo - hi))
    return lo - l, hi + l


def _sb_prompt_kernel(bias_ref, q_ref, k_ref, v_ref, o_ref, *, tq, tk, sub):
    h = pl.program_id(1)
    i = pl.program_id(2)
    n_sub = tk // sub
    bias = bias_ref[h]
    q = (q_ref[0] * (1.0 / math.sqrt(HEAD_DIM))).astype(BF16)
    nt = (((1,), (1,)), ((), ()))
    r = lax.broadcasted_iota(jnp.int32, (sub, sub), 0)
    c = lax.broadcasted_iota(jnp.int32, (sub, sub), 1)
    later_mat = jnp.where(r > c, 1.0, 0.0).astype(BF16)
    q_pos = lax.broadcasted_iota(jnp.int32, (tq, sub), 0)
    if tq == tk:
        n_full = i
    else:
        n_full = (i * tq) // tk
        q_pos = q_pos + (i * tq - n_full * tk)
    k_off = lax.broadcasted_iota(jnp.int32, (tq, sub), 1)

    def block(start, carry, acc, masked):
        k = k_ref[0, pl.ds(start, tk), :]
        v = v_ref[0, pl.ds(start, tk), :]
        z_all = lax.dot_general(q, k, nt, preferred_element_type=F32) + bias
        ws = [None] * n_sub
        for cidx in reversed(range(n_sub)):
            log_take, cost = _stick_terms(z_all[:, cidx * sub:(cidx + 1) * sub])
            if masked:
                causal = (k_off + cidx * sub) < q_pos
                cost = jnp.where(causal, cost, 0.0)
            later = jnp.dot(cost.astype(BF16), later_mat, preferred_element_type=F32)
            w = jnp.exp(log_take - later - carry)
            if masked:
                w = jnp.where(causal, w, 0.0)
            ws[cidx] = w.astype(BF16)
            carry = carry + jnp.sum(cost, axis=-1, keepdims=True)
        acc = acc + jnp.dot(jnp.concatenate(ws, axis=1), v, preferred_element_type=F32)
        return carry, acc

    zeros = (jnp.zeros((tq, 1), F32), jnp.zeros((tq, HEAD_DIM), F32))
    carry, acc = block(pl.multiple_of(n_full * tk, tk), *zeros, True)

    def body(t, state):
        start = pl.multiple_of((n_full - 1 - t) * tk, tk)
        return block(start, state[0], state[1], False)

    carry, acc = lax.fori_loop(0, n_full, body, (carry, acc))
    o_ref[0] = acc.astype(o_ref.dtype)


def _sb_prompt(q, kv_bf16, sb_bias, batch, seq, tq, tk):
    n_heads = q.shape[1] // HEAD_DIM
    assert tk % tq == 0 and seq % tk == 0
    q3 = q.reshape(batch, seq, n_heads * HEAD_DIM)
    kv3 = kv_bf16.reshape(batch, seq, 2 * n_heads * HEAD_DIM)
    out = pl.pallas_call(
        functools.partial(_sb_prompt_kernel, tq=tq, tk=tk, sub=min(tk, 2 * LANES)),
        grid=(batch, n_heads, seq // tq),
        in_specs=[
            pl.BlockSpec(memory_space=pltpu.SMEM),
            pl.BlockSpec((1, tq, HEAD_DIM), lambda b, h, i: (b, i, h)),
            pl.BlockSpec((1, seq, HEAD_DIM), lambda b, h, i: (b, 0, h)),
            pl.BlockSpec((1, seq, HEAD_DIM), lambda b, h, i: (b, 0, n_heads + h)),
        ],
        out_specs=pl.BlockSpec((1, tq, HEAD_DIM), lambda b, h, i: (b, i, h)),
        out_shape=jax.ShapeDtypeStruct((batch, seq, n_heads * HEAD_DIM), BF16),
        compiler_params=_params(("arbitrary", "arbitrary", "arbitrary"), 40),
        name="sb_prompt",
    )(sb_bias, q3, kv3, kv3)
    return out.reshape(batch * seq, n_heads * HEAD_DIM)


def _relayout_page(x_ref, o_ref, t, n_heads):
    page = o_ref.shape[1]
    keys = 2 * LANES // n_heads
    n = keys * n_heads
    i = lax.broadcasted_iota(jnp.int32, (n, n), 0)
    j = lax.broadcasted_iota(jnp.int32, (n, n), 1)
    perm = jnp.where(jnp.logical_and(i // keys == j % n_heads, i % keys == j // n_heads), 1.0, 0.0).astype(BF16)
    for c in range(page // keys):
        k_rows = x_ref[c * keys:(c + 1) * keys, 0:n_heads, :].reshape(n, HEAD_DIM)
        v_rows = x_ref[c * keys:(c + 1) * keys, n_heads:2 * n_heads, :].reshape(n, HEAD_DIM)
        both = jnp.concatenate([k_rows, v_rows], axis=1).astype(BF16)
        y = jnp.dot(perm, both, preferred_element_type=F32)
        for hh in range(n_heads):
            blk = y[hh * keys:(hh + 1) * keys].astype(BF16)
            o_ref[t, c * keys:(c + 1) * keys, hh * HEAD_DIM:(hh + 1) * HEAD_DIM] = blk[:, :HEAD_DIM]
            o_ref[t, c * keys:(c + 1) * keys, (n_heads + hh) * HEAD_DIM:(n_heads + hh + 1) * HEAD_DIM] = (
                blk[:, HEAD_DIM:])


def _sb_sample_step(qmat_ref, bias_ref, kv_ref, o_ref, acc_ref, carry_ref, n_heads):
    p = pl.program_id(1)
    hd = n_heads * HEAD_DIM
    pages_per_step, page = kv_ref.shape[0], kv_ref.shape[1]

    @pl.when(p == 0)
    def _():
        acc_ref[...] = jnp.zeros_like(acc_ref)
        carry_ref[...] = jnp.zeros_like(carry_ref)

    r = lax.broadcasted_iota(jnp.int32, (page, page), 0)
    c = lax.broadcasted_iota(jnp.int32, (page, page), 1)
    later_mat = jnp.where(c > r, 1.0, 0.0).astype(BF16)
    k_all = kv_ref[:, :, :hd].reshape(pages_per_step * page, hd)
    half = hd // 2
    z = (jnp.dot(k_all[:, :half], qmat_ref[0, :half, :], preferred_element_type=F32)
         + jnp.dot(k_all[:, half:], qmat_ref[0, half:, :], preferred_element_type=F32) + bias_ref[...])
    log_take, cost = _stick_terms(z)
    carry = carry_ref[...]
    ws = [None] * pages_per_step
    for t in reversed(range(pages_per_step)):
        rows = slice(t * page, (t + 1) * page)
        later = jnp.dot(later_mat, cost[rows].astype(BF16), preferred_element_type=F32)
        w = jnp.exp(log_take[rows] - later - carry)
        ws[t] = w.T[:MOD_ROWS].astype(BF16)
        carry = carry + jnp.sum(cost[rows], axis=0, keepdims=True)
    carry_ref[...] = carry
    w_all = jnp.concatenate(ws, axis=1)
    v_all = kv_ref[:, :, hd:].reshape(pages_per_step * page, hd)
    acc_ref[...] += jnp.dot(w_all, v_all, preferred_element_type=F32)

    @pl.when(p == pl.num_programs(1) - 1)
    def _():
        for hh in range(n_heads):
            sl = slice(hh * HEAD_DIM, (hh + 1) * HEAD_DIM)
            o_ref[0, :, sl] = acc_ref[hh:hh + 1, sl]


def _sb_sample_kernel(qmat_ref, bias_ref, kv_ref, o_ref, acc_ref, carry_ref, *, n_heads):
    _sb_sample_step(qmat_ref, bias_ref, kv_ref, o_ref, acc_ref, carry_ref, n_heads)


def _sb_sample_gather_kernel(pt_ref, qmat_ref, bias_ref, *refs, n_heads, pages_per_step):
    x_refs = refs[:pages_per_step]
    o_ref, kv_ref, acc_ref, carry_ref = refs[pages_per_step:]
    for t, x_ref in enumerate(x_refs):
        _relayout_page(x_ref, kv_ref, t, n_heads)
    _sb_sample_step(qmat_ref, bias_ref, kv_ref, o_ref, acc_ref, carry_ref, n_heads)


def _sb_sample_operands(q, sb_bias):
    hd = q.shape[1]
    n_heads = hd // HEAD_DIM
    assert n_heads <= MOD_ROWS
    head_of_row = jnp.arange(hd, dtype=jnp.int32) // HEAD_DIM
    sel = (head_of_row[:, None] == jnp.arange(LANES, dtype=jnp.int32)[None, :]).astype(F32)
    qmat = ((q * (1.0 / math.sqrt(HEAD_DIM)))[:, :, None] * sel[None]).astype(BF16)
    bias_row = jnp.zeros((1, LANES), F32).at[0, :n_heads].set(sb_bias.astype(F32))
    return qmat, bias_row


def _sb_sample_gather(q, cache_kv, page_table, sb_bias, pages_per_step):
    db, hd = q.shape
    n_phys, page, _, n_heads, _ = cache_kv.shape
    n_pages = page_table.shape[1]
    slabs = 2 * n_heads
    assert (2 * LANES) % n_heads == 0 and page % (2 * LANES // n_heads) == 0 and n_heads % 8 == 0
    steps = n_pages // pages_per_step
    qmat, bias_row = _sb_sample_operands(q, sb_bias)
    rows = cache_kv.reshape(n_phys, page, slabs, HEAD_DIM)

    def page_spec(t):
        return pl.BlockSpec((None, page, slabs, HEAD_DIM),
                            lambda b, p, pt: (pt[b, (steps - 1 - p) * pages_per_step + t], 0, 0, 0))

    out, kv_pages = pl.pallas_call(
        functools.partial(_sb_sample_gather_kernel, n_heads=n_heads, pages_per_step=pages_per_step),
        grid_spec=pltpu.PrefetchScalarGridSpec(
            num_scalar_prefetch=1,
            grid=(db, steps),
            in_specs=[
                pl.BlockSpec((1, hd, LANES), lambda b, p, pt: (b, 0, 0)),
                pl.BlockSpec((1, LANES), lambda b, p, pt: (0, 0)),
            ] + [page_spec(t) for t in range(pages_per_step)],
            out_specs=[
                pl.BlockSpec((1, 1, hd), lambda b, p, pt: (b, 0, 0)),
                pl.BlockSpec((pages_per_step, page, 2 * hd), lambda b, p, pt: (b * steps + steps - 1 - p, 0, 0)),
            ],
            scratch_shapes=[pltpu.VMEM((MOD_ROWS, hd), F32), pltpu.VMEM((1, LANES), F32)],
        ),
        out_shape=[jax.ShapeDtypeStruct((db, 1, hd), F32),
                   jax.ShapeDtypeStruct((db * n_pages, page, 2 * hd), BF16)],
        compiler_params=_params(("arbitrary", "arbitrary"), 48),
        name="sb_sample_gather",
    )(page_table, qmat, bias_row, *([rows] * pages_per_step))
    return out.reshape(db, hd), kv_pages


def _sb_sample(q, kv_pages, sb_bias, pages_per_step):
    db, hd = q.shape
    n_heads = hd // HEAD_DIM
    page = kv_pages.shape[1]
    steps = kv_pages.shape[0] // (db * pages_per_step)
    qmat, bias_row = _sb_sample_operands(q, sb_bias)
    out = pl.pallas_call(
        functools.partial(_sb_sample_kernel, n_heads=n_heads),
        grid=(db, steps),
        in_specs=[
            pl.BlockSpec((1, hd, LANES), lambda b, p: (b, 0, 0)),
            pl.BlockSpec((1, LANES), lambda b, p: (0, 0)),
            pl.BlockSpec((pages_per_step, page, 2 * hd), lambda b, p: (b * steps + steps - 1 - p, 0, 0)),
        ],
        out_specs=pl.BlockSpec((1, 1, hd), lambda b, p: (b, 0, 0)),
        out_shape=jax.ShapeDtypeStruct((db, 1, hd), F32),
        scratch_shapes=[pltpu.VMEM((MOD_ROWS, hd), F32), pltpu.VMEM((1, LANES), F32)],
        compiler_params=_params(("arbitrary", "arbitrary"), 32),
        name="sb_sample",
    )(qmat, bias_row, kv_pages)
    return out.reshape(db, hd)


def _shift_copies(refs, sems):
    n = len(refs) // 3
    copies = []
    for g in range(n):
        cache_ref, new_ref, out_ref = refs[g], refs[n + g], refs[2 * n + g]
        w = cache_ref.shape[2]
        copies.append(pltpu.make_async_copy(cache_ref.at[:, :, pl.ds(1, w - 1)], out_ref.at[:, :, pl.ds(0, w - 1)],
                                            sems.at[2 * g]))
        copies.append(pltpu.make_async_copy(new_ref, out_ref.at[:, :, pl.ds(w - 1, 1)], sems.at[2 * g + 1]))
    return copies


def _shift_kernel(*refs):
    copies = _shift_copies(refs[:-1], refs[-1])
    for cp in copies:
        cp.start()
    for cp in copies:
        cp.wait()


def _shift_windows(caches, news):
    n = len(caches)
    any_spec = pl.BlockSpec(memory_space=pl.ANY)
    return pl.pallas_call(
        _shift_kernel,
        in_specs=[any_spec] * (2 * n),
        out_specs=[any_spec] * n,
        out_shape=[jax.ShapeDtypeStruct(c.shape, c.dtype) for c in caches],
        scratch_shapes=[pltpu.SemaphoreType.DMA((2 * n,))],
        name="shift_windows",
    )(*caches, *news)


def kernel(x_prompt, x_sample, cache_win_g0, cache_win_g1, cache_win_g2, cache_kv, page_table, c_prompt, c_sample, w_mod, b_mod, norm_g, ffn_w_in, ffn_w_out, a_w_qkv, a_q_norm, a_k_norm, a_w_o, kv_norm, w_mod_kv, b_mod_kv, w_kv, sb_k_norm, b_w_q, b_q_norm, b_sb_bias, b_w_o):
    batch, seq, d = x_prompt.shape
    db, ds, _ = x_sample.shape
    depth = w_mod.shape[0]
    n_a = a_w_qkv.shape[0]
    d_ff = ffn_w_out.shape[2]
    n_b_heads = w_kv.shape[1] // (2 * HEAD_DIM)
    assert ds == 1 and db == 8 and db + batch <= MOD_ROWS
    caches = (cache_win_g0, cache_win_g1, cache_win_g2)

    c_all = jnp.zeros((MOD_ROWS, d), F32).at[:db].set(c_sample).at[db:db + batch].set(c_prompt)
    mod = _modulation(c_all, w_mod, b_mod)
    mod_kv = _modulation(c_all, w_mod_kv[None], b_mod_kv[None])
    mod_s = mod[:, :, :db].reshape(depth * N_MOD, db, d)
    mod_p = mod[:, :, db:db + batch].reshape(depth * N_MOD * batch, 1, d)
    modkv_s = mod_kv[:, :, :db].reshape(2, db, d)
    modkv_p = mod_kv[:, :, db:db + batch].reshape(2 * batch, 1, d)

    tm_p = _pick_tile(seq, 1024)
    rows_p = _Rows(batch * seq, batch, 1, tm_p)
    rows_s = _Rows(db, 1, db, db)
    paths = (
        dict(rows=rows_p, mod=mod_p, modkv=modkv_p, tf=_pick_tile(d_ff, 256)),
        dict(rows=rows_s, mod=mod_s, modkv=modkv_s, tf=_pick_tile(d_ff, 512)),
    )

    norm3 = norm_g.reshape(depth * 3, 1, d)
    kvn3 = kv_norm.reshape(1, 1, d)
    w_in = ffn_w_in.reshape(depth * 2, d, 2 * d_ff)
    w_out = ffn_w_out.reshape(depth * 2, d_ff, d)
    w_kv3 = w_kv[None]

    a_gain = [jnp.concatenate([jnp.tile(a_q_norm[l], A_HEADS), jnp.tile(a_k_norm[l], A_HEADS),
                               jnp.ones((A_HEADS * HEAD_DIM,), F32)])[None] for l in range(n_a)]
    kv_gain = jnp.concatenate([jnp.tile(sb_k_norm, n_b_heads), jnp.ones((n_b_heads * HEAD_DIM,), F32)])[None]
    b_gain = [jnp.tile(b_q_norm[j], n_b_heads)[None] for j in range(depth - n_a)]

    h_p = x_prompt.reshape(batch * seq, d)
    h_s = x_sample.reshape(db, d)
    tn_qkv = GROUP_COLS
    tn_d = _pick_tile(d, 512)
    tn_w = _pick_tile(d, 1024)
    win_p = [[] for _ in range(N_GROUPS)]
    win_s = [[] for _ in range(N_GROUPS)]
    kv_p = kv_s = kv_p_bf16 = None
    n_pages = page_table.shape[1]
    pages_per_step = 4 if n_pages % 4 == 0 else 1
    kv_pages = None
    slopes = jnp.asarray(_alibi_slopes(), F32)

    def ffn(h, path, l, which):
        return _ffn(h, path["rows"], path["mod"], (l * N_MOD + 6 * which), norm3, l * 3 + 2 * which,
                    w_in, w_out, l * 2 + which, path["tf"])

    for l in range(depth):
        h_p = ffn(h_p, paths[0], l, 0)
        h_s = ffn(h_s, paths[1], l, 0)
        if l < n_a:
            qkv_p = _proj(h_p, rows_p, mod_p, l * N_MOD + 3, norm3, l * 3 + 1, a_w_qkv, l, a_gain[l],
                          2 * A_HEADS * HEAD_DIM, tn_qkv)
            qkv_s = _proj(h_s, rows_s, mod_s, l * N_MOD + 3, norm3, l * 3 + 1, a_w_qkv, l, a_gain[l],
                          2 * A_HEADS * HEAD_DIM, tn_qkv)
            outs, lses = [], []
            for g in range(N_GROUPS):
                o_g, lse_g = _dilated_prompt(qkv_p, slopes, batch, seq, g, GROUP_HEADS if DIL_RATES[g] == 1 else 1)
                outs.append(o_g)
                lses.append(lse_g)
            h_p = _mix_out(outs, lses, a_w_o, l, h_p, rows_p, mod_p, l * N_MOD + 5, tn_d)
            mixed_s = _dilated_sample(qkv_s, caches, l)
            h_s = _out_proj(mixed_s, a_w_o, l, h_s, rows_s, mod_s, l * N_MOD + 5, tn_w)

            qkv_p3 = qkv_p.reshape(batch, seq, QKV_COLS)
            qkv_s5 = qkv_s.reshape(db, 1, 3, A_HEADS, HEAD_DIM)
            for g in range(N_GROUPS):
                hs = slice(g * GROUP_HEADS, (g + 1) * GROUP_HEADS)
                keep = min(DIL_WINDOWS[g], seq)
                kv_rows = [qkv_p3[:, seq - keep:, part * A_HEADS * HEAD_DIM + g * GROUP_COLS:
                                  part * A_HEADS * HEAD_DIM + (g + 1) * GROUP_COLS] for part in (1, 2)]
                win_p[g].append(jnp.stack(kv_rows, axis=2).reshape(batch, keep, 2, GROUP_HEADS, HEAD_DIM))
                win_s[g].append(qkv_s5[:, :, 1:3, hs])
        else:
            j = l - n_a
            q_p = _proj(h_p, rows_p, mod_p, l * N_MOD + 3, norm3, l * 3 + 1, b_w_q, j, b_gain[j],
                        n_b_heads * HEAD_DIM, tn_w)
            q_s = _proj(h_s, rows_s, mod_s, l * N_MOD + 3, norm3, l * 3 + 1, b_w_q, j, b_gain[j],
                        n_b_heads * HEAD_DIM, tn_w)
            o_p = _sb_prompt(q_p, kv_p_bf16, b_sb_bias[j], batch, seq, _pick_tile(seq, 512), _pick_tile(seq, 512))
            h_p = _out_proj(o_p, b_w_o, j, h_p, rows_p, mod_p, l * N_MOD + 5, tn_w)
            if kv_pages is None:
                o_s, kv_pages = _sb_sample_gather(q_s, cache_kv, page_table, b_sb_bias[j], pages_per_step)
            else:
                o_s = _sb_sample(q_s, kv_pages, b_sb_bias[j], pages_per_step)
            h_s = _out_proj(o_s, b_w_o, j, h_s, rows_s, mod_s, l * N_MOD + 5, tn_w)
        h_p = ffn(h_p, paths[0], l, 1)
        h_s = ffn(h_s, paths[1], l, 1)
        if l == n_a - 1:
            kv_p, kv_p_bf16 = _proj(h_p, rows_p, modkv_p, 0, kvn3, 0, w_kv3, 0, kv_gain,
                                    n_b_heads * HEAD_DIM, tn_w, with_bf16=True)
            kv_s = _proj(h_s, rows_s, modkv_s, 0, kvn3, 0, w_kv3, 0, kv_gain, n_b_heads * HEAD_DIM, tn_w)

    y_p = h_p.reshape(batch, seq, d)
    y_s = h_s.reshape(db, 1, d)
    win_p = [jnp.stack(w, axis=0) for w in win_p]
    assert all(c.shape[2] + ds > DIL_WINDOWS[g] for g, c in enumerate(caches))
    win_s = _shift_windows(caches, [jnp.stack(w, axis=0) for w in win_s])
    kv_p = kv_p.reshape(batch, seq, 2, n_b_heads, HEAD_DIM)
    kv_s = kv_s.reshape(db, 1, 2, n_b_heads, HEAD_DIM)
    return (y_p, y_s, win_p[0], win_p[1], win_p[2], kv_p, win_s[0], win_s[1], win_s[2], kv_s)
```

```python
import functools
import math

import numpy as np
import jax
import jax.numpy as jnp
from jax import lax
from jax.experimental import pallas as pl
from jax.experimental.pallas import tpu as pltpu

F32 = jnp.float32
BF16 = jnp.bfloat16

EPS = 1e-6
HEAD_DIM = 128
LANES = 128
N_MOD = 9
DIL_WINDOWS = (128, 512, 2048)
DIL_RATES = (1, 4, 16)
N_GROUPS = 3
GROUP_HEADS = 5
A_HEADS = N_GROUPS * GROUP_HEADS
A_BLOCK = 128
GROUP_COLS = GROUP_HEADS * HEAD_DIM
QKV_COLS = 3 * A_HEADS * HEAD_DIM
NEG_BIG = -1e30
MIB = 1024 * 1024
MOD_ROWS = 16


def _alibi_slopes():
    return [float(2.0 ** (-8.0 * (i + 1) / A_HEADS)) for i in range(A_HEADS)]


def _params(semantics, vmem_mib):
    return pltpu.CompilerParams(dimension_semantics=semantics, vmem_limit_bytes=vmem_mib * MIB)


def _pick_tile(n, preferred):
    if n <= preferred:
        return n
    t = (preferred // LANES) * LANES
    while t >= LANES:
        if n % t == 0:
            return t
        t -= LANES
    return n


class _Rows:
    def __init__(self, n_rows, groups, cond_rows, tm):
        self.n_rows = n_rows
        self.groups = groups
        self.cond_rows = cond_rows
        self.tm = tm
        self.tiles_per_group = (n_rows // groups) // tm
        assert self.tiles_per_group * tm * groups == n_rows

    def cond_spec(self, chunk, width, col_of=None):
        g, tpg = self.groups, self.tiles_per_group
        if col_of is None:
            return pl.BlockSpec((1, self.cond_rows, width), lambda i, j: (chunk * g + i // tpg, 0, 0))
        return pl.BlockSpec((1, self.cond_rows, width), lambda i, j: (chunk * g + i // tpg, 0, col_of(j)))


def _norm_mod(x, gain, shift, scale):
    ms = jnp.mean(x * x, axis=-1, keepdims=True)
    y = x * lax.rsqrt(ms + EPS) * gain
    return y * (1.0 + scale) + shift


def _mod_kernel(c_ref, w_ref, b_ref, o_ref):
    c = c_ref[...]
    a = (c * jax.nn.sigmoid(c)).astype(BF16)
    o_ref[0, 0] = jnp.dot(a, w_ref[0].astype(BF16), preferred_element_type=F32) + b_ref[0]


def _modulation(c_all, w, b):
    n_layers, d, n_out = w.shape
    n_chunks = n_out // d
    tn = _pick_tile(d, 1024)
    per_chunk = d // tn
    return pl.pallas_call(
        _mod_kernel,
        grid=(n_layers, n_out // tn),
        in_specs=[
            pl.BlockSpec((MOD_ROWS, d), lambda l, j: (0, 0)),
            pl.BlockSpec((1, d, tn), lambda l, j: (l, 0, j)),
            pl.BlockSpec((1, 1, tn), lambda l, j: (l, 0, j)),
        ],
        out_specs=pl.BlockSpec((1, 1, MOD_ROWS, tn), lambda l, j: (l, j // per_chunk, 0, j % per_chunk)),
        out_shape=jax.ShapeDtypeStruct((n_layers, n_chunks, MOD_ROWS, d), F32),
        compiler_params=_params(("arbitrary", "arbitrary"), 40),
        name="modulation",
    )(c_all, w, b.reshape(n_layers, 1, n_out))


def _ffn_kernel(x_ref, sh_ref, sc_ref, gt_ref, ng_ref, wg_ref, wu_ref, wo_ref, o_ref, xn_ref):
    f = pl.program_id(1)

    @pl.when(f == 0)
    def _():
        u = _norm_mod(x_ref[...], ng_ref[0], sh_ref[0], sc_ref[0])
        xn_ref[...] = u.astype(BF16)
        o_ref[...] = jnp.zeros_like(o_ref)

    xn = xn_ref[...]
    gate = jnp.dot(xn, wg_ref[...].astype(BF16), preferred_element_type=F32)
    up = jnp.dot(xn, wu_ref[...].astype(BF16), preferred_element_type=F32)
    act = (gate * jax.nn.sigmoid(gate) * up).astype(BF16)
    o_ref[...] += jnp.dot(act, wo_ref[...].astype(BF16), preferred_element_type=F32)

    @pl.when(f == pl.num_programs(1) - 1)
    def _():
        o_ref[...] = x_ref[...] + 0.5 * gt_ref[0] * o_ref[...]


def _ffn(h, rows, mod, chunk0, norm_g, norm_idx, w_in, w_out, w_idx, tf):
    m, d = h.shape
    d_ff = w_out.shape[1]
    n_f = d_ff // tf
    tm = rows.tm
    return pl.pallas_call(
        _ffn_kernel,
        grid=(m // tm, n_f),
        in_specs=[
            pl.BlockSpec((tm, d), lambda i, f: (i, 0)),
            rows.cond_spec(chunk0, d),
            rows.cond_spec(chunk0 + 1, d),
            rows.cond_spec(chunk0 + 2, d),
            pl.BlockSpec((1, 1, d), lambda i, f: (norm_idx, 0, 0)),
            pl.BlockSpec((None, d, tf), lambda i, f: (w_idx, 0, f)),
            pl.BlockSpec((None, d, tf), lambda i, f: (w_idx, 0, n_f + f)),
            pl.BlockSpec((None, tf, d), lambda i, f: (w_idx, f, 0)),
        ],
        out_specs=pl.BlockSpec((tm, d), lambda i, f: (i, 0)),
        out_shape=jax.ShapeDtypeStruct((m, d), F32),
        scratch_shapes=[pltpu.VMEM((tm, d), BF16)],
        compiler_params=_params(("arbitrary", "arbitrary"), 58),
        name="ffn",
    )(h, mod, mod, mod, norm_g, w_in, w_in, w_out)


def _proj_kernel(x_ref, sh_ref, sc_ref, ng_ref, w_ref, hg_ref, *rest, n_norm_tiles, with_bf16):
    if with_bf16:
        o_ref, obf_ref, xn_ref = rest
    else:
        o_ref, xn_ref = rest
        obf_ref = None
    j = pl.program_id(1)

    @pl.when(j == 0)
    def _():
        xn_ref[...] = _norm_mod(x_ref[...], ng_ref[0], sh_ref[0], sc_ref[0]).astype(BF16)

    acc = jnp.dot(xn_ref[...], w_ref[...].astype(BF16), preferred_element_type=F32)
    tn = acc.shape[1]

    def store(val, sl):
        o_ref[:, sl] = val
        if obf_ref is not None:
            obf_ref[:, sl] = val.astype(BF16)

    @pl.when(j < n_norm_tiles)
    def _():
        for t in range(tn // HEAD_DIM):
            sl = slice(t * HEAD_DIM, (t + 1) * HEAD_DIM)
            a = acc[:, sl]
            ms = jnp.mean(a * a, axis=-1, keepdims=True)
            store(a * lax.rsqrt(ms + EPS) * hg_ref[:, sl], sl)

    @pl.when(j >= n_norm_tiles)
    def _():
        store(acc, slice(None))


def _proj(h, rows, mod, chunk0, norm_g, norm_idx, w, w_idx, head_gain, n_norm_cols, tn, with_bf16=False):
    m, d = h.shape
    n = w.shape[-1]
    tm = rows.tm
    assert n % tn == 0 and n_norm_cols % tn == 0 and tn % HEAD_DIM == 0
    out_shape = [jax.ShapeDtypeStruct((m, n), F32)]
    out_specs = [pl.BlockSpec((tm, tn), lambda i, j: (i, j))]
    if with_bf16:
        out_shape.append(jax.ShapeDtypeStruct((m, n), BF16))
        out_specs.append(pl.BlockSpec((tm, tn), lambda i, j: (i, j)))
    res = pl.pallas_call(
        functools.partial(_proj_kernel, n_norm_tiles=n_norm_cols // tn, with_bf16=with_bf16),
        grid=(m // tm, n // tn),
        in_specs=[
            pl.BlockSpec((tm, d), lambda i, j: (i, 0)),
            rows.cond_spec(chunk0, d),
            rows.cond_spec(chunk0 + 1, d),
            pl.BlockSpec((1, 1, d), lambda i, j: (norm_idx, 0, 0)),
            pl.BlockSpec((None, d, tn), lambda i, j: (w_idx, 0, j)),
            pl.BlockSpec((1, tn), lambda i, j: (0, j)),
        ],
        out_specs=out_specs,
        out_shape=out_shape,
        scratch_shapes=[pltpu.VMEM((tm, d), BF16)],
        compiler_params=_params(("arbitrary", "arbitrary"), 62),
        name="proj",
    )(h, mod, mod, norm_g, w, head_gain)
    return res if with_bf16 else res[0]


def _out_kernel(x_ref, w_ref, h_ref, gt_ref, o_ref):
    y = jnp.dot(x_ref[...].astype(BF16), w_ref[...].astype(BF16), preferred_element_type=F32)
    o_ref[...] = h_ref[...] + gt_ref[0] * y


def _out_proj(x, w, w_idx, h, rows, mod, chunk, tn):
    m, k = x.shape
    n = w.shape[-1]
    tm = rows.tm
    return pl.pallas_call(
        _out_kernel,
        grid=(m // tm, n // tn),
        in_specs=[
            pl.BlockSpec((tm, k), lambda i, j: (i, 0)),
            pl.BlockSpec((None, k, tn), lambda i, j: (w_idx, 0, j)),
            pl.BlockSpec((tm, tn), lambda i, j: (i, j)),
            rows.cond_spec(chunk, tn, col_of=lambda j: j),
        ],
        out_specs=pl.BlockSpec((tm, tn), lambda i, j: (i, j)),
        out_shape=jax.ShapeDtypeStruct((m, n), F32),
        compiler_params=_params(("arbitrary", "arbitrary"), 48),
        name="out_proj",
    )(x, w, h, mod)


def _dil_kernel(slopes_ref, q_ref, kc_ref, kp_ref, vc_ref, vp_ref, o_ref, lse_ref, *, dil, heads, head0):
    n = pl.program_id(1)
    hb = pl.program_id(2)
    scale = 1.0 / math.sqrt(HEAD_DIM)
    qi = lax.broadcasted_iota(jnp.int32, (A_BLOCK, A_BLOCK), 0)
    ki = lax.broadcasted_iota(jnp.int32, (A_BLOCK, A_BLOCK), 1)
    step_c = qi - ki
    step_p = step_c + A_BLOCK
    valid_c = step_c >= 0
    valid_p = jnp.logical_and(step_p <= A_BLOCK, n > 0)
    dist_c = (step_c * dil).astype(F32)
    dist_p = (step_p * dil).astype(F32)
    nt = (((1,), (1,)), ((), ()))
    for hh in range(heads):
        slope = slopes_ref[head0 + hb * heads + hh]
        bias_c = jnp.where(valid_c, -slope * dist_c, NEG_BIG)
        bias_p = jnp.where(valid_p, -slope * dist_p, NEG_BIG)
        sl = slice(hh * HEAD_DIM, (hh + 1) * HEAD_DIM)
        for res in range(dil):
            rows = pl.ds(res, A_BLOCK, stride=dil) if dil > 1 else slice(None)
            q = (q_ref[rows, sl] * scale).astype(BF16)
            s_c = lax.dot_general(q, kc_ref[rows, sl].astype(BF16), nt, preferred_element_type=F32) + bias_c
            s_p = lax.dot_general(q, kp_ref[rows, sl].astype(BF16), nt, preferred_element_type=F32) + bias_p
            mx = jnp.maximum(jnp.max(s_c, axis=-1, keepdims=True), jnp.max(s_p, axis=-1, keepdims=True))
            p_c = jnp.exp(s_c - mx)
            p_p = jnp.exp(s_p - mx)
            den = jnp.sum(p_c, axis=-1, keepdims=True) + jnp.sum(p_p, axis=-1, keepdims=True)
            o = jnp.dot(p_c.astype(BF16), vc_ref[rows, sl].astype(BF16), preferred_element_type=F32)
            o = o + jnp.dot(p_p.astype(BF16), vp_ref[rows, sl].astype(BF16), preferred_element_type=F32)
            o_ref[rows, sl] = o / den
            lse_ref[rows, sl] = jnp.broadcast_to(mx + jnp.log(den), (A_BLOCK, HEAD_DIM))


def _dilated_prompt(qkv, slopes, batch, seq, g, heads):
    dil = DIL_RATES[g]
    span = dil * A_BLOCK
    assert DIL_WINDOWS[g] // dil == A_BLOCK and seq % span == 0 and GROUP_HEADS % heads == 0
    nb = seq // span
    cw = heads * HEAD_DIM
    q_col = g * GROUP_COLS // cw
    k_col = (A_HEADS * HEAD_DIM + g * GROUP_COLS) // cw
    v_col = (2 * A_HEADS * HEAD_DIM + g * GROUP_COLS) // cw
    blk = (span, cw)

    def cur(col):
        return pl.BlockSpec(blk, lambda b, n, hb: (b * nb + n, col + hb))

    def prev(col):
        return pl.BlockSpec(blk, lambda b, n, hb: (b * nb + jnp.maximum(n - 1, 0), col + hb))

    out_spec = pl.BlockSpec(blk, lambda b, n, hb: (b * nb + n, hb))
    return pl.pallas_call(
        functools.partial(_dil_kernel, dil=dil, heads=heads, head0=g * GROUP_HEADS),
        grid=(batch, nb, GROUP_HEADS // heads),
        in_specs=[pl.BlockSpec(memory_space=pltpu.SMEM), cur(q_col), cur(k_col), prev(k_col), cur(v_col), prev(v_col)],
        out_specs=[out_spec, out_spec],
        out_shape=[jax.ShapeDtypeStruct((batch * seq, GROUP_COLS), F32)] * 2,
        compiler_params=_params(("arbitrary", "arbitrary", "arbitrary"), 40),
        name="dilated_prompt",
    )(slopes, qkv, qkv, qkv, qkv, qkv)


def _mix_out_kernel(o0_ref, o1_ref, o2_ref, l0_ref, l1_ref, l2_ref, w_ref, h_ref, gt_ref, out_ref, mix_ref):
    j = pl.program_id(1)

    @pl.when(j == 0)
    def _():
        ls = []
        for l_ref in (l0_ref, l1_ref, l2_ref):
            heads = [l_ref[:, hh * HEAD_DIM:(hh + 1) * HEAD_DIM] for hh in range(GROUP_HEADS)]
            htop = functools.reduce(jnp.maximum, heads)
            htot = functools.reduce(lambda a, b: a + b, [jnp.exp(l - htop) for l in heads])
            ls.append(htop + jnp.log(htot) - math.log(GROUP_HEADS))
        top = jnp.maximum(jnp.maximum(ls[0], ls[1]), ls[2])
        es = [jnp.exp(l - top) for l in ls]
        den = es[0] + es[1] + es[2]
        for g, o_ref in enumerate((o0_ref, o1_ref, o2_ref)):
            alpha = es[g] / den
            for hh in range(GROUP_HEADS):
                src = slice(hh * HEAD_DIM, (hh + 1) * HEAD_DIM)
                dst = slice(g * GROUP_COLS + hh * HEAD_DIM, g * GROUP_COLS + (hh + 1) * HEAD_DIM)
                mix_ref[:, dst] = (o_ref[:, src] * alpha).astype(BF16)

    y = jnp.dot(mix_ref[...], w_ref[...].astype(BF16), preferred_element_type=F32)
    out_ref[...] = h_ref[...] + gt_ref[0] * y


def _mix_out(outs, lses, w, w_idx, h, rows, mod, chunk, tn):
    m = h.shape[0]
    n = w.shape[-1]
    k = w.shape[-2]
    tm = rows.tm
    o_spec = pl.BlockSpec((tm, GROUP_COLS), lambda i, j: (i, 0))
    return pl.pallas_call(
        _mix_out_kernel,
        grid=(m // tm, n // tn),
        in_specs=[
            o_spec, o_spec, o_spec, o_spec, o_spec, o_spec,
            pl.BlockSpec((None, k, tn), lambda i, j: (w_idx, 0, j)),
            pl.BlockSpec((tm, tn), lambda i, j: (i, j)),
            rows.cond_spec(chunk, tn, col_of=lambda j: j),
        ],
        out_specs=pl.BlockSpec((tm, tn), lambda i, j: (i, j)),
        out_shape=jax.ShapeDtypeStruct((m, n), F32),
        scratch_shapes=[pltpu.VMEM((tm, k), BF16)],
        compiler_params=_params(("arbitrary", "arbitrary"), 58),
        name="mix_out",
    )(*outs, *lses, w, h, mod)


def _dil_sample_kernel(qkv_ref, c0_ref, c1_ref, c2_ref, o_ref, *, slopes):
    scale = 1.0 / math.sqrt(HEAD_DIM)
    nt = (((1,), (1,)), ((), ()))
    kbase = A_HEADS * HEAD_DIM
    vbase = 2 * A_HEADS * HEAD_DIM
    steps = (A_BLOCK - lax.broadcasted_iota(jnp.int32, (1, A_BLOCK), 1)).astype(F32)
    outs, glses = [], []
    for g, c_ref in enumerate((c0_ref, c1_ref, c2_ref)):
        dil = DIL_RATES[g]
        head_outs, head_lses = [], []
        for hh in range(GROUP_HEADS):
            col = (g * GROUP_HEADS + hh) * HEAD_DIM
            q = qkv_ref[0, :, col:col + HEAD_DIM]
            k_new = qkv_ref[0, :, kbase + col:kbase + col + HEAD_DIM]
            v_new = qkv_ref[0, :, vbase + col:vbase + col + HEAD_DIM]
            k_buf = c_ref[0, 0, :, hh * HEAD_DIM:(hh + 1) * HEAD_DIM]
            v_buf = c_ref[0, 0, :, GROUP_COLS + hh * HEAD_DIM:GROUP_COLS + (hh + 1) * HEAD_DIM]
            q8 = jnp.broadcast_to(q * scale, (8, HEAD_DIM)).astype(BF16)
            s_buf = lax.dot_general(q8, k_buf.astype(BF16), nt, preferred_element_type=F32)[0:1]
            s_buf = s_buf - slopes[g * GROUP_HEADS + hh] * dil * steps
            s_new = jnp.sum((q * scale).astype(BF16).astype(F32) * k_new.astype(BF16).astype(F32),
                            axis=-1, keepdims=True)
            mx = jnp.maximum(jnp.max(s_buf, axis=-1, keepdims=True), s_new)
            p_buf = jnp.exp(s_buf - mx)
            p_new = jnp.exp(s_new - mx)
            den = jnp.sum(p_buf, axis=-1, keepdims=True) + p_new
            p8 = jnp.broadcast_to(p_buf, (8, A_BLOCK)).astype(BF16)
            o = jnp.dot(p8, v_buf.astype(BF16), preferred_element_type=F32)[0:1]
            o = (o + p_new.astype(BF16).astype(F32) * v_new.astype(BF16).astype(F32)) / den
            head_outs.append(o)
            head_lses.append(mx + jnp.log(den))
        top = functools.reduce(jnp.maximum, head_lses)
        tot = functools.reduce(lambda a, b: a + b, [jnp.exp(l - top) for l in head_lses])
        glses.append(top + jnp.log(tot) - math.log(GROUP_HEADS))
        outs.append(head_outs)
    top = functools.reduce(jnp.maximum, glses)
    es = [jnp.exp(l - top) for l in glses]
    den = es[0] + es[1] + es[2]
    for g in range(N_GROUPS):
        alpha = es[g] / den
        for hh in range(GROUP_HEADS):
            col = (g * GROUP_HEADS + hh) * HEAD_DIM
            o_ref[0, :, col:col + HEAD_DIM] = outs[g][hh] * alpha


def _dilated_sample(qkv_s, caches, layer):
    db = qkv_s.shape[0]
    row_cols = 2 * GROUP_COLS
    views, specs = [], []
    for g, c in enumerate(caches):
        dil = DIL_RATES[g]
        assert c.shape[2] == DIL_WINDOWS[g] and c.shape[2] // dil == A_BLOCK
        views.append(c[:, :, ::dil].reshape(c.shape[0], db, A_BLOCK, row_cols))
        specs.append(pl.BlockSpec((1, 1, A_BLOCK, row_cols), lambda b: (layer, b, 0, 0)))
    out = pl.pallas_call(
        functools.partial(_dil_sample_kernel, slopes=tuple(_alibi_slopes())),
        grid=(db,),
        in_specs=[pl.BlockSpec((1, 1, QKV_COLS), lambda b: (b, 0, 0))] + specs,
        out_specs=pl.BlockSpec((1, 1, A_HEADS * HEAD_DIM), lambda b: (b, 0, 0)),
        out_shape=jax.ShapeDtypeStruct((db, 1, A_HEADS * HEAD_DIM), F32),
        compiler_params=_params(("arbitrary",), 32),
        name="dilated_sample",
    )(qkv_s.reshape(db, 1, QKV_COLS), *views)
    return out.reshape(db, A_HEADS * HEAD_DIM)


def _stick_terms(z):
    lo = jnp.minimum(z, 0.0)
    hi = jnp.maximum(z, 0.0)
    l = jnp.log(1.0 + jnp.exp(lo - hi))
    return lo - l, hi + l


def _sb_prompt_kernel(bias_ref, q_ref, k_ref, v_ref, o_ref, *, tq, tk, sub):
    h = pl.program_id(1)
    i = pl.program_id(2)
    n_sub = tk // sub
    bias = bias_ref[h]
    q = (q_ref[0] * (1.0 / math.sqrt(HEAD_DIM))).astype(BF16)
    nt = (((1,), (1,)), ((), ()))
    r = lax.broadcasted_iota(jnp.int32, (sub, sub), 0)
    c = lax.broadcasted_iota(jnp.int32, (sub, sub), 1)
    later_mat = jnp.where(r > c, 1.0, 0.0).astype(BF16)
    q_pos = lax.broadcasted_iota(jnp.int32, (tq, sub), 0)
    if tq == tk:
        n_full = i
    else:
        n_full = (i * tq) // tk
        q_pos = q_pos + (i * tq - n_full * tk)
    k_off = lax.broadcasted_iota(jnp.int32, (tq, sub), 1)

    def block(start, carry, acc, masked):
        k = k_ref[0, pl.ds(start, tk), :]
        v = v_ref[0, pl.ds(start, tk), :]
        z_all = lax.dot_general(q, k, nt, preferred_element_type=F32) + bias
        ws = [None] * n_sub
        for cidx in reversed(range(n_sub)):
            log_take, cost = _stick_terms(z_all[:, cidx * sub:(cidx + 1) * sub])
            if masked:
                causal = (k_off + cidx * sub) < q_pos
                cost = jnp.where(causal, cost, 0.0)
            later = jnp.dot(cost.astype(BF16), later_mat, preferred_element_type=F32)
            w = jnp.exp(log_take - later - carry)
            if masked:
                w = jnp.where(causal, w, 0.0)
            ws[cidx] = w.astype(BF16)
            carry = carry + jnp.sum(cost, axis=-1, keepdims=True)
        acc = acc + jnp.dot(jnp.concatenate(ws, axis=1), v, preferred_element_type=F32)
        return carry, acc

    zeros = (jnp.zeros((tq, 1), F32), jnp.zeros((tq, HEAD_DIM), F32))
    carry, acc = block(pl.multiple_of(n_full * tk, tk), *zeros, True)

    def body(t, state):
        start = pl.multiple_of((n_full - 1 - t) * tk, tk)
        return block(start, state[0], state[1], False)

    carry, acc = lax.fori_loop(0, n_full, body, (carry, acc))
    o_ref[0] = acc.astype(o_ref.dtype)


def _sb_prompt(q, kv_bf16, sb_bias, batch, seq, tq, tk):
    n_heads = q.shape[1] // HEAD_DIM
    assert tk % tq == 0 and seq % tk == 0
    q3 = q.reshape(batch, seq, n_heads * HEAD_DIM)
    kv3 = kv_bf16.reshape(batch, seq, 2 * n_heads * HEAD_DIM)
    out = pl.pallas_call(
        functools.partial(_sb_prompt_kernel, tq=tq, tk=tk, sub=min(tk, 2 * LANES)),
        grid=(batch, n_heads, seq // tq),
        in_specs=[
            pl.BlockSpec(memory_space=pltpu.SMEM),
            pl.BlockSpec((1, tq, HEAD_DIM), lambda b, h, i: (b, i, h)),
            pl.BlockSpec((1, seq, HEAD_DIM), lambda b, h, i: (b, 0, h)),
            pl.BlockSpec((1, seq, HEAD_DIM), lambda b, h, i: (b, 0, n_heads + h)),
        ],
        out_specs=pl.BlockSpec((1, tq, HEAD_DIM), lambda b, h, i: (b, i, h)),
        out_shape=jax.ShapeDtypeStruct((batch, seq, n_heads * HEAD_DIM), BF16),
        compiler_params=_params(("arbitrary", "arbitrary", "arbitrary"), 40),
        name="sb_prompt",
    )(sb_bias, q3, kv3, kv3)
    return out.reshape(batch * seq, n_heads * HEAD_DIM)


def _relayout_page(x_ref, o_ref, t, n_heads):
    page = o_ref.shape[1]
    keys = 2 * LANES // n_heads
    n = keys * n_heads
    i = lax.broadcasted_iota(jnp.int32, (n, n), 0)
    j = lax.broadcasted_iota(jnp.int32, (n, n), 1)
    perm = jnp.where(jnp.logical_and(i // keys == j % n_heads, i % keys == j // n_heads), 1.0, 0.0).astype(BF16)
    for c in range(page // keys):
        k_rows = x_ref[c * keys:(c + 1) * keys, 0:n_heads, :].reshape(n, HEAD_DIM)
        v_rows = x_ref[c * keys:(c + 1) * keys, n_heads:2 * n_heads, :].reshape(n, HEAD_DIM)
        both = jnp.concatenate([k_rows, v_rows], axis=1).astype(BF16)
        y = jnp.dot(perm, both, preferred_element_type=F32)
        for hh in range(n_heads):
            blk = y[hh * keys:(hh + 1) * keys].astype(BF16)
            o_ref[t, c * keys:(c + 1) * keys, hh * HEAD_DIM:(hh + 1) * HEAD_DIM] = blk[:, :HEAD_DIM]
            o_ref[t, c * keys:(c + 1) * keys, (n_heads + hh) * HEAD_DIM:(n_heads + hh + 1) * HEAD_DIM] = (
                blk[:, HEAD_DIM:])


def _sb_sample_step(qmat_ref, bias_ref, kv_ref, o_ref, acc_ref, carry_ref, n_heads):
    p = pl.program_id(1)
    hd = n_heads * HEAD_DIM
    pages_per_step, page = kv_ref.shape[0], kv_ref.shape[1]

    @pl.when(p == 0)
    def _():
        acc_ref[...] = jnp.zeros_like(acc_ref)
        carry_ref[...] = jnp.zeros_like(carry_ref)

    r = lax.broadcasted_iota(jnp.int32, (page, page), 0)
    c = lax.broadcasted_iota(jnp.int32, (page, page), 1)
    later_mat = jnp.where(c > r, 1.0, 0.0).astype(BF16)
    k_all = kv_ref[:, :, :hd].reshape(pages_per_step * page, hd)
    half = hd // 2
    z = (jnp.dot(k_all[:, :half], qmat_ref[0, :half, :], preferred_element_type=F32)
         + jnp.dot(k_all[:, half:], qmat_ref[0, half:, :], preferred_element_type=F32) + bias_ref[...])
    log_take, cost = _stick_terms(z)
    carry = carry_ref[...]
    ws = [None] * pages_per_step
    for t in reversed(range(pages_per_step)):
        rows = slice(t * page, (t + 1) * page)
        later = jnp.dot(later_mat, cost[rows].astype(BF16), preferred_element_type=F32)
        w = jnp.exp(log_take[rows] - later - carry)
        ws[t] = w.T[:MOD_ROWS].astype(BF16)
        carry = carry + jnp.sum(cost[rows], axis=0, keepdims=True)
    carry_ref[...] = carry
    w_all = jnp.concatenate(ws, axis=1)
    v_all = kv_ref[:, :, hd:].reshape(pages_per_step * page, hd)
    acc_ref[...] += jnp.dot(w_all, v_all, preferred_element_type=F32)

    @pl.when(p == pl.num_programs(1) - 1)
    def _():
        for hh in range(n_heads):
            sl = slice(hh * HEAD_DIM, (hh + 1) * HEAD_DIM)
            o_ref[0, :, sl] = acc_ref[hh:hh + 1, sl]


def _sb_sample_kernel(qmat_ref, bias_ref, kv_ref, o_ref, acc_ref, carry_ref, *, n_heads):
    _sb_sample_step(qmat_ref, bias_ref, kv_ref, o_ref, acc_ref, carry_ref, n_heads)


def _sb_sample_gather_kernel(pt_ref, qmat_ref, bias_ref, *refs, n_heads, pages_per_step):
    x_refs = refs[:pages_per_step]
    o_ref, kv_ref, acc_ref, carry_ref = refs[pages_per_step:]
    for t, x_ref in enumerate(x_refs):
        _relayout_page(x_ref, kv_ref, t, n_heads)
    _sb_sample_step(qmat_ref, bias_ref, kv_ref, o_ref, acc_ref, carry_ref, n_heads)


def _sb_sample_operands(q, sb_bias):
    hd = q.shape[1]
    n_heads = hd // HEAD_DIM
    assert n_heads <= MOD_ROWS
    head_of_row = jnp.arange(hd, dtype=jnp.int32) // HEAD_DIM
    sel = (head_of_row[:, None] == jnp.arange(LANES, dtype=jnp.int32)[None, :]).astype(F32)
    qmat = ((q * (1.0 / math.sqrt(HEAD_DIM)))[:, :, None] * sel[None]).astype(BF16)
    bias_row = jnp.zeros((1, LANES), F32).at[0, :n_heads].set(sb_bias.astype(F32))
    return qmat, bias_row


def _sb_sample_gather(q, cache_kv, page_table, sb_bias, pages_per_step):
    db, hd = q.shape
    n_phys, page, _, n_heads, _ = cache_kv.shape
    n_pages = page_table.shape[1]
    slabs = 2 * n_heads
    assert (2 * LANES) % n_heads == 0 and page % (2 * LANES // n_heads) == 0 and n_heads % 8 == 0
    steps = n_pages // pages_per_step
    qmat, bias_row = _sb_sample_operands(q, sb_bias)
    rows = cache_kv.reshape(n_phys, page, slabs, HEAD_DIM)

    def page_spec(t):
        return pl.BlockSpec((None, page, slabs, HEAD_DIM),
                            lambda b, p, pt: (pt[b, (steps - 1 - p) * pages_per_step + t], 0, 0, 0))

    out, kv_pages = pl.pallas_call(
        functools.partial(_sb_sample_gather_kernel, n_heads=n_heads, pages_per_step=pages_per_step),
        grid_spec=pltpu.PrefetchScalarGridSpec(
            num_scalar_prefetch=1,
            grid=(db, steps),
            in_specs=[
                pl.BlockSpec((1, hd, LANES), lambda b, p, pt: (b, 0, 0)),
                pl.BlockSpec((1, LANES), lambda b, p, pt: (0, 0)),
            ] + [page_spec(t) for t in range(pages_per_step)],
            out_specs=[
                pl.BlockSpec((1, 1, hd), lambda b, p, pt: (b, 0, 0)),
                pl.BlockSpec((pages_per_step, page, 2 * hd), lambda b, p, pt: (b * steps + steps - 1 - p, 0, 0)),
            ],
            scratch_shapes=[pltpu.VMEM((MOD_ROWS, hd), F32), pltpu.VMEM((1, LANES), F32)],
        ),
        out_shape=[jax.ShapeDtypeStruct((db, 1, hd), F32),
                   jax.ShapeDtypeStruct((db * n_pages, page, 2 * hd), BF16)],
        compiler_params=_params(("arbitrary", "arbitrary"), 48),
        name="sb_sample_gather",
    )(page_table, qmat, bias_row, *([rows] * pages_per_step))
    return out.reshape(db, hd), kv_pages


def _sb_sample(q, kv_pages, sb_bias, pages_per_step):
    db, hd = q.shape
    n_heads = hd // HEAD_DIM
    page = kv_pages.shape[1]
    steps = kv_pages.shape[0] // (db * pages_per_step)
    qmat, bias_row = _sb_sample_operands(q, sb_bias)
    out = pl.pallas_call(
        functools.partial(_sb_sample_kernel, n_heads=n_heads),
        grid=(db, steps),
        in_specs=[
            pl.BlockSpec((1, hd, LANES), lambda b, p: (b, 0, 0)),
            pl.BlockSpec((1, LANES), lambda b, p: (0, 0)),
            pl.BlockSpec((pages_per_step, page, 2 * hd), lambda b, p: (b * steps + steps - 1 - p, 0, 0)),
        ],
        out_specs=pl.BlockSpec((1, 1, hd), lambda b, p: (b, 0, 0)),
        out_shape=jax.ShapeDtypeStruct((db, 1, hd), F32),
        scratch_shapes=[pltpu.VMEM((MOD_ROWS, hd), F32), pltpu.VMEM((1, LANES), F32)],
        compiler_params=_params(("arbitrary", "arbitrary"), 32),
        name="sb_sample",
    )(qmat, bias_row, kv_pages)
    return out.reshape(db, hd)


def kernel(x_prompt, x_sample, cache_win_g0, cache_win_g1, cache_win_g2, cache_kv, page_table, c_prompt, c_sample, w_mod, b_mod, norm_g, ffn_w_in, ffn_w_out, a_w_qkv, a_q_norm, a_k_norm, a_w_o, kv_norm, w_mod_kv, b_mod_kv, w_kv, sb_k_norm, b_w_q, b_q_norm, b_sb_bias, b_w_o):
    batch, seq, d = x_prompt.shape
    db, ds, _ = x_sample.shape
    depth = w_mod.shape[0]
    n_a = a_w_qkv.shape[0]
    d_ff = ffn_w_out.shape[2]
    n_b_heads = w_kv.shape[1] // (2 * HEAD_DIM)
    assert ds == 1 and db == 8 and db + batch <= MOD_ROWS
    caches = (cache_win_g0, cache_win_g1, cache_win_g2)

    c_all = jnp.zeros((MOD_ROWS, d), F32).at[:db].set(c_sample).at[db:db + batch].set(c_prompt)
    mod = _modulation(c_all, w_mod, b_mod)
    mod_kv = _modulation(c_all, w_mod_kv[None], b_mod_kv[None])
    mod_s = mod[:, :, :db].reshape(depth * N_MOD, db, d)
    mod_p = mod[:, :, db:db + batch].reshape(depth * N_MOD * batch, 1, d)
    modkv_s = mod_kv[:, :, :db].reshape(2, db, d)
    modkv_p = mod_kv[:, :, db:db + batch].reshape(2 * batch, 1, d)

    tm_p = _pick_tile(seq, 1024)
    rows_p = _Rows(batch * seq, batch, 1, tm_p)
    rows_s = _Rows(db, 1, db, db)
    paths = (
        dict(rows=rows_p, mod=mod_p, modkv=modkv_p, tf=_pick_tile(d_ff, 256)),
        dict(rows=rows_s, mod=mod_s, modkv=modkv_s, tf=_pick_tile(d_ff, 512)),
    )

    norm3 = norm_g.reshape(depth * 3, 1, d)
    kvn3 = kv_norm.reshape(1, 1, d)
    w_in = ffn_w_in.reshape(depth * 2, d, 2 * d_ff)
    w_out = ffn_w_out.reshape(depth * 2, d_ff, d)
    w_kv3 = w_kv[None]

    a_gain = [jnp.concatenate([jnp.tile(a_q_norm[l], A_HEADS), jnp.tile(a_k_norm[l], A_HEADS),
                               jnp.ones((A_HEADS * HEAD_DIM,), F32)])[None] for l in range(n_a)]
    kv_gain = jnp.concatenate([jnp.tile(sb_k_norm, n_b_heads), jnp.ones((n_b_heads * HEAD_DIM,), F32)])[None]
    b_gain = [jnp.tile(b_q_norm[j], n_b_heads)[None] for j in range(depth - n_a)]

    h_p = x_prompt.reshape(batch * seq, d)
    h_s = x_sample.reshape(db, d)
    tn_qkv = GROUP_COLS
    tn_d = _pick_tile(d, 512)
    tn_w = _pick_tile(d, 1024)
    win_p = [[] for _ in range(N_GROUPS)]
    win_s = [[] for _ in range(N_GROUPS)]
    kv_p = kv_s = kv_p_bf16 = None
    n_pages = page_table.shape[1]
    pages_per_step = 4 if n_pages % 4 == 0 else 1
    kv_pages = None
    slopes = jnp.asarray(_alibi_slopes(), F32)

    def ffn(h, path, l, which):
        return _ffn(h, path["rows"], path["mod"], (l * N_MOD + 6 * which), norm3, l * 3 + 2 * which,
                    w_in, w_out, l * 2 + which, path["tf"])

    for l in range(depth):
        h_p = ffn(h_p, paths[0], l, 0)
        h_s = ffn(h_s, paths[1], l, 0)
        if l < n_a:
            qkv_p = _proj(h_p, rows_p, mod_p, l * N_MOD + 3, norm3, l * 3 + 1, a_w_qkv, l, a_gain[l],
                          2 * A_HEADS * HEAD_DIM, tn_qkv)
            qkv_s = _proj(h_s, rows_s, mod_s, l * N_MOD + 3, norm3, l * 3 + 1, a_w_qkv, l, a_gain[l],
                          2 * A_HEADS * HEAD_DIM, tn_qkv)
            outs, lses = [], []
            for g in range(N_GROUPS):
                o_g, lse_g = _dilated_prompt(qkv_p, slopes, batch, seq, g, GROUP_HEADS if DIL_RATES[g] == 1 else 1)
                outs.append(o_g)
                lses.append(lse_g)
            h_p = _mix_out(outs, lses, a_w_o, l, h_p, rows_p, mod_p, l * N_MOD + 5, tn_d)
            mixed_s = _dilated_sample(qkv_s, caches, l)
            h_s = _out_proj(mixed_s, a_w_o, l, h_s, rows_s, mod_s, l * N_MOD + 5, tn_w)

            qkv_p3 = qkv_p.reshape(batch, seq, QKV_COLS)
            qkv_s5 = qkv_s.reshape(db, 1, 3, A_HEADS, HEAD_DIM)
            for g in range(N_GROUPS):
                hs = slice(g * GROUP_HEADS, (g + 1) * GROUP_HEADS)
                keep = min(DIL_WINDOWS[g], seq)
                kv_rows = [qkv_p3[:, seq - keep:, part * A_HEADS * HEAD_DIM + g * GROUP_COLS:
                                  part * A_HEADS * HEAD_DIM + (g + 1) * GROUP_COLS] for part in (1, 2)]
                win_p[g].append(jnp.stack(kv_rows, axis=2).reshape(batch, keep, 2, GROUP_HEADS, HEAD_DIM))
                buf = caches[g][l]
                win_s[g].append(jnp.concatenate([buf[:, 1:], qkv_s5[:, :, 1:3, hs]], axis=1))
        else:
            j = l - n_a
            q_p = _proj(h_p, rows_p, mod_p, l * N_MOD + 3, norm3, l * 3 + 1, b_w_q, j, b_gain[j],
                        n_b_heads * HEAD_DIM, tn_w)
            q_s = _proj(h_s, rows_s, mod_s, l * N_MOD + 3, norm3, l * 3 + 1, b_w_q, j, b_gain[j],
                        n_b_heads * HEAD_DIM, tn_w)
            o_p = _sb_prompt(q_p, kv_p_bf16, b_sb_bias[j], batch, seq, _pick_tile(seq, 512), _pick_tile(seq, 512))
            h_p = _out_proj(o_p, b_w_o, j, h_p, rows_p, mod_p, l * N_MOD + 5, tn_w)
            if kv_pages is None:
                o_s, kv_pages = _sb_sample_gather(q_s, cache_kv, page_table, b_sb_bias[j], pages_per_step)
            else:
                o_s = _sb_sample(q_s, kv_pages, b_sb_bias[j], pages_per_step)
            h_s = _out_proj(o_s, b_w_o, j, h_s, rows_s, mod_s, l * N_MOD + 5, tn_w)
        h_p = ffn(h_p, paths[0], l, 1)
        h_s = ffn(h_s, paths[1], l, 1)
        if l == n_a - 1:
            kv_p, kv_p_bf16 = _proj(h_p, rows_p, modkv_p, 0, kvn3, 0, w_kv3, 0, kv_gain,
                                    n_b_heads * HEAD_DIM, tn_w, with_bf16=True)
            kv_s = _proj(h_s, rows_s, modkv_s, 0, kvn3, 0, w_kv3, 0, kv_gain, n_b_heads * HEAD_DIM, tn_w)

    y_p = h_p.reshape(batch, seq, d)
    y_s = h_s.reshape(db, 1, d)
    win_p = [jnp.stack(w, axis=0) for w in win_p]
    win_s = [jnp.stack(w, axis=0) for w in win_s]
    kv_p = kv_p.reshape(batch, seq, 2, n_b_heads, HEAD_DIM)
    kv_s = kv_s.reshape(db, 1, 2, n_b_heads, HEAD_DIM)
    return (y_p, y_s, win_p[0], win_p[1], win_p[2], kv_p, win_s[0], win_s[1], win_s[2], kv_s)
```

```python
import functools
import math

import numpy as np
import jax
import jax.numpy as jnp
from jax import lax
from jax.experimental import pallas as pl
from jax.experimental.pallas import tpu as pltpu

F32 = jnp.float32
BF16 = jnp.bfloat16

EPS = 1e-6
HEAD_DIM = 128
LANES = 128
N_MOD = 9
DIL_WINDOWS = (128, 512, 2048)
DIL_RATES = (1, 4, 16)
N_GROUPS = 3
GROUP_HEADS = 5
A_HEADS = N_GROUPS * GROUP_HEADS
A_BLOCK = 128
GROUP_COLS = GROUP_HEADS * HEAD_DIM
QKV_COLS = 3 * A_HEADS * HEAD_DIM
NEG_BIG = -1e30
MIB = 1024 * 1024
MOD_ROWS = 16


def _alibi_slopes():
    return [float(2.0 ** (-8.0 * (i + 1) / A_HEADS)) for i in range(A_HEADS)]


def _params(semantics, vmem_mib):
    return pltpu.CompilerParams(dimension_semantics=semantics, vmem_limit_bytes=vmem_mib * MIB)


def _pick_tile(n, preferred):
    if n <= preferred:
        return n
    t = (preferred // LANES) * LANES
    while t >= LANES:
        if n % t == 0:
            return t
        t -= LANES
    return n


class _Rows:
    def __init__(self, n_rows, groups, cond_rows, tm):
        self.n_rows = n_rows
        self.groups = groups
        self.cond_rows = cond_rows
        self.tm = tm
        self.tiles_per_group = (n_rows // groups) // tm
        assert self.tiles_per_group * tm * groups == n_rows

    def cond_spec(self, chunk, width, col_of=None):
        g, tpg = self.groups, self.tiles_per_group
        if col_of is None:
            return pl.BlockSpec((1, self.cond_rows, width), lambda i, j: (chunk * g + i // tpg, 0, 0))
        return pl.BlockSpec((1, self.cond_rows, width), lambda i, j: (chunk * g + i // tpg, 0, col_of(j)))


def _norm_mod(x, gain, shift, scale):
    ms = jnp.mean(x * x, axis=-1, keepdims=True)
    y = x * lax.rsqrt(ms + EPS) * gain
    return y * (1.0 + scale) + shift


def _mod_kernel(c_ref, w_ref, b_ref, o_ref):
    c = c_ref[...]
    a = (c * jax.nn.sigmoid(c)).astype(BF16)
    o_ref[0, 0] = jnp.dot(a, w_ref[0].astype(BF16), preferred_element_type=F32) + b_ref[0]


def _modulation(c_all, w, b):
    n_layers, d, n_out = w.shape
    n_chunks = n_out // d
    tn = _pick_tile(d, 1024)
    per_chunk = d // tn
    return pl.pallas_call(
        _mod_kernel,
        grid=(n_layers, n_out // tn),
        in_specs=[
            pl.BlockSpec((MOD_ROWS, d), lambda l, j: (0, 0)),
            pl.BlockSpec((1, d, tn), lambda l, j: (l, 0, j)),
            pl.BlockSpec((1, 1, tn), lambda l, j: (l, 0, j)),
        ],
        out_specs=pl.BlockSpec((1, 1, MOD_ROWS, tn), lambda l, j: (l, j // per_chunk, 0, j % per_chunk)),
        out_shape=jax.ShapeDtypeStruct((n_layers, n_chunks, MOD_ROWS, d), F32),
        compiler_params=_params(("arbitrary", "arbitrary"), 40),
        name="modulation",
    )(c_all, w, b.reshape(n_layers, 1, n_out))


def _ffn_kernel(x_ref, sh_ref, sc_ref, gt_ref, ng_ref, wg_ref, wu_ref, wo_ref, o_ref, xn_ref):
    f = pl.program_id(1)

    @pl.when(f == 0)
    def _():
        u = _norm_mod(x_ref[...], ng_ref[0], sh_ref[0], sc_ref[0])
        xn_ref[...] = u.astype(BF16)
        o_ref[...] = jnp.zeros_like(o_ref)

    xn = xn_ref[...]
    gate = jnp.dot(xn, wg_ref[...].astype(BF16), preferred_element_type=F32)
    up = jnp.dot(xn, wu_ref[...].astype(BF16), preferred_element_type=F32)
    act = (gate * jax.nn.sigmoid(gate) * up).astype(BF16)
    o_ref[...] += jnp.dot(act, wo_ref[...].astype(BF16), preferred_element_type=F32)

    @pl.when(f == pl.num_programs(1) - 1)
    def _():
        o_ref[...] = x_ref[...] + 0.5 * gt_ref[0] * o_ref[...]


def _ffn(h, rows, mod, chunk0, norm_g, norm_idx, w_in, w_out, w_idx, tf):
    m, d = h.shape
    d_ff = w_out.shape[1]
    n_f = d_ff // tf
    tm = rows.tm
    return pl.pallas_call(
        _ffn_kernel,
        grid=(m // tm, n_f),
        in_specs=[
            pl.BlockSpec((tm, d), lambda i, f: (i, 0)),
            rows.cond_spec(chunk0, d),
            rows.cond_spec(chunk0 + 1, d),
            rows.cond_spec(chunk0 + 2, d),
            pl.BlockSpec((1, 1, d), lambda i, f: (norm_idx, 0, 0)),
            pl.BlockSpec((None, d, tf), lambda i, f: (w_idx, 0, f)),
            pl.BlockSpec((None, d, tf), lambda i, f: (w_idx, 0, n_f + f)),
            pl.BlockSpec((None, tf, d), lambda i, f: (w_idx, f, 0)),
        ],
        out_specs=pl.BlockSpec((tm, d), lambda i, f: (i, 0)),
        out_shape=jax.ShapeDtypeStruct((m, d), F32),
        scratch_shapes=[pltpu.VMEM((tm, d), BF16)],
        compiler_params=_params(("arbitrary", "arbitrary"), 58),
        name="ffn",
    )(h, mod, mod, mod, norm_g, w_in, w_in, w_out)


def _ffn_up_kernel(x_ref, sh_ref, sc_ref, ng_ref, wg_ref, wu_ref, a_ref, xn_ref):
    @pl.when(pl.program_id(1) == 0)
    def _():
        xn_ref[...] = _norm_mod(x_ref[...], ng_ref[0], sh_ref[0], sc_ref[0]).astype(BF16)

    xn = xn_ref[...]
    gate = jnp.dot(xn, wg_ref[...].astype(BF16), preferred_element_type=F32)
    up = jnp.dot(xn, wu_ref[...].astype(BF16), preferred_element_type=F32)
    a_ref[...] = (gate * jax.nn.sigmoid(gate) * up).astype(BF16)


def _ffn_up(h, rows, mod, chunk0, norm_g, norm_idx, w_in, w_idx, tf):
    m, d = h.shape
    d_ff = w_in.shape[2] // 2
    n_f = d_ff // tf
    tm = rows.tm
    return pl.pallas_call(
        _ffn_up_kernel,
        grid=(m // tm, n_f),
        in_specs=[
            pl.BlockSpec((tm, d), lambda i, f: (i, 0)),
            rows.cond_spec(chunk0, d),
            rows.cond_spec(chunk0 + 1, d),
            pl.BlockSpec((1, 1, d), lambda i, f: (norm_idx, 0, 0)),
            pl.BlockSpec((None, d, tf), lambda i, f: (w_idx, 0, f)),
            pl.BlockSpec((None, d, tf), lambda i, f: (w_idx, 0, n_f + f)),
        ],
        out_specs=pl.BlockSpec((tm, tf), lambda i, f: (i, f)),
        out_shape=jax.ShapeDtypeStruct((m, d_ff), BF16),
        scratch_shapes=[pltpu.VMEM((tm, d), BF16)],
        compiler_params=_params(("arbitrary", "arbitrary"), 56),
        name="ffn_up",
    )(h, mod, mod, norm_g, w_in, w_in)


def _proj_kernel(x_ref, sh_ref, sc_ref, ng_ref, w_ref, hg_ref, *rest, n_norm_tiles, with_bf16):
    if with_bf16:
        o_ref, obf_ref, xn_ref = rest
    else:
        o_ref, xn_ref = rest
        obf_ref = None
    j = pl.program_id(1)

    @pl.when(j == 0)
    def _():
        xn_ref[...] = _norm_mod(x_ref[...], ng_ref[0], sh_ref[0], sc_ref[0]).astype(BF16)

    acc = jnp.dot(xn_ref[...], w_ref[...].astype(BF16), preferred_element_type=F32)
    tn = acc.shape[1]

    def store(val, sl):
        o_ref[:, sl] = val
        if obf_ref is not None:
            obf_ref[:, sl] = val.astype(BF16)

    @pl.when(j < n_norm_tiles)
    def _():
        for t in range(tn // HEAD_DIM):
            sl = slice(t * HEAD_DIM, (t + 1) * HEAD_DIM)
            a = acc[:, sl]
            ms = jnp.mean(a * a, axis=-1, keepdims=True)
            store(a * lax.rsqrt(ms + EPS) * hg_ref[:, sl], sl)

    @pl.when(j >= n_norm_tiles)
    def _():
        store(acc, slice(None))


def _proj(h, rows, mod, chunk0, norm_g, norm_idx, w, w_idx, head_gain, n_norm_cols, tn, with_bf16=False):
    m, d = h.shape
    n = w.shape[-1]
    tm = rows.tm
    assert n % tn == 0 and n_norm_cols % tn == 0 and tn % HEAD_DIM == 0
    out_shape = [jax.ShapeDtypeStruct((m, n), F32)]
    out_specs = [pl.BlockSpec((tm, tn), lambda i, j: (i, j))]
    if with_bf16:
        out_shape.append(jax.ShapeDtypeStruct((m, n), BF16))
        out_specs.append(pl.BlockSpec((tm, tn), lambda i, j: (i, j)))
    res = pl.pallas_call(
        functools.partial(_proj_kernel, n_norm_tiles=n_norm_cols // tn, with_bf16=with_bf16),
        grid=(m // tm, n // tn),
        in_specs=[
            pl.BlockSpec((tm, d), lambda i, j: (i, 0)),
            rows.cond_spec(chunk0, d),
            rows.cond_spec(chunk0 + 1, d),
            pl.BlockSpec((1, 1, d), lambda i, j: (norm_idx, 0, 0)),
            pl.BlockSpec((None, d, tn), lambda i, j: (w_idx, 0, j)),
            pl.BlockSpec((1, tn), lambda i, j: (0, j)),
        ],
        out_specs=out_specs,
        out_shape=out_shape,
        scratch_shapes=[pltpu.VMEM((tm, d), BF16)],
        compiler_params=_params(("arbitrary", "arbitrary"), 62),
        name="proj",
    )(h, mod, mod, norm_g, w, head_gain)
    return res if with_bf16 else res[0]


def _out_kernel(x_ref, w_ref, h_ref, gt_ref, o_ref, *, scale):
    y = jnp.dot(x_ref[...].astype(BF16), w_ref[...].astype(BF16), preferred_element_type=F32)
    gate = gt_ref[0] if scale == 1.0 else scale * gt_ref[0]
    o_ref[...] = h_ref[...] + gate * y


def _out_proj(x, w, w_idx, h, rows, mod, chunk, tn, scale=1.0, vmem_mib=48):
    m, k = x.shape
    n = w.shape[-1]
    tm = rows.tm
    return pl.pallas_call(
        functools.partial(_out_kernel, scale=scale),
        grid=(m // tm, n // tn),
        in_specs=[
            pl.BlockSpec((tm, k), lambda i, j: (i, 0)),
            pl.BlockSpec((None, k, tn), lambda i, j: (w_idx, 0, j)),
            pl.BlockSpec((tm, tn), lambda i, j: (i, j)),
            rows.cond_spec(chunk, tn, col_of=lambda j: j),
        ],
        out_specs=pl.BlockSpec((tm, tn), lambda i, j: (i, j)),
        out_shape=jax.ShapeDtypeStruct((m, n), F32),
        compiler_params=_params(("arbitrary", "arbitrary"), vmem_mib),
        name="out_proj",
    )(x, w, h, mod)


def _dil_kernel(slopes_ref, q_ref, kc_ref, kp_ref, vc_ref, vp_ref, o_ref, lse_ref, *, dil, heads, head0):
    n = pl.program_id(1)
    hb = pl.program_id(2)
    scale = 1.0 / math.sqrt(HEAD_DIM)
    qi = lax.broadcasted_iota(jnp.int32, (A_BLOCK, A_BLOCK), 0)
    ki = lax.broadcasted_iota(jnp.int32, (A_BLOCK, A_BLOCK), 1)
    step_c = qi - ki
    step_p = step_c + A_BLOCK
    valid_c = step_c >= 0
    valid_p = jnp.logical_and(step_p <= A_BLOCK, n > 0)
    dist_c = (step_c * dil).astype(F32)
    dist_p = (step_p * dil).astype(F32)
    nt = (((1,), (1,)), ((), ()))
    for hh in range(heads):
        slope = slopes_ref[head0 + hb * heads + hh]
        bias_c = jnp.where(valid_c, -slope * dist_c, NEG_BIG)
        bias_p = jnp.where(valid_p, -slope * dist_p, NEG_BIG)
        sl = slice(hh * HEAD_DIM, (hh + 1) * HEAD_DIM)
        for res in range(dil):
            rows = pl.ds(res, A_BLOCK, stride=dil) if dil > 1 else slice(None)
            q = (q_ref[rows, sl] * scale).astype(BF16)
            s_c = lax.dot_general(q, kc_ref[rows, sl].astype(BF16), nt, preferred_element_type=F32) + bias_c
            s_p = lax.dot_general(q, kp_ref[rows, sl].astype(BF16), nt, preferred_element_type=F32) + bias_p
            mx = jnp.maximum(jnp.max(s_c, axis=-1, keepdims=True), jnp.max(s_p, axis=-1, keepdims=True))
            p_c = jnp.exp(s_c - mx)
            p_p = jnp.exp(s_p - mx)
            den = jnp.sum(p_c, axis=-1, keepdims=True) + jnp.sum(p_p, axis=-1, keepdims=True)
            o = jnp.dot(p_c.astype(BF16), vc_ref[rows, sl].astype(BF16), preferred_element_type=F32)
            o = o + jnp.dot(p_p.astype(BF16), vp_ref[rows, sl].astype(BF16), preferred_element_type=F32)
            o_ref[rows, sl] = o / den
            lse_ref[rows, sl] = jnp.broadcast_to(mx + jnp.log(den), (A_BLOCK, HEAD_DIM))


def _dilated_prompt(qkv, slopes, batch, seq, g, heads):
    dil = DIL_RATES[g]
    span = dil * A_BLOCK
    assert DIL_WINDOWS[g] // dil == A_BLOCK and seq % span == 0 and GROUP_HEADS % heads == 0
    nb = seq // span
    cw = heads * HEAD_DIM
    q_col = g * GROUP_COLS // cw
    k_col = (A_HEADS * HEAD_DIM + g * GROUP_COLS) // cw
    v_col = (2 * A_HEADS * HEAD_DIM + g * GROUP_COLS) // cw
    blk = (span, cw)

    def cur(col):
        return pl.BlockSpec(blk, lambda b, n, hb: (b * nb + n, col + hb))

    def prev(col):
        return pl.BlockSpec(blk, lambda b, n, hb: (b * nb + jnp.maximum(n - 1, 0), col + hb))

    out_spec = pl.BlockSpec(blk, lambda b, n, hb: (b * nb + n, hb))
    return pl.pallas_call(
        functools.partial(_dil_kernel, dil=dil, heads=heads, head0=g * GROUP_HEADS),
        grid=(batch, nb, GROUP_HEADS // heads),
        in_specs=[pl.BlockSpec(memory_space=pltpu.SMEM), cur(q_col), cur(k_col), prev(k_col), cur(v_col), prev(v_col)],
        out_specs=[out_spec, out_spec],
        out_shape=[jax.ShapeDtypeStruct((batch * seq, GROUP_COLS), F32)] * 2,
        compiler_params=_params(("arbitrary", "arbitrary", "arbitrary"), 40),
        name="dilated_prompt",
    )(slopes, qkv, qkv, qkv, qkv, qkv)


def _mix_out_kernel(o0_ref, o1_ref, o2_ref, l0_ref, l1_ref, l2_ref, w_ref, h_ref, gt_ref, out_ref, mix_ref):
    j = pl.program_id(1)

    @pl.when(j == 0)
    def _():
        ls = []
        for l_ref in (l0_ref, l1_ref, l2_ref):
            heads = [l_ref[:, hh * HEAD_DIM:(hh + 1) * HEAD_DIM] for hh in range(GROUP_HEADS)]
            htop = functools.reduce(jnp.maximum, heads)
            htot = functools.reduce(lambda a, b: a + b, [jnp.exp(l - htop) for l in heads])
            ls.append(htop + jnp.log(htot) - math.log(GROUP_HEADS))
        top = jnp.maximum(jnp.maximum(ls[0], ls[1]), ls[2])
        es = [jnp.exp(l - top) for l in ls]
        den = es[0] + es[1] + es[2]
        for g, o_ref in enumerate((o0_ref, o1_ref, o2_ref)):
            alpha = es[g] / den
            for hh in range(GROUP_HEADS):
                src = slice(hh * HEAD_DIM, (hh + 1) * HEAD_DIM)
                dst = slice(g * GROUP_COLS + hh * HEAD_DIM, g * GROUP_COLS + (hh + 1) * HEAD_DIM)
                mix_ref[:, dst] = (o_ref[:, src] * alpha).astype(BF16)

    y = jnp.dot(mix_ref[...], w_ref[...].astype(BF16), preferred_element_type=F32)
    out_ref[...] = h_ref[...] + gt_ref[0] * y


def _mix_out(outs, lses, w, w_idx, h, rows, mod, chunk, tn):
    m = h.shape[0]
    n = w.shape[-1]
    k = w.shape[-2]
    tm = rows.tm
    o_spec = pl.BlockSpec((tm, GROUP_COLS), lambda i, j: (i, 0))
    return pl.pallas_call(
        _mix_out_kernel,
        grid=(m // tm, n // tn),
        in_specs=[
            o_spec, o_spec, o_spec, o_spec, o_spec, o_spec,
            pl.BlockSpec((None, k, tn), lambda i, j: (w_idx, 0, j)),
            pl.BlockSpec((tm, tn), lambda i, j: (i, j)),
            rows.cond_spec(chunk, tn, col_of=lambda j: j),
        ],
        out_specs=pl.BlockSpec((tm, tn), lambda i, j: (i, j)),
        out_shape=jax.ShapeDtypeStruct((m, n), F32),
        scratch_shapes=[pltpu.VMEM((tm, k), BF16)],
        compiler_params=_params(("arbitrary", "arbitrary"), 58),
        name="mix_out",
    )(*outs, *lses, w, h, mod)


def _dil_sample_kernel(qkv_ref, c0_ref, c1_ref, c2_ref, o_ref, *, slopes):
    scale = 1.0 / math.sqrt(HEAD_DIM)
    nt = (((1,), (1,)), ((), ()))
    kbase = A_HEADS * HEAD_DIM
    vbase = 2 * A_HEADS * HEAD_DIM
    steps = (A_BLOCK - lax.broadcasted_iota(jnp.int32, (1, A_BLOCK), 1)).astype(F32)
    outs, glses = [], []
    for g, c_ref in enumerate((c0_ref, c1_ref, c2_ref)):
        dil = DIL_RATES[g]
        head_outs, head_lses = [], []
        for hh in range(GROUP_HEADS):
            col = (g * GROUP_HEADS + hh) * HEAD_DIM
            q = qkv_ref[0, :, col:col + HEAD_DIM]
            k_new = qkv_ref[0, :, kbase + col:kbase + col + HEAD_DIM]
            v_new = qkv_ref[0, :, vbase + col:vbase + col + HEAD_DIM]
            k_buf = c_ref[0, 0, :, hh * HEAD_DIM:(hh + 1) * HEAD_DIM]
            v_buf = c_ref[0, 0, :, GROUP_COLS + hh * HEAD_DIM:GROUP_COLS + (hh + 1) * HEAD_DIM]
            q8 = jnp.broadcast_to(q * scale, (8, HEAD_DIM)).astype(BF16)
            s_buf = lax.dot_general(q8, k_buf.astype(BF16), nt, preferred_element_type=F32)[0:1]
            s_buf = s_buf - slopes[g * GROUP_HEADS + hh] * dil * steps
            s_new = jnp.sum((q * scale).astype(BF16).astype(F32) * k_new.astype(BF16).astype(F32),
                            axis=-1, keepdims=True)
            mx = jnp.maximum(jnp.max(s_buf, axis=-1, keepdims=True), s_new)
            p_buf = jnp.exp(s_buf - mx)
            p_new = jnp.exp(s_new - mx)
            den = jnp.sum(p_buf, axis=-1, keepdims=True) + p_new
            p8 = jnp.broadcast_to(p_buf, (8, A_BLOCK)).astype(BF16)
            o = jnp.dot(p8, v_buf.astype(BF16), preferred_element_type=F32)[0:1]
            o = (o + p_new.astype(BF16).astype(F32) * v_new.astype(BF16).astype(F32)) / den
            head_outs.append(o)
            head_lses.append(mx + jnp.log(den))
        top = functools.reduce(jnp.maximum, head_lses)
        tot = functools.reduce(lambda a, b: a + b, [jnp.exp(l - top) for l in head_lses])
        glses.append(top + jnp.log(tot) - math.log(GROUP_HEADS))
        outs.append(head_outs)
    top = functools.reduce(jnp.maximum, glses)
    es = [jnp.exp(l - top) for l in glses]
    den = es[0] + es[1] + es[2]
    for g in range(N_GROUPS):
        alpha = es[g] / den
        for hh in range(GROUP_HEADS):
            col = (g * GROUP_HEADS + hh) * HEAD_DIM
            o_ref[0, :, col:col + HEAD_DIM] = outs[g][hh] * alpha


def _dilated_sample(qkv_s, caches, layer):
    db = qkv_s.shape[0]
    row_cols = 2 * GROUP_COLS
    views, specs = [], []
    for g, c in enumerate(caches):
        dil = DIL_RATES[g]
        assert c.shape[2] == DIL_WINDOWS[g] and c.shape[2] // dil == A_BLOCK
        views.append(c[:, :, ::dil].reshape(c.shape[0], db, A_BLOCK, row_cols))
        specs.append(pl.BlockSpec((1, 1, A_BLOCK, row_cols), lambda b: (layer, b, 0, 0)))
    out = pl.pallas_call(
        functools.partial(_dil_sample_kernel, slopes=tuple(_alibi_slopes())),
        grid=(db,),
        in_specs=[pl.BlockSpec((1, 1, QKV_COLS), lambda b: (b, 0, 0))] + specs,
        out_specs=pl.BlockSpec((1, 1, A_HEADS * HEAD_DIM), lambda b: (b, 0, 0)),
        out_shape=jax.ShapeDtypeStruct((db, 1, A_HEADS * HEAD_DIM), F32),
        compiler_params=_params(("arbitrary",), 32),
        name="dilated_sample",
    )(qkv_s.reshape(db, 1, QKV_COLS), *views)
    return out.reshape(db, A_HEADS * HEAD_DIM)


def _stick_terms(z):
    lo = jnp.minimum(z, 0.0)
    hi = jnp.maximum(z, 0.0)
    l = jnp.log(1.0 + jnp.exp(lo - hi))
    return lo - l, hi + l


def _sb_prompt_kernel(bias_ref, q_ref, k_ref, v_ref, o_ref, *, tq, tk, sub):
    h = pl.program_id(1)
    i = pl.program_id(2)
    n_sub = tk // sub
    bias = bias_ref[h]
    q = (q_ref[0] * (1.0 / math.sqrt(HEAD_DIM))).astype(BF16)
    nt = (((1,), (1,)), ((), ()))
    r = lax.broadcasted_iota(jnp.int32, (sub, sub), 0)
    c = lax.broadcasted_iota(jnp.int32, (sub, sub), 1)
    later_mat = jnp.where(r > c, 1.0, 0.0).astype(BF16)
    q_pos = lax.broadcasted_iota(jnp.int32, (tq, sub), 0)
    if tq == tk:
        n_full = i
    else:
        n_full = (i * tq) // tk
        q_pos = q_pos + (i * tq - n_full * tk)
    k_off = lax.broadcasted_iota(jnp.int32, (tq, sub), 1)

    def block(start, carry, acc, masked):
        k = k_ref[0, pl.ds(start, tk), :]
        v = v_ref[0, pl.ds(start, tk), :]
        z_all = lax.dot_general(q, k, nt, preferred_element_type=F32) + bias
        ws = [None] * n_sub
        for cidx in reversed(range(n_sub)):
            log_take, cost = _stick_terms(z_all[:, cidx * sub:(cidx + 1) * sub])
            if masked:
                causal = (k_off + cidx * sub) < q_pos
                cost = jnp.where(causal, cost, 0.0)
            later = jnp.dot(cost.astype(BF16), later_mat, preferred_element_type=F32)
            w = jnp.exp(log_take - later - carry)
            if masked:
                w = jnp.where(causal, w, 0.0)
            ws[cidx] = w.astype(BF16)
            carry = carry + jnp.sum(cost, axis=-1, keepdims=True)
        acc = acc + jnp.dot(jnp.concatenate(ws, axis=1), v, preferred_element_type=F32)
        return carry, acc

    zeros = (jnp.zeros((tq, 1), F32), jnp.zeros((tq, HEAD_DIM), F32))
    carry, acc = block(pl.multiple_of(n_full * tk, tk), *zeros, True)

    def body(t, state):
        start = pl.multiple_of((n_full - 1 - t) * tk, tk)
        return block(start, state[0], state[1], False)

    carry, acc = lax.fori_loop(0, n_full, body, (carry, acc))
    o_ref[0] = acc.astype(o_ref.dtype)


def _sb_prompt(q, kv_bf16, sb_bias, batch, seq, tq, tk):
    n_heads = q.shape[1] // HEAD_DIM
    assert tk % tq == 0 and seq % tk == 0
    q3 = q.reshape(batch, seq, n_heads * HEAD_DIM)
    kv3 = kv_bf16.reshape(batch, seq, 2 * n_heads * HEAD_DIM)
    out = pl.pallas_call(
        functools.partial(_sb_prompt_kernel, tq=tq, tk=tk, sub=min(tk, 2 * LANES)),
        grid=(batch, n_heads, seq // tq),
        in_specs=[
            pl.BlockSpec(memory_space=pltpu.SMEM),
            pl.BlockSpec((1, tq, HEAD_DIM), lambda b, h, i: (b, i, h)),
            pl.BlockSpec((1, seq, HEAD_DIM), lambda b, h, i: (b, 0, h)),
            pl.BlockSpec((1, seq, HEAD_DIM), lambda b, h, i: (b, 0, n_heads + h)),
        ],
        out_specs=pl.BlockSpec((1, tq, HEAD_DIM), lambda b, h, i: (b, i, h)),
        out_shape=jax.ShapeDtypeStruct((batch, seq, n_heads * HEAD_DIM), BF16),
        compiler_params=_params(("arbitrary", "arbitrary", "arbitrary"), 40),
        name="sb_prompt",
    )(sb_bias, q3, kv3, kv3)
    return out.reshape(batch * seq, n_heads * HEAD_DIM)


def _relayout_page(x_ref, o_ref, t, n_heads):
    page = o_ref.shape[1]
    keys = 2 * LANES // n_heads
    n = keys * n_heads
    i = lax.broadcasted_iota(jnp.int32, (n, n), 0)
    j = lax.broadcasted_iota(jnp.int32, (n, n), 1)
    perm = jnp.where(jnp.logical_and(i // keys == j % n_heads, i % keys == j // n_heads), 1.0, 0.0).astype(BF16)
    for c in range(page // keys):
        k_rows = x_ref[c * keys:(c + 1) * keys, 0:n_heads, :].reshape(n, HEAD_DIM)
        v_rows = x_ref[c * keys:(c + 1) * keys, n_heads:2 * n_heads, :].reshape(n, HEAD_DIM)
        both = jnp.concatenate([k_rows, v_rows], axis=1).astype(BF16)
        y = jnp.dot(perm, both, preferred_element_type=F32)
        for hh in range(n_heads):
            blk = y[hh * keys:(hh + 1) * keys].astype(BF16)
            o_ref[t, c * keys:(c + 1) * keys, hh * HEAD_DIM:(hh + 1) * HEAD_DIM] = blk[:, :HEAD_DIM]
            o_ref[t, c * keys:(c + 1) * keys, (n_heads + hh) * HEAD_DIM:(n_heads + hh + 1) * HEAD_DIM] = (
                blk[:, HEAD_DIM:])


def _sb_sample_step(qmat_ref, bias_ref, kv_ref, o_ref, acc_ref, carry_ref, n_heads):
    p = pl.program_id(1)
    hd = n_heads * HEAD_DIM
    pages_per_step, page = kv_ref.shape[0], kv_ref.shape[1]

    @pl.when(p == 0)
    def _():
        acc_ref[...] = jnp.zeros_like(acc_ref)
        carry_ref[...] = jnp.zeros_like(carry_ref)

    r = lax.broadcasted_iota(jnp.int32, (page, page), 0)
    c = lax.broadcasted_iota(jnp.int32, (page, page), 1)
    later_mat = jnp.where(c > r, 1.0, 0.0).astype(BF16)
    k_all = kv_ref[:, :, :hd].reshape(pages_per_step * page, hd)
    half = hd // 2
    z = (jnp.dot(k_all[:, :half], qmat_ref[0, :half, :], preferred_element_type=F32)
         + jnp.dot(k_all[:, half:], qmat_ref[0, half:, :], preferred_element_type=F32) + bias_ref[...])
    log_take, cost = _stick_terms(z)
    carry = carry_ref[...]
    ws = [None] * pages_per_step
    for t in reversed(range(pages_per_step)):
        rows = slice(t * page, (t + 1) * page)
        later = jnp.dot(later_mat, cost[rows].astype(BF16), preferred_element_type=F32)
        w = jnp.exp(log_take[rows] - later - carry)
        ws[t] = w.T[:MOD_ROWS].astype(BF16)
        carry = carry + jnp.sum(cost[rows], axis=0, keepdims=True)
    carry_ref[...] = carry
    w_all = jnp.concatenate(ws, axis=1)
    v_all = kv_ref[:, :, hd:].reshape(pages_per_step * page, hd)
    acc_ref[...] += jnp.dot(w_all, v_all, preferred_element_type=F32)

    @pl.when(p == pl.num_programs(1) - 1)
    def _():
        for hh in range(n_heads):
            sl = slice(hh * HEAD_DIM, (hh + 1) * HEAD_DIM)
            o_ref[0, :, sl] = acc_ref[hh:hh + 1, sl]


def _sb_sample_kernel(qmat_ref, bias_ref, kv_ref, o_ref, acc_ref, carry_ref, *, n_heads):
    _sb_sample_step(qmat_ref, bias_ref, kv_ref, o_ref, acc_ref, carry_ref, n_heads)


def _sb_sample_gather_kernel(pt_ref, qmat_ref, bias_ref, *refs, n_heads, pages_per_step):
    x_refs = refs[:pages_per_step]
    o_ref, kv_ref, acc_ref, carry_ref = refs[pages_per_step:]
    for t, x_ref in enumerate(x_refs):
        _relayout_page(x_ref, kv_ref, t, n_heads)
    _sb_sample_step(qmat_ref, bias_ref, kv_ref, o_ref, acc_ref, carry_ref, n_heads)


def _sb_sample_operands(q, sb_bias):
    hd = q.shape[1]
    n_heads = hd // HEAD_DIM
    assert n_heads <= MOD_ROWS
    head_of_row = jnp.arange(hd, dtype=jnp.int32) // HEAD_DIM
    sel = (head_of_row[:, None] == jnp.arange(LANES, dtype=jnp.int32)[None, :]).astype(F32)
    qmat = ((q * (1.0 / math.sqrt(HEAD_DIM)))[:, :, None] * sel[None]).astype(BF16)
    bias_row = jnp.zeros((1, LANES), F32).at[0, :n_heads].set(sb_bias.astype(F32))
    return qmat, bias_row


def _sb_sample_gather(q, cache_kv, page_table, sb_bias, pages_per_step):
    db, hd = q.shape
    n_phys, page, _, n_heads, _ = cache_kv.shape
    n_pages = page_table.shape[1]
    slabs = 2 * n_heads
    assert (2 * LANES) % n_heads == 0 and page % (2 * LANES // n_heads) == 0 and n_heads % 8 == 0
    steps = n_pages // pages_per_step
    qmat, bias_row = _sb_sample_operands(q, sb_bias)
    rows = cache_kv.reshape(n_phys, page, slabs, HEAD_DIM)

    def page_spec(t):
        return pl.BlockSpec((None, page, slabs, HEAD_DIM),
                            lambda b, p, pt: (pt[b, (steps - 1 - p) * pages_per_step + t], 0, 0, 0))

    out, kv_pages = pl.pallas_call(
        functools.partial(_sb_sample_gather_kernel, n_heads=n_heads, pages_per_step=pages_per_step),
        grid_spec=pltpu.PrefetchScalarGridSpec(
            num_scalar_prefetch=1,
            grid=(db, steps),
            in_specs=[
                pl.BlockSpec((1, hd, LANES), lambda b, p, pt: (b, 0, 0)),
                pl.BlockSpec((1, LANES), lambda b, p, pt: (0, 0)),
            ] + [page_spec(t) for t in range(pages_per_step)],
            out_specs=[
                pl.BlockSpec((1, 1, hd), lambda b, p, pt: (b, 0, 0)),
                pl.BlockSpec((pages_per_step, page, 2 * hd), lambda b, p, pt: (b * steps + steps - 1 - p, 0, 0)),
            ],
            scratch_shapes=[pltpu.VMEM((MOD_ROWS, hd), F32), pltpu.VMEM((1, LANES), F32)],
        ),
        out_shape=[jax.ShapeDtypeStruct((db, 1, hd), F32),
                   jax.ShapeDtypeStruct((db * n_pages, page, 2 * hd), BF16)],
        compiler_params=_params(("arbitrary", "arbitrary"), 48),
        name="sb_sample_gather",
    )(page_table, qmat, bias_row, *([rows] * pages_per_step))
    return out.reshape(db, hd), kv_pages


def _sb_sample(q, kv_pages, sb_bias, pages_per_step):
    db, hd = q.shape
    n_heads = hd // HEAD_DIM
    page = kv_pages.shape[1]
    steps = kv_pages.shape[0] // (db * pages_per_step)
    qmat, bias_row = _sb_sample_operands(q, sb_bias)
    out = pl.pallas_call(
        functools.partial(_sb_sample_kernel, n_heads=n_heads),
        grid=(db, steps),
        in_specs=[
            pl.BlockSpec((1, hd, LANES), lambda b, p: (b, 0, 0)),
            pl.BlockSpec((1, LANES), lambda b, p: (0, 0)),
            pl.BlockSpec((pages_per_step, page, 2 * hd), lambda b, p: (b * steps + steps - 1 - p, 0, 0)),
        ],
        out_specs=pl.BlockSpec((1, 1, hd), lambda b, p: (b, 0, 0)),
        out_shape=jax.ShapeDtypeStruct((db, 1, hd), F32),
        scratch_shapes=[pltpu.VMEM((MOD_ROWS, hd), F32), pltpu.VMEM((1, LANES), F32)],
        compiler_params=_params(("arbitrary", "arbitrary"), 40),
        name="sb_sample",
    )(qmat, bias_row, kv_pages)
    return out.reshape(db, hd)


def kernel(x_prompt, x_sample, cache_win_g0, cache_win_g1, cache_win_g2, cache_kv, page_table, c_prompt, c_sample, w_mod, b_mod, norm_g, ffn_w_in, ffn_w_out, a_w_qkv, a_q_norm, a_k_norm, a_w_o, kv_norm, w_mod_kv, b_mod_kv, w_kv, sb_k_norm, b_w_q, b_q_norm, b_sb_bias, b_w_o):
    batch, seq, d = x_prompt.shape
    db, ds, _ = x_sample.shape
    depth = w_mod.shape[0]
    n_a = a_w_qkv.shape[0]
    d_ff = ffn_w_out.shape[2]
    n_b_heads = w_kv.shape[1] // (2 * HEAD_DIM)
    assert ds == 1 and db == 8 and db + batch <= MOD_ROWS
    caches = (cache_win_g0, cache_win_g1, cache_win_g2)

    c_all = jnp.zeros((MOD_ROWS, d), F32).at[:db].set(c_sample).at[db:db + batch].set(c_prompt)
    mod = _modulation(c_all, w_mod, b_mod)
    mod_kv = _modulation(c_all, w_mod_kv[None], b_mod_kv[None])
    mod_s = mod[:, :, :db].reshape(depth * N_MOD, db, d)
    mod_p = mod[:, :, db:db + batch].reshape(depth * N_MOD * batch, 1, d)
    modkv_s = mod_kv[:, :, :db].reshape(2, db, d)
    modkv_p = mod_kv[:, :, db:db + batch].reshape(2 * batch, 1, d)

    tm_p = _pick_tile(seq, 1024)
    rows_p = _Rows(batch * seq, batch, 1, tm_p)
    rows_s = _Rows(db, 1, db, db)
    paths = (
        dict(rows=rows_p, mod=mod_p, modkv=modkv_p, tf=_pick_tile(d_ff, 512), split=True),
        dict(rows=rows_s, mod=mod_s, modkv=modkv_s, tf=_pick_tile(d_ff, 512), split=False),
    )
    tn_ffn_down = _pick_tile(d, 256)

    norm3 = norm_g.reshape(depth * 3, 1, d)
    kvn3 = kv_norm.reshape(1, 1, d)
    w_in = ffn_w_in.reshape(depth * 2, d, 2 * d_ff)
    w_out = ffn_w_out.reshape(depth * 2, d_ff, d)
    w_kv3 = w_kv[None]

    a_gain = [jnp.concatenate([jnp.tile(a_q_norm[l], A_HEADS), jnp.tile(a_k_norm[l], A_HEADS),
                               jnp.ones((A_HEADS * HEAD_DIM,), F32)])[None] for l in range(n_a)]
    kv_gain = jnp.concatenate([jnp.tile(sb_k_norm, n_b_heads), jnp.ones((n_b_heads * HEAD_DIM,), F32)])[None]
    b_gain = [jnp.tile(b_q_norm[j], n_b_heads)[None] for j in range(depth - n_a)]

    h_p = x_prompt.reshape(batch * seq, d)
    h_s = x_sample.reshape(db, d)
    tn_qkv = GROUP_COLS
    tn_d = _pick_tile(d, 512)
    tn_w = _pick_tile(d, 1024)
    win_p = [[] for _ in range(N_GROUPS)]
    win_s = [[] for _ in range(N_GROUPS)]
    kv_p = kv_s = kv_p_bf16 = None
    n_pages = page_table.shape[1]
    pages_per_step = 4 if n_pages % 4 == 0 else 1
    kv_pages = None
    slopes = jnp.asarray(_alibi_slopes(), F32)

    def ffn(h, path, l, which):
        chunk0 = l * N_MOD + 6 * which
        if path["split"]:
            act = _ffn_up(h, path["rows"], path["mod"], chunk0, norm3, l * 3 + 2 * which, w_in, l * 2 + which,
                          path["tf"])
            return _out_proj(act, w_out, l * 2 + which, h, path["rows"], path["mod"], chunk0 + 2, tn_ffn_down,
                             scale=0.5, vmem_mib=56)
        return _ffn(h, path["rows"], path["mod"], chunk0, norm3, l * 3 + 2 * which,
                    w_in, w_out, l * 2 + which, path["tf"])

    for l in range(depth):
        h_p = ffn(h_p, paths[0], l, 0)
        h_s = ffn(h_s, paths[1], l, 0)
        if l < n_a:
            qkv_p = _proj(h_p, rows_p, mod_p, l * N_MOD + 3, norm3, l * 3 + 1, a_w_qkv, l, a_gain[l],
                          2 * A_HEADS * HEAD_DIM, tn_qkv)
            qkv_s = _proj(h_s, rows_s, mod_s, l * N_MOD + 3, norm3, l * 3 + 1, a_w_qkv, l, a_gain[l],
                          2 * A_HEADS * HEAD_DIM, tn_qkv)
            outs, lses = [], []
            for g in range(N_GROUPS):
                o_g, lse_g = _dilated_prompt(qkv_p, slopes, batch, seq, g, GROUP_HEADS if DIL_RATES[g] == 1 else 1)
                outs.append(o_g)
                lses.append(lse_g)
            h_p = _mix_out(outs, lses, a_w_o, l, h_p, rows_p, mod_p, l * N_MOD + 5, tn_d)
            mixed_s = _dilated_sample(qkv_s, caches, l)
            h_s = _out_proj(mixed_s, a_w_o, l, h_s, rows_s, mod_s, l * N_MOD + 5, tn_w)

            qkv_p3 = qkv_p.reshape(batch, seq, QKV_COLS)
            qkv_s5 = qkv_s.reshape(db, 1, 3, A_HEADS, HEAD_DIM)
            for g in range(N_GROUPS):
                hs = slice(g * GROUP_HEADS, (g + 1) * GROUP_HEADS)
                keep = min(DIL_WINDOWS[g], seq)
                kv_rows = [qkv_p3[:, seq - keep:, part * A_HEADS * HEAD_DIM + g * GROUP_COLS:
                                  part * A_HEADS * HEAD_DIM + (g + 1) * GROUP_COLS] for part in (1, 2)]
                win_p[g].append(jnp.stack(kv_rows, axis=2).reshape(batch, keep, 2, GROUP_HEADS, HEAD_DIM))
                buf = caches[g][l]
                win_s[g].append(jnp.concatenate([buf[:, 1:], qkv_s5[:, :, 1:3, hs]], axis=1))
        else:
            j = l - n_a
            q_p = _proj(h_p, rows_p, mod_p, l * N_MOD + 3, norm3, l * 3 + 1, b_w_q, j, b_gain[j],
                        n_b_heads * HEAD_DIM, tn_w)
            q_s = _proj(h_s, rows_s, mod_s, l * N_MOD + 3, norm3, l * 3 + 1, b_w_q, j, b_gain[j],
                        n_b_heads * HEAD_DIM, tn_w)
            o_p = _sb_prompt(q_p, kv_p_bf16, b_sb_bias[j], batch, seq, _pick_tile(seq, 512), _pick_tile(seq, 512))
            h_p = _out_proj(o_p, b_w_o, j, h_p, rows_p, mod_p, l * N_MOD + 5, tn_w)
            if kv_pages is None:
                o_s, kv_pages = _sb_sample_gather(q_s, cache_kv, page_table, b_sb_bias[j], pages_per_step)
            else:
                o_s = _sb_sample(q_s, kv_pages, b_sb_bias[j], 2 * pages_per_step if n_pages % 8 == 0 else pages_per_step)
            h_s = _out_proj(o_s, b_w_o, j, h_s, rows_s, mod_s, l * N_MOD + 5, tn_w)
        h_p = ffn(h_p, paths[0], l, 1)
        h_s = ffn(h_s, paths[1], l, 1)
        if l == n_a - 1:
            kv_p, kv_p_bf16 = _proj(h_p, rows_p, modkv_p, 0, kvn3, 0, w_kv3, 0, kv_gain,
                                    n_b_heads * HEAD_DIM, tn_w, with_bf16=True)
            kv_s = _proj(h_s, rows_s, modkv_s, 0, kvn3, 0, w_kv3, 0, kv_gain, n_b_heads * HEAD_DIM, tn_w)

    y_p = h_p.reshape(batch, seq, d)
    y_s = h_s.reshape(db, 1, d)
    win_p = [jnp.stack(w, axis=0) for w in win_p]
    win_s = [jnp.stack(w, axis=0) for w in win_s]
    kv_p = kv_p.reshape(batch, seq, 2, n_b_heads, HEAD_DIM)
    kv_s = kv_s.reshape(db, 1, 2, n_b_heads, HEAD_DIM)
    return (y_p, y_s, win_p[0], win_p[1], win_p[2], kv_p, win_s[0], win_s[1], win_s[2], kv_s)
```

```python
import functools
import math

import numpy as np
import jax
import jax.numpy as jnp
from jax import lax
from jax.experimental import pallas as pl
from jax.experimental.pallas import tpu as pltpu

F32 = jnp.float32
BF16 = jnp.bfloat16

EPS = 1e-6
HEAD_DIM = 128
LANES = 128
N_MOD = 9
DIL_WINDOWS = (128, 512, 2048)
DIL_RATES = (1, 4, 16)
N_GROUPS = 3
GROUP_HEADS = 5
A_HEADS = N_GROUPS * GROUP_HEADS
A_BLOCK = 128
GROUP_COLS = GROUP_HEADS * HEAD_DIM
QKV_COLS = 3 * A_HEADS * HEAD_DIM
NEG_BIG = -1e30
MIB = 1024 * 1024
MOD_ROWS = 16


def _alibi_slopes():
    return [float(2.0 ** (-8.0 * (i + 1) / A_HEADS)) for i in range(A_HEADS)]


def _params(semantics, vmem_mib):
    return pltpu.CompilerParams(dimension_semantics=semantics, vmem_limit_bytes=vmem_mib * MIB)


def _pick_tile(n, preferred):
    if n <= preferred:
        return n
    t = (preferred // LANES) * LANES
    while t >= LANES:
        if n % t == 0:
            return t
        t -= LANES
    return n


class _Rows:
    def __init__(self, n_rows, groups, cond_rows, tm):
        self.n_rows = n_rows
        self.groups = groups
        self.cond_rows = cond_rows
        self.tm = tm
        self.tiles_per_group = (n_rows // groups) // tm
        assert self.tiles_per_group * tm * groups == n_rows

    def cond_spec(self, chunk, width, col_of=None):
        g, tpg = self.groups, self.tiles_per_group
        if col_of is None:
            return pl.BlockSpec((1, self.cond_rows, width), lambda i, j: (chunk * g + i // tpg, 0, 0))
        return pl.BlockSpec((1, self.cond_rows, width), lambda i, j: (chunk * g + i // tpg, 0, col_of(j)))


def _norm_mod(x, gain, shift, scale):
    ms = jnp.mean(x * x, axis=-1, keepdims=True)
    y = x * lax.rsqrt(ms + EPS) * gain
    return y * (1.0 + scale) + shift


def _mod_kernel(c_ref, w_ref, b_ref, o_ref):
    c = c_ref[...]
    a = (c * jax.nn.sigmoid(c)).astype(BF16)
    o_ref[0, 0] = jnp.dot(a, w_ref[0].astype(BF16), preferred_element_type=F32) + b_ref[0]


def _modulation(c_all, w, b):
    n_layers, d, n_out = w.shape
    n_chunks = n_out // d
    tn = _pick_tile(d, 1024)
    per_chunk = d // tn
    return pl.pallas_call(
        _mod_kernel,
        grid=(n_layers, n_out // tn),
        in_specs=[
            pl.BlockSpec((MOD_ROWS, d), lambda l, j: (0, 0)),
            pl.BlockSpec((1, d, tn), lambda l, j: (l, 0, j)),
            pl.BlockSpec((1, 1, tn), lambda l, j: (l, 0, j)),
        ],
        out_specs=pl.BlockSpec((1, 1, MOD_ROWS, tn), lambda l, j: (l, j // per_chunk, 0, j % per_chunk)),
        out_shape=jax.ShapeDtypeStruct((n_layers, n_chunks, MOD_ROWS, d), F32),
        compiler_params=_params(("arbitrary", "arbitrary"), 40),
        name="modulation",
    )(c_all, w, b.reshape(n_layers, 1, n_out))


def _ffn_kernel(x_ref, sh_ref, sc_ref, gt_ref, ng_ref, wg_ref, wu_ref, wo_ref, o_ref, xn_ref):
    f = pl.program_id(1)

    @pl.when(f == 0)
    def _():
        u = _norm_mod(x_ref[...], ng_ref[0], sh_ref[0], sc_ref[0])
        xn_ref[...] = u.astype(BF16)
        o_ref[...] = jnp.zeros_like(o_ref)

    xn = xn_ref[...]
    gate = jnp.dot(xn, wg_ref[...].astype(BF16), preferred_element_type=F32)
    up = jnp.dot(xn, wu_ref[...].astype(BF16), preferred_element_type=F32)
    act = (gate * jax.nn.sigmoid(gate) * up).astype(BF16)
    o_ref[...] += jnp.dot(act, wo_ref[...].astype(BF16), preferred_element_type=F32)

    @pl.when(f == pl.num_programs(1) - 1)
    def _():
        o_ref[...] = x_ref[...] + 0.5 * gt_ref[0] * o_ref[...]


def _ffn(h, rows, mod, chunk0, norm_g, norm_idx, w_in, w_out, w_idx, tf):
    m, d = h.shape
    d_ff = w_out.shape[1]
    n_f = d_ff // tf
    tm = rows.tm
    return pl.pallas_call(
        _ffn_kernel,
        grid=(m // tm, n_f),
        in_specs=[
            pl.BlockSpec((tm, d), lambda i, f: (i, 0)),
            rows.cond_spec(chunk0, d),
            rows.cond_spec(chunk0 + 1, d),
            rows.cond_spec(chunk0 + 2, d),
            pl.BlockSpec((1, 1, d), lambda i, f: (norm_idx, 0, 0)),
            pl.BlockSpec((None, d, tf), lambda i, f: (w_idx, 0, f)),
            pl.BlockSpec((None, d, tf), lambda i, f: (w_idx, 0, n_f + f)),
            pl.BlockSpec((None, tf, d), lambda i, f: (w_idx, f, 0)),
        ],
        out_specs=pl.BlockSpec((tm, d), lambda i, f: (i, 0)),
        out_shape=jax.ShapeDtypeStruct((m, d), F32),
        scratch_shapes=[pltpu.VMEM((tm, d), BF16)],
        compiler_params=_params(("arbitrary", "arbitrary"), 58),
        name="ffn",
    )(h, mod, mod, mod, norm_g, w_in, w_in, w_out)


def _proj_kernel(x_ref, sh_ref, sc_ref, ng_ref, w_ref, hg_ref, *rest, n_norm_tiles, with_bf16):
    if with_bf16:
        o_ref, obf_ref, xn_ref = rest
    else:
        o_ref, xn_ref = rest
        obf_ref = None
    j = pl.program_id(1)

    @pl.when(j == 0)
    def _():
        xn_ref[...] = _norm_mod(x_ref[...], ng_ref[0], sh_ref[0], sc_ref[0]).astype(BF16)

    acc = jnp.dot(xn_ref[...], w_ref[...].astype(BF16), preferred_element_type=F32)
    tn = acc.shape[1]

    def store(val, sl):
        o_ref[:, sl] = val
        if obf_ref is not None:
            obf_ref[:, sl] = val.astype(BF16)

    @pl.when(j < n_norm_tiles)
    def _():
        for t in range(tn // HEAD_DIM):
            sl = slice(t * HEAD_DIM, (t + 1) * HEAD_DIM)
            a = acc[:, sl]
            ms = jnp.mean(a * a, axis=-1, keepdims=True)
            store(a * lax.rsqrt(ms + EPS) * hg_ref[:, sl], sl)

    @pl.when(j >= n_norm_tiles)
    def _():
        store(acc, slice(None))


def _proj(h, rows, mod, chunk0, norm_g, norm_idx, w, w_idx, head_gain, n_norm_cols, tn, with_bf16=False):
    m, d = h.shape
    n = w.shape[-1]
    tm = rows.tm
    assert n % tn == 0 and n_norm_cols % tn == 0 and tn % HEAD_DIM == 0
    out_shape = [jax.ShapeDtypeStruct((m, n), F32)]
    out_specs = [pl.BlockSpec((tm, tn), lambda i, j: (i, j))]
    if with_bf16:
        out_shape.append(jax.ShapeDtypeStruct((m, n), BF16))
        out_specs.append(pl.BlockSpec((tm, tn), lambda i, j: (i, j)))
    res = pl.pallas_call(
        functools.partial(_proj_kernel, n_norm_tiles=n_norm_cols // tn, with_bf16=with_bf16),
        grid=(m // tm, n // tn),
        in_specs=[
            pl.BlockSpec((tm, d), lambda i, j: (i, 0)),
            rows.cond_spec(chunk0, d),
            rows.cond_spec(chunk0 + 1, d),
            pl.BlockSpec((1, 1, d), lambda i, j: (norm_idx, 0, 0)),
            pl.BlockSpec((None, d, tn), lambda i, j: (w_idx, 0, j)),
            pl.BlockSpec((1, tn), lambda i, j: (0, j)),
        ],
        out_specs=out_specs,
        out_shape=out_shape,
        scratch_shapes=[pltpu.VMEM((tm, d), BF16)],
        compiler_params=_params(("arbitrary", "arbitrary"), 62),
        name="proj",
    )(h, mod, mod, norm_g, w, head_gain)
    return res if with_bf16 else res[0]


def _out_kernel(x_ref, w_ref, h_ref, gt_ref, o_ref):
    y = jnp.dot(x_ref[...].astype(BF16), w_ref[...].astype(BF16), preferred_element_type=F32)
    o_ref[...] = h_ref[...] + gt_ref[0] * y


def _out_proj(x, w, w_idx, h, rows, mod, chunk, tn):
    m, k = x.shape
    n = w.shape[-1]
    tm = rows.tm
    return pl.pallas_call(
        _out_kernel,
        grid=(m // tm, n // tn),
        in_specs=[
            pl.BlockSpec((tm, k), lambda i, j: (i, 0)),
            pl.BlockSpec((None, k, tn), lambda i, j: (w_idx, 0, j)),
            pl.BlockSpec((tm, tn), lambda i, j: (i, j)),
            rows.cond_spec(chunk, tn, col_of=lambda j: j),
        ],
        out_specs=pl.BlockSpec((tm, tn), lambda i, j: (i, j)),
        out_shape=jax.ShapeDtypeStruct((m, n), F32),
        compiler_params=_params(("arbitrary", "arbitrary"), 48),
        name="out_proj",
    )(x, w, h, mod)


def _dil_kernel(slopes_ref, q_ref, kc_ref, kp_ref, vc_ref, vp_ref, o_ref, lse_ref, *, dil, heads, head0):
    n = pl.program_id(1)
    hb = pl.program_id(2)
    scale = 1.0 / math.sqrt(HEAD_DIM)
    qi = lax.broadcasted_iota(jnp.int32, (A_BLOCK, A_BLOCK), 0)
    ki = lax.broadcasted_iota(jnp.int32, (A_BLOCK, A_BLOCK), 1)
    step_c = qi - ki
    step_p = step_c + A_BLOCK
    valid_c = step_c >= 0
    valid_p = jnp.logical_and(step_p <= A_BLOCK, n > 0)
    dist_c = (step_c * dil).astype(F32)
    dist_p = (step_p * dil).astype(F32)
    nt = (((1,), (1,)), ((), ()))
    for hh in range(heads):
        slope = slopes_ref[head0 + hb * heads + hh]
        bias_c = jnp.where(valid_c, -slope * dist_c, NEG_BIG)
        bias_p = jnp.where(valid_p, -slope * dist_p, NEG_BIG)
        sl = slice(hh * HEAD_DIM, (hh + 1) * HEAD_DIM)
        for res in range(dil):
            rows = pl.ds(res, A_BLOCK, stride=dil) if dil > 1 else slice(None)
            q = (q_ref[rows, sl] * scale).astype(BF16)
            s_c = lax.dot_general(q, kc_ref[rows, sl].astype(BF16), nt, preferred_element_type=F32) + bias_c
            s_p = lax.dot_general(q, kp_ref[rows, sl].astype(BF16), nt, preferred_element_type=F32) + bias_p
            mx = jnp.maximum(jnp.max(s_c, axis=-1, keepdims=True), jnp.max(s_p, axis=-1, keepdims=True))
            p_c = jnp.exp(s_c - mx)
            p_p = jnp.exp(s_p - mx)
            den = jnp.sum(p_c, axis=-1, keepdims=True) + jnp.sum(p_p, axis=-1, keepdims=True)
            o = jnp.dot(p_c.astype(BF16), vc_ref[rows, sl].astype(BF16), preferred_element_type=F32)
            o = o + jnp.dot(p_p.astype(BF16), vp_ref[rows, sl].astype(BF16), preferred_element_type=F32)
            o_ref[rows, sl] = o / den
            lse_ref[rows, sl] = jnp.broadcast_to(mx + jnp.log(den), (A_BLOCK, HEAD_DIM))


def _dilated_prompt(qkv, slopes, batch, seq, g, heads):
    dil = DIL_RATES[g]
    span = dil * A_BLOCK
    assert DIL_WINDOWS[g] // dil == A_BLOCK and seq % span == 0 and GROUP_HEADS % heads == 0
    nb = seq // span
    cw = heads * HEAD_DIM
    q_col = g * GROUP_COLS // cw
    k_col = (A_HEADS * HEAD_DIM + g * GROUP_COLS) // cw
    v_col = (2 * A_HEADS * HEAD_DIM + g * GROUP_COLS) // cw
    blk = (span, cw)

    def cur(col):
        return pl.BlockSpec(blk, lambda b, n, hb: (b * nb + n, col + hb))

    def prev(col):
        return pl.BlockSpec(blk, lambda b, n, hb: (b * nb + jnp.maximum(n - 1, 0), col + hb))

    out_spec = pl.BlockSpec(blk, lambda b, n, hb: (b * nb + n, hb))
    return pl.pallas_call(
        functools.partial(_dil_kernel, dil=dil, heads=heads, head0=g * GROUP_HEADS),
        grid=(batch, nb, GROUP_HEADS // heads),
        in_specs=[pl.BlockSpec(memory_space=pltpu.SMEM), cur(q_col), cur(k_col), prev(k_col), cur(v_col), prev(v_col)],
        out_specs=[out_spec, out_spec],
        out_shape=[jax.ShapeDtypeStruct((batch * seq, GROUP_COLS), F32)] * 2,
        compiler_params=_params(("arbitrary", "arbitrary", "arbitrary"), 40),
        name="dilated_prompt",
    )(slopes, qkv, qkv, qkv, qkv, qkv)


def _mix_out_kernel(o0_ref, o1_ref, o2_ref, l0_ref, l1_ref, l2_ref, w_ref, h_ref, gt_ref, out_ref, mix_ref):
    j = pl.program_id(1)

    @pl.when(j == 0)
    def _():
        ls = []
        for l_ref in (l0_ref, l1_ref, l2_ref):
            heads = [l_ref[:, hh * HEAD_DIM:(hh + 1) * HEAD_DIM] for hh in range(GROUP_HEADS)]
            htop = functools.reduce(jnp.maximum, heads)
            htot = functools.reduce(lambda a, b: a + b, [jnp.exp(l - htop) for l in heads])
            ls.append(htop + jnp.log(htot) - math.log(GROUP_HEADS))
        top = jnp.maximum(jnp.maximum(ls[0], ls[1]), ls[2])
        es = [jnp.exp(l - top) for l in ls]
        den = es[0] + es[1] + es[2]
        for g, o_ref in enumerate((o0_ref, o1_ref, o2_ref)):
            alpha = es[g] / den
            for hh in range(GROUP_HEADS):
                src = slice(hh * HEAD_DIM, (hh + 1) * HEAD_DIM)
                dst = slice(g * GROUP_COLS + hh * HEAD_DIM, g * GROUP_COLS + (hh + 1) * HEAD_DIM)
                mix_ref[:, dst] = (o_ref[:, src] * alpha).astype(BF16)

    y = jnp.dot(mix_ref[...], w_ref[...].astype(BF16), preferred_element_type=F32)
    out_ref[...] = h_ref[...] + gt_ref[0] * y


def _mix_out(outs, lses, w, w_idx, h, rows, mod, chunk, tn):
    m = h.shape[0]
    n = w.shape[-1]
    k = w.shape[-2]
    tm = rows.tm
    o_spec = pl.BlockSpec((tm, GROUP_COLS), lambda i, j: (i, 0))
    return pl.pallas_call(
        _mix_out_kernel,
        grid=(m // tm, n // tn),
        in_specs=[
            o_spec, o_spec, o_spec, o_spec, o_spec, o_spec,
            pl.BlockSpec((None, k, tn), lambda i, j: (w_idx, 0, j)),
            pl.BlockSpec((tm, tn), lambda i, j: (i, j)),
            rows.cond_spec(chunk, tn, col_of=lambda j: j),
        ],
        out_specs=pl.BlockSpec((tm, tn), lambda i, j: (i, j)),
        out_shape=jax.ShapeDtypeStruct((m, n), F32),
        scratch_shapes=[pltpu.VMEM((tm, k), BF16)],
        compiler_params=_params(("arbitrary", "arbitrary"), 58),
        name="mix_out",
    )(*outs, *lses, w, h, mod)


def _dil_sample_kernel(qkv_ref, c0_ref, c1_ref, c2_ref, o_ref, *, slopes):
    scale = 1.0 / math.sqrt(HEAD_DIM)
    nt = (((1,), (1,)), ((), ()))
    kbase = A_HEADS * HEAD_DIM
    vbase = 2 * A_HEADS * HEAD_DIM
    steps = (A_BLOCK - lax.broadcasted_iota(jnp.int32, (1, A_BLOCK), 1)).astype(F32)
    outs, glses = [], []
    for g, c_ref in enumerate((c0_ref, c1_ref, c2_ref)):
        dil = DIL_RATES[g]
        head_outs, head_lses = [], []
        for hh in range(GROUP_HEADS):
            col = (g * GROUP_HEADS + hh) * HEAD_DIM
            q = qkv_ref[0, :, col:col + HEAD_DIM]
            k_new = qkv_ref[0, :, kbase + col:kbase + col + HEAD_DIM]
            v_new = qkv_ref[0, :, vbase + col:vbase + col + HEAD_DIM]
            k_buf = c_ref[0, 0, :, hh * HEAD_DIM:(hh + 1) * HEAD_DIM]
            v_buf = c_ref[0, 0, :, GROUP_COLS + hh * HEAD_DIM:GROUP_COLS + (hh + 1) * HEAD_DIM]
            q8 = jnp.broadcast_to(q * scale, (8, HEAD_DIM)).astype(BF16)
            s_buf = lax.dot_general(q8, k_buf.astype(BF16), nt, preferred_element_type=F32)[0:1]
            s_buf = s_buf - slopes[g * GROUP_HEADS + hh] * dil * steps
            s_new = jnp.sum((q * scale).astype(BF16).astype(F32) * k_new.astype(BF16).astype(F32),
                            axis=-1, keepdims=True)
            mx = jnp.maximum(jnp.max(s_buf, axis=-1, keepdims=True), s_new)
            p_buf = jnp.exp(s_buf - mx)
            p_new = jnp.exp(s_new - mx)
            den = jnp.sum(p_buf, axis=-1, keepdims=True) + p_new
            p8 = jnp.broadcast_to(p_buf, (8, A_BLOCK)).astype(BF16)
            o = jnp.dot(p8, v_buf.astype(BF16), preferred_element_type=F32)[0:1]
            o = (o + p_new.astype(BF16).astype(F32) * v_new.astype(BF16).astype(F32)) / den
            head_outs.append(o)
            head_lses.append(mx + jnp.log(den))
        top = functools.reduce(jnp.maximum, head_lses)
        tot = functools.reduce(lambda a, b: a + b, [jnp.exp(l - top) for l in head_lses])
        glses.append(top + jnp.log(tot) - math.log(GROUP_HEADS))
        outs.append(head_outs)
    top = functools.reduce(jnp.maximum, glses)
    es = [jnp.exp(l - top) for l in glses]
    den = es[0] + es[1] + es[2]
    for g in range(N_GROUPS):
        alpha = es[g] / den
        for hh in range(GROUP_HEADS):
            col = (g * GROUP_HEADS + hh) * HEAD_DIM
            o_ref[0, :, col:col + HEAD_DIM] = outs[g][hh] * alpha


def _dilated_sample(qkv_s, caches, layer):
    db = qkv_s.shape[0]
    row_cols = 2 * GROUP_COLS
    views, specs = [], []
    for g, c in enumerate(caches):
        dil = DIL_RATES[g]
        assert c.shape[2] == DIL_WINDOWS[g] and c.shape[2] // dil == A_BLOCK
        views.append(c[:, :, ::dil].reshape(c.shape[0], db, A_BLOCK, row_cols))
        specs.append(pl.BlockSpec((1, 1, A_BLOCK, row_cols), lambda b: (layer, b, 0, 0)))
    out = pl.pallas_call(
        functools.partial(_dil_sample_kernel, slopes=tuple(_alibi_slopes())),
        grid=(db,),
        in_specs=[pl.BlockSpec((1, 1, QKV_COLS), lambda b: (b, 0, 0))] + specs,
        out_specs=pl.BlockSpec((1, 1, A_HEADS * HEAD_DIM), lambda b: (b, 0, 0)),
        out_shape=jax.ShapeDtypeStruct((db, 1, A_HEADS * HEAD_DIM), F32),
        compiler_params=_params(("arbitrary",), 32),
        name="dilated_sample",
    )(qkv_s.reshape(db, 1, QKV_COLS), *views)
    return out.reshape(db, A_HEADS * HEAD_DIM)


def _stick_terms(z):
    lo = jnp.minimum(z, 0.0)
    hi = jnp.maximum(z, 0.0)
    l = jnp.log(1.0 + jnp.exp(lo - hi))
    return lo - l, hi + l


def _sb_prompt_kernel(bias_ref, q_ref, k_ref, v_ref, o_ref, *, tq, tk, sub, hpb):
    hb = pl.program_id(1)
    i = pl.program_id(2)
    n_sub = tk // sub
    biases = [bias_ref[hb * hpb + hh] for hh in range(hpb)]
    heads = [slice(hh * HEAD_DIM, (hh + 1) * HEAD_DIM) for hh in range(hpb)]
    qs = [(q_ref[0, :, sl] * (1.0 / math.sqrt(HEAD_DIM))).astype(BF16) for sl in heads]
    nt = (((1,), (1,)), ((), ()))
    r = lax.broadcasted_iota(jnp.int32, (sub, sub), 0)
    c = lax.broadcasted_iota(jnp.int32, (sub, sub), 1)
    later_mat = jnp.where(r > c, 1.0, 0.0).astype(BF16)
    q_pos = lax.broadcasted_iota(jnp.int32, (tq, sub), 0)
    if tq == tk:
        n_full = i
    else:
        n_full = (i * tq) // tk
        q_pos = q_pos + (i * tq - n_full * tk)
    k_off = lax.broadcasted_iota(jnp.int32, (tq, sub), 1)

    def block(start, state, masked):
        return tuple(head_block(start, hh, state[hh][0], state[hh][1], masked) for hh in range(hpb))

    def head_block(start, hh, carry, acc, masked):
        k = k_ref[0, pl.ds(start, tk), heads[hh]]
        v = v_ref[0, pl.ds(start, tk), heads[hh]]
        z_all = lax.dot_general(qs[hh], k, nt, preferred_element_type=F32) + biases[hh]
        ws = [None] * n_sub
        for cidx in reversed(range(n_sub)):
            log_take, cost = _stick_terms(z_all[:, cidx * sub:(cidx + 1) * sub])
            if masked:
                causal = (k_off + cidx * sub) < q_pos
                cost = jnp.where(causal, cost, 0.0)
            later = jnp.dot(cost.astype(BF16), later_mat, preferred_element_type=F32)
            w = jnp.exp(log_take - later - carry)
            if masked:
                w = jnp.where(causal, w, 0.0)
            ws[cidx] = w.astype(BF16)
            carry = carry + jnp.sum(cost, axis=-1, keepdims=True)
        acc = acc + jnp.dot(jnp.concatenate(ws, axis=1), v, preferred_element_type=F32)
        return carry, acc

    zeros = tuple((jnp.zeros((tq, 1), F32), jnp.zeros((tq, HEAD_DIM), F32)) for _ in range(hpb))
    state = block(pl.multiple_of(n_full * tk, tk), zeros, True)

    def body(t, state):
        return block(pl.multiple_of((n_full - 1 - t) * tk, tk), state, False)

    state = lax.fori_loop(0, n_full, body, state)
    for hh in range(hpb):
        o_ref[0, :, heads[hh]] = state[hh][1].astype(o_ref.dtype)


def _sb_prompt(q, kv_bf16, sb_bias, batch, seq, tq, tk):
    n_heads = q.shape[1] // HEAD_DIM
    hpb = 4 if n_heads % 4 == 0 else 1
    n_hb = n_heads // hpb
    assert tk % tq == 0 and seq % tk == 0
    q3 = q.reshape(batch, seq, n_heads * HEAD_DIM)
    kv3 = kv_bf16.reshape(batch, seq, 2 * n_heads * HEAD_DIM)
    out = pl.pallas_call(
        functools.partial(_sb_prompt_kernel, tq=tq, tk=tk, sub=min(tk, 2 * LANES), hpb=hpb),
        grid=(batch, n_hb, seq // tq),
        in_specs=[
            pl.BlockSpec(memory_space=pltpu.SMEM),
            pl.BlockSpec((1, tq, hpb * HEAD_DIM), lambda b, h, i: (b, i, h)),
            pl.BlockSpec((1, seq, hpb * HEAD_DIM), lambda b, h, i: (b, 0, h)),
            pl.BlockSpec((1, seq, hpb * HEAD_DIM), lambda b, h, i: (b, 0, n_hb + h)),
        ],
        out_specs=pl.BlockSpec((1, tq, hpb * HEAD_DIM), lambda b, h, i: (b, i, h)),
        out_shape=jax.ShapeDtypeStruct((batch, seq, n_heads * HEAD_DIM), BF16),
        compiler_params=_params(("arbitrary", "arbitrary", "arbitrary"), 40),
        name="sb_prompt",
    )(sb_bias, q3, kv3, kv3)
    return out.reshape(batch * seq, n_heads * HEAD_DIM)


def _relayout_page(x_ref, o_ref, t, n_heads):
    page = o_ref.shape[1]
    keys = 2 * LANES // n_heads
    n = keys * n_heads
    i = lax.broadcasted_iota(jnp.int32, (n, n), 0)
    j = lax.broadcasted_iota(jnp.int32, (n, n), 1)
    perm = jnp.where(jnp.logical_and(i // keys == j % n_heads, i % keys == j // n_heads), 1.0, 0.0).astype(BF16)
    for c in range(page // keys):
        k_rows = x_ref[c * keys:(c + 1) * keys, 0:n_heads, :].reshape(n, HEAD_DIM)
        v_rows = x_ref[c * keys:(c + 1) * keys, n_heads:2 * n_heads, :].reshape(n, HEAD_DIM)
        both = jnp.concatenate([k_rows, v_rows], axis=1).astype(BF16)
        y = jnp.dot(perm, both, preferred_element_type=F32)
        for hh in range(n_heads):
            blk = y[hh * keys:(hh + 1) * keys].astype(BF16)
            o_ref[t, c * keys:(c + 1) * keys, hh * HEAD_DIM:(hh + 1) * HEAD_DIM] = blk[:, :HEAD_DIM]
            o_ref[t, c * keys:(c + 1) * keys, (n_heads + hh) * HEAD_DIM:(n_heads + hh + 1) * HEAD_DIM] = (
                blk[:, HEAD_DIM:])


def _sb_sample_step(qmat_ref, bias_ref, kv_ref, o_ref, acc_ref, carry_ref, n_heads):
    p = pl.program_id(1)
    hd = n_heads * HEAD_DIM
    pages_per_step, page = kv_ref.shape[0], kv_ref.shape[1]

    @pl.when(p == 0)
    def _():
        acc_ref[...] = jnp.zeros_like(acc_ref)
        carry_ref[...] = jnp.zeros_like(carry_ref)

    r = lax.broadcasted_iota(jnp.int32, (page, page), 0)
    c = lax.broadcasted_iota(jnp.int32, (page, page), 1)
    later_mat = jnp.where(c > r, 1.0, 0.0).astype(BF16)
    k_all = kv_ref[:, :, :hd].reshape(pages_per_step * page, hd)
    half = hd // 2
    z = (jnp.dot(k_all[:, :half], qmat_ref[0, :half, :], preferred_element_type=F32)
         + jnp.dot(k_all[:, half:], qmat_ref[0, half:, :], preferred_element_type=F32) + bias_ref[...])
    log_take, cost = _stick_terms(z)
    carry = carry_ref[...]
    ws = [None] * pages_per_step
    for t in reversed(range(pages_per_step)):
        rows = slice(t * page, (t + 1) * page)
        later = jnp.dot(later_mat, cost[rows].astype(BF16), preferred_element_type=F32)
        w = jnp.exp(log_take[rows] - later - carry)
        ws[t] = w.T[:MOD_ROWS].astype(BF16)
        carry = carry + jnp.sum(cost[rows], axis=0, keepdims=True)
    carry_ref[...] = carry
    w_all = jnp.concatenate(ws, axis=1)
    v_all = kv_ref[:, :, hd:].reshape(pages_per_step * page, hd)
    acc_ref[...] += jnp.dot(w_all, v_all, preferred_element_type=F32)

    @pl.when(p == pl.num_programs(1) - 1)
    def _():
        for hh in range(n_heads):
            sl = slice(hh * HEAD_DIM, (hh + 1) * HEAD_DIM)
            o_ref[0, :, sl] = acc_ref[hh:hh + 1, sl]


def _sb_sample_kernel(qmat_ref, bias_ref, kv_ref, o_ref, acc_ref, carry_ref, *, n_heads):
    _sb_sample_step(qmat_ref, bias_ref, kv_ref, o_ref, acc_ref, carry_ref, n_heads)


def _sb_sample_gather_kernel(pt_ref, qmat_ref, bias_ref, *refs, n_heads, pages_per_step):
    x_refs = refs[:pages_per_step]
    o_ref, kv_ref, acc_ref, carry_ref = refs[pages_per_step:]
    for t, x_ref in enumerate(x_refs):
        _relayout_page(x_ref, kv_ref, t, n_heads)
    _sb_sample_step(qmat_ref, bias_ref, kv_ref, o_ref, acc_ref, carry_ref, n_heads)


def _sb_sample_operands(q, sb_bias):
    hd = q.shape[1]
    n_heads = hd // HEAD_DIM
    assert n_heads <= MOD_ROWS
    head_of_row = jnp.arange(hd, dtype=jnp.int32) // HEAD_DIM
    sel = (head_of_row[:, None] == jnp.arange(LANES, dtype=jnp.int32)[None, :]).astype(F32)
    qmat = ((q * (1.0 / math.sqrt(HEAD_DIM)))[:, :, None] * sel[None]).astype(BF16)
    bias_row = jnp.zeros((1, LANES), F32).at[0, :n_heads].set(sb_bias.astype(F32))
    return qmat, bias_row


def _sb_sample_gather(q, cache_kv, page_table, sb_bias, pages_per_step):
    db, hd = q.shape
    n_phys, page, _, n_heads, _ = cache_kv.shape
    n_pages = page_table.shape[1]
    slabs = 2 * n_heads
    assert (2 * LANES) % n_heads == 0 and page % (2 * LANES // n_heads) == 0 and n_heads % 8 == 0
    steps = n_pages // pages_per_step
    qmat, bias_row = _sb_sample_operands(q, sb_bias)
    rows = cache_kv.reshape(n_phys, page, slabs, HEAD_DIM)

    def page_spec(t):
        return pl.BlockSpec((None, page, slabs, HEAD_DIM),
                            lambda b, p, pt: (pt[b, (steps - 1 - p) * pages_per_step + t], 0, 0, 0))

    out, kv_pages = pl.pallas_call(
        functools.partial(_sb_sample_gather_kernel, n_heads=n_heads, pages_per_step=pages_per_step),
        grid_spec=pltpu.PrefetchScalarGridSpec(
            num_scalar_prefetch=1,
            grid=(db, steps),
            in_specs=[
                pl.BlockSpec((1, hd, LANES), lambda b, p, pt: (b, 0, 0)),
                pl.BlockSpec((1, LANES), lambda b, p, pt: (0, 0)),
            ] + [page_spec(t) for t in range(pages_per_step)],
            out_specs=[
                pl.BlockSpec((1, 1, hd), lambda b, p, pt: (b, 0, 0)),
                pl.BlockSpec((pages_per_step, page, 2 * hd), lambda b, p, pt: (b * steps + steps - 1 - p, 0, 0)),
            ],
            scratch_shapes=[pltpu.VMEM((MOD_ROWS, hd), F32), pltpu.VMEM((1, LANES), F32)],
        ),
        out_shape=[jax.ShapeDtypeStruct((db, 1, hd), F32),
                   jax.ShapeDtypeStruct((db * n_pages, page, 2 * hd), BF16)],
        compiler_params=_params(("arbitrary", "arbitrary"), 48),
        name="sb_sample_gather",
    )(page_table, qmat, bias_row, *([rows] * pages_per_step))
    return out.reshape(db, hd), kv_pages


def _sb_sample(q, kv_pages, sb_bias, pages_per_step):
    db, hd = q.shape
    n_heads = hd // HEAD_DIM
    page = kv_pages.shape[1]
    steps = kv_pages.shape[0] // (db * pages_per_step)
    qmat, bias_row = _sb_sample_operands(q, sb_bias)
    out = pl.pallas_call(
        functools.partial(_sb_sample_kernel, n_heads=n_heads),
        grid=(db, steps),
        in_specs=[
            pl.BlockSpec((1, hd, LANES), lambda b, p: (b, 0, 0)),
            pl.BlockSpec((1, LANES), lambda b, p: (0, 0)),
            pl.BlockSpec((pages_per_step, page, 2 * hd), lambda b, p: (b * steps + steps - 1 - p, 0, 0)),
        ],
        out_specs=pl.BlockSpec((1, 1, hd), lambda b, p: (b, 0, 0)),
        out_shape=jax.ShapeDtypeStruct((db, 1, hd), F32),
        scratch_shapes=[pltpu.VMEM((MOD_ROWS, hd), F32), pltpu.VMEM((1, LANES), F32)],
        compiler_params=_params(("arbitrary", "arbitrary"), 40),
        name="sb_sample",
    )(qmat, bias_row, kv_pages)
    return out.reshape(db, hd)


def kernel(x_prompt, x_sample, cache_win_g0, cache_win_g1, cache_win_g2, cache_kv, page_table, c_prompt, c_sample, w_mod, b_mod, norm_g, ffn_w_in, ffn_w_out, a_w_qkv, a_q_norm, a_k_norm, a_w_o, kv_norm, w_mod_kv, b_mod_kv, w_kv, sb_k_norm, b_w_q, b_q_norm, b_sb_bias, b_w_o):
    batch, seq, d = x_prompt.shape
    db, ds, _ = x_sample.shape
    depth = w_mod.shape[0]
    n_a = a_w_qkv.shape[0]
    d_ff = ffn_w_out.shape[2]
    n_b_heads = w_kv.shape[1] // (2 * HEAD_DIM)
    assert ds == 1 and db == 8 and db + batch <= MOD_ROWS
    caches = (cache_win_g0, cache_win_g1, cache_win_g2)

    c_all = jnp.zeros((MOD_ROWS, d), F32).at[:db].set(c_sample).at[db:db + batch].set(c_prompt)
    mod = _modulation(c_all, w_mod, b_mod)
    mod_kv = _modulation(c_all, w_mod_kv[None], b_mod_kv[None])
    mod_s = mod[:, :, :db].reshape(depth * N_MOD, db, d)
    mod_p = mod[:, :, db:db + batch].reshape(depth * N_MOD * batch, 1, d)
    modkv_s = mod_kv[:, :, :db].reshape(2, db, d)
    modkv_p = mod_kv[:, :, db:db + batch].reshape(2 * batch, 1, d)

    tm_p = _pick_tile(seq, 1024)
    rows_p = _Rows(batch * seq, batch, 1, tm_p)
    rows_s = _Rows(db, 1, db, db)
    paths = (
        dict(rows=rows_p, mod=mod_p, modkv=modkv_p, tf=_pick_tile(d_ff, 256)),
        dict(rows=rows_s, mod=mod_s, modkv=modkv_s, tf=_pick_tile(d_ff, 512)),
    )

    norm3 = norm_g.reshape(depth * 3, 1, d)
    kvn3 = kv_norm.reshape(1, 1, d)
    w_in = ffn_w_in.reshape(depth * 2, d, 2 * d_ff)
    w_out = ffn_w_out.reshape(depth * 2, d_ff, d)
    w_kv3 = w_kv[None]

    a_gain = [jnp.concatenate([jnp.tile(a_q_norm[l], A_HEADS), jnp.tile(a_k_norm[l], A_HEADS),
                               jnp.ones((A_HEADS * HEAD_DIM,), F32)])[None] for l in range(n_a)]
    kv_gain = jnp.concatenate([jnp.tile(sb_k_norm, n_b_heads), jnp.ones((n_b_heads * HEAD_DIM,), F32)])[None]
    b_gain = [jnp.tile(b_q_norm[j], n_b_heads)[None] for j in range(depth - n_a)]

    h_p = x_prompt.reshape(batch * seq, d)
    h_s = x_sample.reshape(db, d)
    tn_qkv = GROUP_COLS
    tn_d = _pick_tile(d, 512)
    tn_w = _pick_tile(d, 1024)
    win_p = [[] for _ in range(N_GROUPS)]
    win_s = [[] for _ in range(N_GROUPS)]
    kv_p = kv_s = kv_p_bf16 = None
    n_pages = page_table.shape[1]
    pages_per_step = 4 if n_pages % 4 == 0 else 1
    kv_pages = None
    slopes = jnp.asarray(_alibi_slopes(), F32)

    def ffn(h, path, l, which):
        return _ffn(h, path["rows"], path["mod"], (l * N_MOD + 6 * which), norm3, l * 3 + 2 * which,
                    w_in, w_out, l * 2 + which, path["tf"])

    for l in range(depth):
        h_p = ffn(h_p, paths[0], l, 0)
        h_s = ffn(h_s, paths[1], l, 0)
        if l < n_a:
            qkv_p = _proj(h_p, rows_p, mod_p, l * N_MOD + 3, norm3, l * 3 + 1, a_w_qkv, l, a_gain[l],
                          2 * A_HEADS * HEAD_DIM, tn_qkv)
            qkv_s = _proj(h_s, rows_s, mod_s, l * N_MOD + 3, norm3, l * 3 + 1, a_w_qkv, l, a_gain[l],
                          2 * A_HEADS * HEAD_DIM, tn_qkv)
            outs, lses = [], []
            for g in range(N_GROUPS):
                o_g, lse_g = _dilated_prompt(qkv_p, slopes, batch, seq, g, GROUP_HEADS if DIL_RATES[g] == 1 else 1)
                outs.append(o_g)
                lses.append(lse_g)
            h_p = _mix_out(outs, lses, a_w_o, l, h_p, rows_p, mod_p, l * N_MOD + 5, tn_d)
            mixed_s = _dilated_sample(qkv_s, caches, l)
            h_s = _out_proj(mixed_s, a_w_o, l, h_s, rows_s, mod_s, l * N_MOD + 5, tn_w)

            qkv_p3 = qkv_p.reshape(batch, seq, QKV_COLS)
            qkv_s5 = qkv_s.reshape(db, 1, 3, A_HEADS, HEAD_DIM)
            for g in range(N_GROUPS):
                hs = slice(g * GROUP_HEADS, (g + 1) * GROUP_HEADS)
                keep = min(DIL_WINDOWS[g], seq)
                kv_rows = [qkv_p3[:, seq - keep:, part * A_HEADS * HEAD_DIM + g * GROUP_COLS:
                                  part * A_HEADS * HEAD_DIM + (g + 1) * GROUP_COLS] for part in (1, 2)]
                win_p[g].append(jnp.stack(kv_rows, axis=2).reshape(batch, keep, 2, GROUP_HEADS, HEAD_DIM))
                buf = caches[g][l]
                win_s[g].append(jnp.concatenate([buf[:, 1:], qkv_s5[:, :, 1:3, hs]], axis=1))
        else:
            j = l - n_a
            q_p = _proj(h_p, rows_p, mod_p, l * N_MOD + 3, norm3, l * 3 + 1, b_w_q, j, b_gain[j],
                        n_b_heads * HEAD_DIM, tn_w)
            q_s = _proj(h_s, rows_s, mod_s, l * N_MOD + 3, norm3, l * 3 + 1, b_w_q, j, b_gain[j],
                        n_b_heads * HEAD_DIM, tn_w)
            o_p = _sb_prompt(q_p, kv_p_bf16, b_sb_bias[j], batch, seq, _pick_tile(seq, 512), _pick_tile(seq, 512))
            h_p = _out_proj(o_p, b_w_o, j, h_p, rows_p, mod_p, l * N_MOD + 5, tn_w)
            if kv_pages is None:
                o_s, kv_pages = _sb_sample_gather(q_s, cache_kv, page_table, b_sb_bias[j], pages_per_step)
            else:
                o_s = _sb_sample(q_s, kv_pages, b_sb_bias[j], 2 * pages_per_step if n_pages % 8 == 0 else pages_per_step)
            h_s = _out_proj(o_s, b_w_o, j, h_s, rows_s, mod_s, l * N_MOD + 5, tn_w)
        h_p = ffn(h_p, paths[0], l, 1)
        h_s = ffn(h_s, paths[1], l, 1)
        if l == n_a - 1:
            kv_p, kv_p_bf16 = _proj(h_p, rows_p, modkv_p, 0, kvn3, 0, w_kv3, 0, kv_gain,
                                    n_b_heads * HEAD_DIM, tn_w, with_bf16=True)
            kv_s = _proj(h_s, rows_s, modkv_s, 0, kvn3, 0, w_kv3, 0, kv_gain, n_b_heads * HEAD_DIM, tn_w)

    y_p = h_p.reshape(batch, seq, d)
    y_s = h_s.reshape(db, 1, d)
    win_p = [jnp.stack(w, axis=0) for w in win_p]
    win_s = [jnp.stack(w, axis=0) for w in win_s]
    kv_p = kv_p.reshape(batch, seq, 2, n_b_heads, HEAD_DIM)
    kv_s = kv_s.reshape(db, 1, 2, n_b_heads, HEAD_DIM)
    return (y_p, y_s, win_p[0], win_p[1], win_p[2], kv_p, win_s[0], win_s[1], win_s[2], kv_s)
```

```python
import functools
import math

import numpy as np
import jax
import jax.numpy as jnp
from jax import lax
from jax.experimental import pallas as pl
from jax.experimental.pallas import tpu as pltpu

F32 = jnp.float32
BF16 = jnp.bfloat16

EPS = 1e-6
HEAD_DIM = 128
LANES = 128
N_MOD = 9
DIL_WINDOWS = (128, 512, 2048)
DIL_RATES = (1, 4, 16)
N_GROUPS = 3
GROUP_HEADS = 5
A_HEADS = N_GROUPS * GROUP_HEADS
A_BLOCK = 128
GROUP_COLS = GROUP_HEADS * HEAD_DIM
QKV_COLS = 3 * A_HEADS * HEAD_DIM
NEG_BIG = -1e30
NORM_CHUNK = 2 * LANES
MIB = 1024 * 1024
MOD_ROWS = 16


def _alibi_slopes():
    return [float(2.0 ** (-8.0 * (i + 1) / A_HEADS)) for i in range(A_HEADS)]


def _params(semantics, vmem_mib):
    return pltpu.CompilerParams(dimension_semantics=semantics, vmem_limit_bytes=vmem_mib * MIB)


def _pick_tile(n, preferred):
    if n <= preferred:
        return n
    t = (preferred // LANES) * LANES
    while t >= LANES:
        if n % t == 0:
            return t
        t -= LANES
    return n


class _Rows:
    def __init__(self, n_rows, groups, cond_rows, tm):
        self.n_rows = n_rows
        self.groups = groups
        self.cond_rows = cond_rows
        self.tm = tm
        self.tiles_per_group = (n_rows // groups) // tm
        assert self.tiles_per_group * tm * groups == n_rows

    def cond_spec(self, chunk, width, col_of=None):
        g, tpg = self.groups, self.tiles_per_group
        if col_of is None:
            return pl.BlockSpec((1, self.cond_rows, width), lambda i, j: (chunk * g + i // tpg, 0, 0))
        return pl.BlockSpec((1, self.cond_rows, width), lambda i, j: (chunk * g + i // tpg, 0, col_of(j)))


def _norm_mod(x, gain, shift, scale):
    ms = jnp.mean(x * x, axis=-1, keepdims=True)
    y = x * lax.rsqrt(ms + EPS) * gain
    return y * (1.0 + scale) + shift


def _mod_kernel(c_ref, w_ref, b_ref, o_ref):
    c = c_ref[...]
    a = (c * jax.nn.sigmoid(c)).astype(BF16)
    o_ref[0, 0] = jnp.dot(a, w_ref[0].astype(BF16), preferred_element_type=F32) + b_ref[0]


def _modulation(c_all, w, b):
    n_layers, d, n_out = w.shape
    n_chunks = n_out // d
    tn = _pick_tile(d, 1024)
    per_chunk = d // tn
    return pl.pallas_call(
        _mod_kernel,
        grid=(n_layers, n_out // tn),
        in_specs=[
            pl.BlockSpec((MOD_ROWS, d), lambda l, j: (0, 0)),
            pl.BlockSpec((1, d, tn), lambda l, j: (l, 0, j)),
            pl.BlockSpec((1, 1, tn), lambda l, j: (l, 0, j)),
        ],
        out_specs=pl.BlockSpec((1, 1, MOD_ROWS, tn), lambda l, j: (l, j // per_chunk, 0, j % per_chunk)),
        out_shape=jax.ShapeDtypeStruct((n_layers, n_chunks, MOD_ROWS, d), F32),
        compiler_params=_params(("arbitrary", "arbitrary"), 40),
        name="modulation",
    )(c_all, w, b.reshape(n_layers, 1, n_out))


def _ffn_kernel(x_ref, sh_ref, sc_ref, gt_ref, ng_ref, wg_ref, wu_ref, wo_ref, o_ref, xn_ref, *, row_chunks):
    f = pl.program_id(1)
    chunk = x_ref.shape[0] // row_chunks

    def swiglu(xn, wg, wu, wo):
        gate = jnp.dot(xn, wg, preferred_element_type=F32)
        up = jnp.dot(xn, wu, preferred_element_type=F32)
        act = (gate * jax.nn.sigmoid(gate) * up).astype(BF16)
        return jnp.dot(act, wo, preferred_element_type=F32)

    @pl.when(f == 0)
    def _():
        wg, wu, wo = wg_ref[...].astype(BF16), wu_ref[...].astype(BF16), wo_ref[...].astype(BF16)
        for c in range(row_chunks):
            rows = slice(c * chunk, (c + 1) * chunk)
            shift = sh_ref[0] if sh_ref.shape[1] == 1 else sh_ref[0, rows]
            scale = sc_ref[0] if sc_ref.shape[1] == 1 else sc_ref[0, rows]
            xn = _norm_mod(x_ref[rows], ng_ref[0], shift, scale).astype(BF16)
            xn_ref[rows] = xn
            o_ref[rows] = swiglu(xn, wg, wu, wo)

    @pl.when(f > 0)
    def _():
        o_ref[...] += swiglu(xn_ref[...], wg_ref[...].astype(BF16), wu_ref[...].astype(BF16),
                             wo_ref[...].astype(BF16))

    @pl.when(f == pl.num_programs(1) - 1)
    def _():
        o_ref[...] = x_ref[...] + 0.5 * gt_ref[0] * o_ref[...]


def _ffn(h, rows, mod, chunk0, norm_g, norm_idx, w_in, w_out, w_idx, tf):
    m, d = h.shape
    d_ff = w_out.shape[1]
    n_f = d_ff // tf
    tm = rows.tm
    row_chunks = 4 if tm % (4 * 2 * LANES) == 0 else 1
    return pl.pallas_call(
        functools.partial(_ffn_kernel, row_chunks=row_chunks),
        grid=(m // tm, n_f),
        in_specs=[
            pl.BlockSpec((tm, d), lambda i, f: (i, 0)),
            rows.cond_spec(chunk0, d),
            rows.cond_spec(chunk0 + 1, d),
            rows.cond_spec(chunk0 + 2, d),
            pl.BlockSpec((1, 1, d), lambda i, f: (norm_idx, 0, 0)),
            pl.BlockSpec((None, d, tf), lambda i, f: (w_idx, 0, f)),
            pl.BlockSpec((None, d, tf), lambda i, f: (w_idx, 0, n_f + f)),
            pl.BlockSpec((None, tf, d), lambda i, f: (w_idx, f, 0)),
        ],
        out_specs=pl.BlockSpec((tm, d), lambda i, f: (i, 0)),
        out_shape=jax.ShapeDtypeStruct((m, d), F32),
        scratch_shapes=[pltpu.VMEM((tm, d), BF16)],
        compiler_params=_params(("arbitrary", "arbitrary"), 58),
        name="ffn",
    )(h, mod, mod, mod, norm_g, w_in, w_in, w_out)


def _proj_kernel(x_ref, sh_ref, sc_ref, ng_ref, w_ref, hg_ref, *rest, n_norm_tiles, with_bf16, row_chunks):
    if with_bf16:
        o_ref, obf_ref, xn_ref = rest
    else:
        o_ref, xn_ref = rest
        obf_ref = None
    j = pl.program_id(1)
    tn = w_ref.shape[1]
    chunk = x_ref.shape[0] // row_chunks

    def store(val, rows, sl):
        o_ref[rows, sl] = val
        if obf_ref is not None:
            obf_ref[rows, sl] = val.astype(BF16)

    def normed_tile(xn, rows):
        for c0 in range(0, tn, NORM_CHUNK):
            c1 = min(c0 + NORM_CHUNK, tn)
            acc = jnp.dot(xn, w_ref[:, c0:c1].astype(BF16), preferred_element_type=F32)
            for t in range((c1 - c0) // HEAD_DIM):
                a = acc[:, t * HEAD_DIM:(t + 1) * HEAD_DIM]
                sl = slice(c0 + t * HEAD_DIM, c0 + (t + 1) * HEAD_DIM)
                ms = jnp.mean(a * a, axis=-1, keepdims=True)
                store(a * lax.rsqrt(ms + EPS) * hg_ref[:, sl], rows, sl)

    @pl.when(j == 0)
    def _():
        for c in range(row_chunks):
            rows = slice(c * chunk, (c + 1) * chunk)
            shift = sh_ref[0] if sh_ref.shape[1] == 1 else sh_ref[0, rows]
            scale = sc_ref[0] if sc_ref.shape[1] == 1 else sc_ref[0, rows]
            xn = _norm_mod(x_ref[rows], ng_ref[0], shift, scale).astype(BF16)
            xn_ref[rows] = xn
            normed_tile(xn, rows)

    @pl.when(jnp.logical_and(j > 0, j < n_norm_tiles))
    def _():
        normed_tile(xn_ref[...], slice(None))

    @pl.when(j >= n_norm_tiles)
    def _():
        store(jnp.dot(xn_ref[...], w_ref[...].astype(BF16), preferred_element_type=F32), slice(None), slice(None))


def _proj(h, rows, mod, chunk0, norm_g, norm_idx, w, w_idx, head_gain, n_norm_cols, tn, with_bf16=False):
    m, d = h.shape
    n = w.shape[-1]
    tm = rows.tm
    assert n % tn == 0 and n_norm_cols % tn == 0 and tn % HEAD_DIM == 0 and n_norm_cols >= tn
    row_chunks = 4 if tm % (4 * 2 * LANES) == 0 else 1
    out_shape = [jax.ShapeDtypeStruct((m, n), F32)]
    out_specs = [pl.BlockSpec((tm, tn), lambda i, j: (i, j))]
    if with_bf16:
        out_shape.append(jax.ShapeDtypeStruct((m, n), BF16))
        out_specs.append(pl.BlockSpec((tm, tn), lambda i, j: (i, j)))
    res = pl.pallas_call(
        functools.partial(_proj_kernel, n_norm_tiles=n_norm_cols // tn, with_bf16=with_bf16, row_chunks=row_chunks),
        grid=(m // tm, n // tn),
        in_specs=[
            pl.BlockSpec((tm, d), lambda i, j: (i, 0)),
            rows.cond_spec(chunk0, d),
            rows.cond_spec(chunk0 + 1, d),
            pl.BlockSpec((1, 1, d), lambda i, j: (norm_idx, 0, 0)),
            pl.BlockSpec((None, d, tn), lambda i, j: (w_idx, 0, j)),
            pl.BlockSpec((1, tn), lambda i, j: (0, j)),
        ],
        out_specs=out_specs,
        out_shape=out_shape,
        scratch_shapes=[pltpu.VMEM((tm, d), BF16)],
        compiler_params=_params(("arbitrary", "arbitrary"), 62),
        name="proj",
    )(h, mod, mod, norm_g, w, head_gain)
    return res if with_bf16 else res[0]


def _out_kernel(x_ref, w_ref, h_ref, gt_ref, o_ref):
    y = jnp.dot(x_ref[...].astype(BF16), w_ref[...].astype(BF16), preferred_element_type=F32)
    o_ref[...] = h_ref[...] + gt_ref[0] * y


def _out_proj(x, w, w_idx, h, rows, mod, chunk, tn):
    m, k = x.shape
    n = w.shape[-1]
    tm = rows.tm
    return pl.pallas_call(
        _out_kernel,
        grid=(m // tm, n // tn),
        in_specs=[
            pl.BlockSpec((tm, k), lambda i, j: (i, 0)),
            pl.BlockSpec((None, k, tn), lambda i, j: (w_idx, 0, j)),
            pl.BlockSpec((tm, tn), lambda i, j: (i, j)),
            rows.cond_spec(chunk, tn, col_of=lambda j: j),
        ],
        out_specs=pl.BlockSpec((tm, tn), lambda i, j: (i, j)),
        out_shape=jax.ShapeDtypeStruct((m, n), F32),
        compiler_params=_params(("arbitrary", "arbitrary"), 48),
        name="out_proj",
    )(x, w, h, mod)


def _dil_kernel(slopes_ref, q_ref, kc_ref, kp_ref, vc_ref, vp_ref, o_ref, lse_ref, *, dil, heads, head0):
    n = pl.program_id(1)
    hb = pl.program_id(2)
    scale = 1.0 / math.sqrt(HEAD_DIM)
    qi = lax.broadcasted_iota(jnp.int32, (A_BLOCK, A_BLOCK), 0)
    ki = lax.broadcasted_iota(jnp.int32, (A_BLOCK, A_BLOCK), 1)
    step_c = qi - ki
    step_p = step_c + A_BLOCK
    valid_c = step_c >= 0
    valid_p = jnp.logical_and(step_p <= A_BLOCK, n > 0)
    dist_c = (step_c * dil).astype(F32)
    dist_p = (step_p * dil).astype(F32)
    nt = (((1,), (1,)), ((), ()))
    for hh in range(heads):
        slope = slopes_ref[head0 + hb * heads + hh]
        bias_c = jnp.where(valid_c, -slope * dist_c, NEG_BIG)
        bias_p = jnp.where(valid_p, -slope * dist_p, NEG_BIG)
        sl = slice(hh * HEAD_DIM, (hh + 1) * HEAD_DIM)
        for res in range(dil):
            rows = pl.ds(res, A_BLOCK, stride=dil) if dil > 1 else slice(None)
            q = (q_ref[rows, sl] * scale).astype(BF16)
            s_c = lax.dot_general(q, kc_ref[rows, sl].astype(BF16), nt, preferred_element_type=F32) + bias_c
            s_p = lax.dot_general(q, kp_ref[rows, sl].astype(BF16), nt, preferred_element_type=F32) + bias_p
            mx = jnp.maximum(jnp.max(s_c, axis=-1, keepdims=True), jnp.max(s_p, axis=-1, keepdims=True))
            p_c = jnp.exp(s_c - mx)
            p_p = jnp.exp(s_p - mx)
            den = jnp.sum(p_c, axis=-1, keepdims=True) + jnp.sum(p_p, axis=-1, keepdims=True)
            o = jnp.dot(p_c.astype(BF16), vc_ref[rows, sl].astype(BF16), preferred_element_type=F32)
            o = o + jnp.dot(p_p.astype(BF16), vp_ref[rows, sl].astype(BF16), preferred_element_type=F32)
            o_ref[rows, sl] = o / den
            lse_ref[rows, sl] = jnp.broadcast_to(mx + jnp.log(den), (A_BLOCK, HEAD_DIM))


def _dilated_prompt(qkv, slopes, batch, seq, g, heads):
    dil = DIL_RATES[g]
    span = dil * A_BLOCK
    assert DIL_WINDOWS[g] // dil == A_BLOCK and seq % span == 0 and GROUP_HEADS % heads == 0
    nb = seq // span
    cw = heads * HEAD_DIM
    q_col = g * GROUP_COLS // cw
    k_col = (A_HEADS * HEAD_DIM + g * GROUP_COLS) // cw
    v_col = (2 * A_HEADS * HEAD_DIM + g * GROUP_COLS) // cw
    blk = (span, cw)

    def cur(col):
        return pl.BlockSpec(blk, lambda b, n, hb: (b * nb + n, col + hb))

    def prev(col):
        return pl.BlockSpec(blk, lambda b, n, hb: (b * nb + jnp.maximum(n - 1, 0), col + hb))

    out_spec = pl.BlockSpec(blk, lambda b, n, hb: (b * nb + n, hb))
    return pl.pallas_call(
        functools.partial(_dil_kernel, dil=dil, heads=heads, head0=g * GROUP_HEADS),
        grid=(batch, nb, GROUP_HEADS // heads),
        in_specs=[pl.BlockSpec(memory_space=pltpu.SMEM), cur(q_col), cur(k_col), prev(k_col), cur(v_col), prev(v_col)],
        out_specs=[out_spec, out_spec],
        out_shape=[jax.ShapeDtypeStruct((batch * seq, GROUP_COLS), F32)] * 2,
        compiler_params=_params(("arbitrary", "arbitrary", "arbitrary"), 40),
        name="dilated_prompt",
    )(slopes, qkv, qkv, qkv, qkv, qkv)


def _mix_out_kernel(o0_ref, o1_ref, o2_ref, l0_ref, l1_ref, l2_ref, w_ref, h_ref, gt_ref, out_ref, mix_ref):
    j = pl.program_id(1)

    @pl.when(j == 0)
    def _():
        ls = []
        for l_ref in (l0_ref, l1_ref, l2_ref):
            heads = [l_ref[:, hh * HEAD_DIM:(hh + 1) * HEAD_DIM] for hh in range(GROUP_HEADS)]
            htop = functools.reduce(jnp.maximum, heads)
            htot = functools.reduce(lambda a, b: a + b, [jnp.exp(l - htop) for l in heads])
            ls.append(htop + jnp.log(htot) - math.log(GROUP_HEADS))
        top = jnp.maximum(jnp.maximum(ls[0], ls[1]), ls[2])
        es = [jnp.exp(l - top) for l in ls]
        den = es[0] + es[1] + es[2]
        for g, o_ref in enumerate((o0_ref, o1_ref, o2_ref)):
            alpha = es[g] / den
            for hh in range(GROUP_HEADS):
                src = slice(hh * HEAD_DIM, (hh + 1) * HEAD_DIM)
                dst = slice(g * GROUP_COLS + hh * HEAD_DIM, g * GROUP_COLS + (hh + 1) * HEAD_DIM)
                mix_ref[:, dst] = (o_ref[:, src] * alpha).astype(BF16)

    y = jnp.dot(mix_ref[...], w_ref[...].astype(BF16), preferred_element_type=F32)
    out_ref[...] = h_ref[...] + gt_ref[0] * y


def _mix_out(outs, lses, w, w_idx, h, rows, mod, chunk, tn):
    m = h.shape[0]
    n = w.shape[-1]
    k = w.shape[-2]
    tm = rows.tm
    o_spec = pl.BlockSpec((tm, GROUP_COLS), lambda i, j: (i, 0))
    return pl.pallas_call(
        _mix_out_kernel,
        grid=(m // tm, n // tn),
        in_specs=[
            o_spec, o_spec, o_spec, o_spec, o_spec, o_spec,
            pl.BlockSpec((None, k, tn), lambda i, j: (w_idx, 0, j)),
            pl.BlockSpec((tm, tn), lambda i, j: (i, j)),
            rows.cond_spec(chunk, tn, col_of=lambda j: j),
        ],
        out_specs=pl.BlockSpec((tm, tn), lambda i, j: (i, j)),
        out_shape=jax.ShapeDtypeStruct((m, n), F32),
        scratch_shapes=[pltpu.VMEM((tm, k), BF16)],
        compiler_params=_params(("arbitrary", "arbitrary"), 58),
        name="mix_out",
    )(*outs, *lses, w, h, mod)


def _dil_sample_kernel(qkv_ref, c0_ref, c1_ref, c2_ref, o_ref, *, slopes):
    scale = 1.0 / math.sqrt(HEAD_DIM)
    nt = (((1,), (1,)), ((), ()))
    kbase = A_HEADS * HEAD_DIM
    vbase = 2 * A_HEADS * HEAD_DIM
    steps = (A_BLOCK - lax.broadcasted_iota(jnp.int32, (1, A_BLOCK), 1)).astype(F32)
    outs, glses = [], []
    for g, c_ref in enumerate((c0_ref, c1_ref, c2_ref)):
        dil = DIL_RATES[g]
        head_outs, head_lses = [], []
        for hh in range(GROUP_HEADS):
            col = (g * GROUP_HEADS + hh) * HEAD_DIM
            q = qkv_ref[0, :, col:col + HEAD_DIM]
            k_new = qkv_ref[0, :, kbase + col:kbase + col + HEAD_DIM]
            v_new = qkv_ref[0, :, vbase + col:vbase + col + HEAD_DIM]
            k_buf = c_ref[0, 0, :, hh * HEAD_DIM:(hh + 1) * HEAD_DIM]
            v_buf = c_ref[0, 0, :, GROUP_COLS + hh * HEAD_DIM:GROUP_COLS + (hh + 1) * HEAD_DIM]
            q8 = jnp.broadcast_to(q * scale, (8, HEAD_DIM)).astype(BF16)
            s_buf = lax.dot_general(q8, k_buf.astype(BF16), nt, preferred_element_type=F32)[0:1]
            s_buf = s_buf - slopes[g * GROUP_HEADS + hh] * dil * steps
            s_new = jnp.sum((q * scale).astype(BF16).astype(F32) * k_new.astype(BF16).astype(F32),
                            axis=-1, keepdims=True)
            mx = jnp.maximum(jnp.max(s_buf, axis=-1, keepdims=True), s_new)
            p_buf = jnp.exp(s_buf - mx)
            p_new = jnp.exp(s_new - mx)
            den = jnp.sum(p_buf, axis=-1, keepdims=True) + p_new
            p8 = jnp.broadcast_to(p_buf, (8, A_BLOCK)).astype(BF16)
            o = jnp.dot(p8, v_buf.astype(BF16), preferred_element_type=F32)[0:1]
            o = (o + p_new.astype(BF16).astype(F32) * v_new.astype(BF16).astype(F32)) / den
            head_outs.append(o)
            head_lses.append(mx + jnp.log(den))
        top = functools.reduce(jnp.maximum, head_lses)
        tot = functools.reduce(lambda a, b: a + b, [jnp.exp(l - top) for l in head_lses])
        glses.append(top + jnp.log(tot) - math.log(GROUP_HEADS))
        outs.append(head_outs)
    top = functools.reduce(jnp.maximum, glses)
    es = [jnp.exp(l - top) for l in glses]
    den = es[0] + es[1] + es[2]
    for g in range(N_GROUPS):
        alpha = es[g] / den
        for hh in range(GROUP_HEADS):
            col = (g * GROUP_HEADS + hh) * HEAD_DIM
            o_ref[0, :, col:col + HEAD_DIM] = outs[g][hh] * alpha


def _dilated_sample(qkv_s, caches, layer):
    db = qkv_s.shape[0]
    row_cols = 2 * GROUP_COLS
    views, specs = [], []
    for g, c in enumerate(caches):
        dil = DIL_RATES[g]
        assert c.shape[2] == DIL_WINDOWS[g] and c.shape[2] // dil == A_BLOCK
        views.append(c[:, :, ::dil].reshape(c.shape[0], db, A_BLOCK, row_cols))
        specs.append(pl.BlockSpec((1, 1, A_BLOCK, row_cols), lambda b: (layer, b, 0, 0)))
    out = pl.pallas_call(
        functools.partial(_dil_sample_kernel, slopes=tuple(_alibi_slopes())),
        grid=(db,),
        in_specs=[pl.BlockSpec((1, 1, QKV_COLS), lambda b: (b, 0, 0))] + specs,
        out_specs=pl.BlockSpec((1, 1, A_HEADS * HEAD_DIM), lambda b: (b, 0, 0)),
        out_shape=jax.ShapeDtypeStruct((db, 1, A_HEADS * HEAD_DIM), F32),
        compiler_params=_params(("arbitrary",), 32),
        name="dilated_sample",
    )(qkv_s.reshape(db, 1, QKV_COLS), *views)
    return out.reshape(db, A_HEADS * HEAD_DIM)


def _stick_terms(z):
    lo = jnp.minimum(z, 0.0)
    hi = jnp.maximum(z, 0.0)
    l = jnp.log(1.0 + jnp.exp(lo - hi))
    return lo - l, hi + l


def _sb_prompt_kernel(bias_ref, q_ref, k_ref, v_ref, o_ref, *, tq, tk, sub, hpb):
    hb = pl.program_id(1)
    i = pl.program_id(2)
    n_sub = tk // sub
    biases = [bias_ref[hb * hpb + hh] for hh in range(hpb)]
    heads = [slice(hh * HEAD_DIM, (hh + 1) * HEAD_DIM) for hh in range(hpb)]
    qs = [(q_ref[0, :, sl] * (1.0 / math.sqrt(HEAD_DIM))).astype(BF16) for sl in heads]
    nt = (((1,), (1,)), ((), ()))
    r = lax.broadcasted_iota(jnp.int32, (sub, sub), 0)
    c = lax.broadcasted_iota(jnp.int32, (sub, sub), 1)
    later_mat = jnp.where(r > c, 1.0, 0.0).astype(BF16)
    q_pos = lax.broadcasted_iota(jnp.int32, (tq, sub), 0)
    if tq == tk:
        n_full = i
    else:
        n_full = (i * tq) // tk
        q_pos = q_pos + (i * tq - n_full * tk)
    k_off = lax.broadcasted_iota(jnp.int32, (tq, sub), 1)

    def block(start, state, masked):
        return tuple(head_block(start, hh, state[hh][0], state[hh][1], masked) for hh in range(hpb))

    def head_block(start, hh, carry, acc, masked):
        k = k_ref[0, pl.ds(start, tk), heads[hh]]
        v = v_ref[0, pl.ds(start, tk), heads[hh]]
        z_all = lax.dot_general(qs[hh], k, nt, preferred_element_type=F32) + biases[hh]
        ws = [None] * n_sub
        for cidx in reversed(range(n_sub)):
            log_take, cost = _stick_terms(z_all[:, cidx * sub:(cidx + 1) * sub])
            if masked:
                causal = (k_off + cidx * sub) < q_pos
                cost = jnp.where(causal, cost, 0.0)
            later = jnp.dot(cost.astype(BF16), later_mat, preferred_element_type=F32)
            w = jnp.exp(log_take - later - carry)
            if masked:
                w = jnp.where(causal, w, 0.0)
            ws[cidx] = w.astype(BF16)
            carry = carry + jnp.sum(cost, axis=-1, keepdims=True)
        acc = acc + jnp.dot(jnp.concatenate(ws, axis=1), v, preferred_element_type=F32)
        return carry, acc

    zeros = tuple((jnp.zeros((tq, 1), F32), jnp.zeros((tq, HEAD_DIM), F32)) for _ in range(hpb))
    state = block(pl.multiple_of(n_full * tk, tk), zeros, True)

    def body(t, state):
        return block(pl.multiple_of((n_full - 1 - t) * tk, tk), state, False)

    state = lax.fori_loop(0, n_full, body, state)
    for hh in range(hpb):
        o_ref[0, :, heads[hh]] = state[hh][1].astype(o_ref.dtype)


def _sb_prompt(q, kv_bf16, sb_bias, batch, seq, tq, tk):
    n_heads = q.shape[1] // HEAD_DIM
    hpb = 4 if n_heads % 4 == 0 else 1
    n_hb = n_heads // hpb
    assert tk % tq == 0 and seq % tk == 0
    q3 = q.reshape(batch, seq, n_heads * HEAD_DIM)
    kv3 = kv_bf16.reshape(batch, seq, 2 * n_heads * HEAD_DIM)
    out = pl.pallas_call(
        functools.partial(_sb_prompt_kernel, tq=tq, tk=tk, sub=min(tk, 2 * LANES), hpb=hpb),
        grid=(batch, n_hb, seq // tq),
        in_specs=[
            pl.BlockSpec(memory_space=pltpu.SMEM),
            pl.BlockSpec((1, tq, hpb * HEAD_DIM), lambda b, h, i: (b, i, h)),
            pl.BlockSpec((1, seq, hpb * HEAD_DIM), lambda b, h, i: (b, 0, h)),
            pl.BlockSpec((1, seq, hpb * HEAD_DIM), lambda b, h, i: (b, 0, n_hb + h)),
        ],
        out_specs=pl.BlockSpec((1, tq, hpb * HEAD_DIM), lambda b, h, i: (b, i, h)),
        out_shape=jax.ShapeDtypeStruct((batch, seq, n_heads * HEAD_DIM), BF16),
        compiler_params=_params(("arbitrary", "arbitrary", "arbitrary"), 40),
        name="sb_prompt",
    )(sb_bias, q3, kv3, kv3)
    return out.reshape(batch * seq, n_heads * HEAD_DIM)


def _relayout_page(x_ref, o_ref, t, n_heads):
    page = o_ref.shape[1]
    keys = 2 * LANES // n_heads
    n = keys * n_heads
    i = lax.broadcasted_iota(jnp.int32, (n, n), 0)
    j = lax.broadcasted_iota(jnp.int32, (n, n), 1)
    perm = jnp.where(jnp.logical_and(i // keys == j % n_heads, i % keys == j // n_heads), 1.0, 0.0).astype(BF16)
    for c in range(page // keys):
        k_rows = x_ref[c * keys:(c + 1) * keys, 0:n_heads, :].reshape(n, HEAD_DIM)
        v_rows = x_ref[c * keys:(c + 1) * keys, n_heads:2 * n_heads, :].reshape(n, HEAD_DIM)
        both = jnp.concatenate([k_rows, v_rows], axis=1).astype(BF16)
        y = jnp.dot(perm, both, preferred_element_type=F32)
        for hh in range(n_heads):
            blk = y[hh * keys:(hh + 1) * keys].astype(BF16)
            o_ref[t, c * keys:(c + 1) * keys, hh * HEAD_DIM:(hh + 1) * HEAD_DIM] = blk[:, :HEAD_DIM]
            o_ref[t, c * keys:(c + 1) * keys, (n_heads + hh) * HEAD_DIM:(n_heads + hh + 1) * HEAD_DIM] = (
                blk[:, HEAD_DIM:])


def _sb_sample_step(qmat_ref, bias_ref, kv_ref, o_ref, acc_ref, carry_ref, n_heads):
    p = pl.program_id(1)
    hd = n_heads * HEAD_DIM
    pages_per_step, page = kv_ref.shape[0], kv_ref.shape[1]

    @pl.when(p == 0)
    def _():
        acc_ref[...] = jnp.zeros_like(acc_ref)
        carry_ref[...] = jnp.zeros_like(carry_ref)

    r = lax.broadcasted_iota(jnp.int32, (page, page), 0)
    c = lax.broadcasted_iota(jnp.int32, (page, page), 1)
    later_mat = jnp.where(c > r, 1.0, 0.0).astype(BF16)
    k_all = kv_ref[:, :, :hd].reshape(pages_per_step * page, hd)
    half = hd // 2
    z = (jnp.dot(k_all[:, :half], qmat_ref[0, :half, :], preferred_element_type=F32)
         + jnp.dot(k_all[:, half:], qmat_ref[0, half:, :], preferred_element_type=F32) + bias_ref[...])
    log_take, cost = _stick_terms(z)
    carry = carry_ref[...]
    ws = [None] * pages_per_step
    for t in reversed(range(pages_per_step)):
        rows = slice(t * page, (t + 1) * page)
        later = jnp.dot(later_mat, cost[rows].astype(BF16), preferred_element_type=F32)
        w = jnp.exp(log_take[rows] - later - carry)
        ws[t] = w.T[:MOD_ROWS].astype(BF16)
        carry = carry + jnp.sum(cost[rows], axis=0, keepdims=True)
    carry_ref[...] = carry
    w_all = jnp.concatenate(ws, axis=1)
    v_all = kv_ref[:, :, hd:].reshape(pages_per_step * page, hd)
    acc_ref[...] += jnp.dot(w_all, v_all, preferred_element_type=F32)

    @pl.when(p == pl.num_programs(1) - 1)
    def _():
        for hh in range(n_heads):
            sl = slice(hh * HEAD_DIM, (hh + 1) * HEAD_DIM)
            o_ref[0, :, sl] = acc_ref[hh:hh + 1, sl]


def _sb_sample_kernel(qmat_ref, bias_ref, kv_ref, o_ref, acc_ref, carry_ref, *, n_heads):
    _sb_sample_step(qmat_ref, bias_ref, kv_ref, o_ref, acc_ref, carry_ref, n_heads)


def _sb_sample_gather_kernel(pt_ref, qmat_ref, bias_ref, *refs, n_heads, pages_per_step):
    x_refs = refs[:pages_per_step]
    o_ref, kv_ref, acc_ref, carry_ref = refs[pages_per_step:]
    for t, x_ref in enumerate(x_refs):
        _relayout_page(x_ref, kv_ref, t, n_heads)
    _sb_sample_step(qmat_ref, bias_ref, kv_ref, o_ref, acc_ref, carry_ref, n_heads)


def _sb_sample_operands(q, sb_bias):
    hd = q.shape[1]
    n_heads = hd // HEAD_DIM
    assert n_heads <= MOD_ROWS
    head_of_row = jnp.arange(hd, dtype=jnp.int32) // HEAD_DIM
    sel = (head_of_row[:, None] == jnp.arange(LANES, dtype=jnp.int32)[None, :]).astype(F32)
    qmat = ((q * (1.0 / math.sqrt(HEAD_DIM)))[:, :, None] * sel[None]).astype(BF16)
    bias_row = jnp.zeros((1, LANES), F32).at[0, :n_heads].set(sb_bias.astype(F32))
    return qmat, bias_row


def _sb_sample_gather(q, cache_kv, page_table, sb_bias, pages_per_step):
    db, hd = q.shape
    n_phys, page, _, n_heads, _ = cache_kv.shape
    n_pages = page_table.shape[1]
    slabs = 2 * n_heads
    assert (2 * LANES) % n_heads == 0 and page % (2 * LANES // n_heads) == 0 and n_heads % 8 == 0
    steps = n_pages // pages_per_step
    qmat, bias_row = _sb_sample_operands(q, sb_bias)
    rows = cache_kv.reshape(n_phys, page, slabs, HEAD_DIM)

    def page_spec(t):
        return pl.BlockSpec((None, page, slabs, HEAD_DIM),
                            lambda b, p, pt: (pt[b, (steps - 1 - p) * pages_per_step + t], 0, 0, 0))

    out, kv_pages = pl.pallas_call(
        functools.partial(_sb_sample_gather_kernel, n_heads=n_heads, pages_per_step=pages_per_step),
        grid_spec=pltpu.PrefetchScalarGridSpec(
            num_scalar_prefetch=1,
            grid=(db, steps),
            in_specs=[
                pl.BlockSpec((1, hd, LANES), lambda b, p, pt: (b, 0, 0)),
                pl.BlockSpec((1, LANES), lambda b, p, pt: (0, 0)),
            ] + [page_spec(t) for t in range(pages_per_step)],
            out_specs=[
                pl.BlockSpec((1, 1, hd), lambda b, p, pt: (b, 0, 0)),
                pl.BlockSpec((pages_per_step, page, 2 * hd), lambda b, p, pt: (b * steps + steps - 1 - p, 0, 0)),
            ],
            scratch_shapes=[pltpu.VMEM((MOD_ROWS, hd), F32), pltpu.VMEM((1, LANES), F32)],
        ),
        out_shape=[jax.ShapeDtypeStruct((db, 1, hd), F32),
                   jax.ShapeDtypeStruct((db * n_pages, page, 2 * hd), BF16)],
        compiler_params=_params(("arbitrary", "arbitrary"), 48),
        name="sb_sample_gather",
    )(page_table, qmat, bias_row, *([rows] * pages_per_step))
    return out.reshape(db, hd), kv_pages


def _sb_sample(q, kv_pages, sb_bias, pages_per_step):
    db, hd = q.shape
    n_heads = hd // HEAD_DIM
    page = kv_pages.shape[1]
    steps = kv_pages.shape[0] // (db * pages_per_step)
    qmat, bias_row = _sb_sample_operands(q, sb_bias)
    out = pl.pallas_call(
        functools.partial(_sb_sample_kernel, n_heads=n_heads),
        grid=(db, steps),
        in_specs=[
            pl.BlockSpec((1, hd, LANES), lambda b, p: (b, 0, 0)),
            pl.BlockSpec((1, LANES), lambda b, p: (0, 0)),
            pl.BlockSpec((pages_per_step, page, 2 * hd), lambda b, p: (b * steps + steps - 1 - p, 0, 0)),
        ],
        out_specs=pl.BlockSpec((1, 1, hd), lambda b, p: (b, 0, 0)),
        out_shape=jax.ShapeDtypeStruct((db, 1, hd), F32),
        scratch_shapes=[pltpu.VMEM((MOD_ROWS, hd), F32), pltpu.VMEM((1, LANES), F32)],
        compiler_params=_params(("arbitrary", "arbitrary"), 40),
        name="sb_sample",
    )(qmat, bias_row, kv_pages)
    return out.reshape(db, hd)


def kernel(x_prompt, x_sample, cache_win_g0, cache_win_g1, cache_win_g2, cache_kv, page_table, c_prompt, c_sample, w_mod, b_mod, norm_g, ffn_w_in, ffn_w_out, a_w_qkv, a_q_norm, a_k_norm, a_w_o, kv_norm, w_mod_kv, b_mod_kv, w_kv, sb_k_norm, b_w_q, b_q_norm, b_sb_bias, b_w_o):
    batch, seq, d = x_prompt.shape
    db, ds, _ = x_sample.shape
    depth = w_mod.shape[0]
    n_a = a_w_qkv.shape[0]
    d_ff = ffn_w_out.shape[2]
    n_b_heads = w_kv.shape[1] // (2 * HEAD_DIM)
    assert ds == 1 and db == 8 and db + batch <= MOD_ROWS
    caches = (cache_win_g0, cache_win_g1, cache_win_g2)

    c_all = jnp.zeros((MOD_ROWS, d), F32).at[:db].set(c_sample).at[db:db + batch].set(c_prompt)
    mod = _modulation(c_all, w_mod, b_mod)
    mod_kv = _modulation(c_all, w_mod_kv[None], b_mod_kv[None])
    mod_s = mod[:, :, :db].reshape(depth * N_MOD, db, d)
    mod_p = mod[:, :, db:db + batch].reshape(depth * N_MOD * batch, 1, d)
    modkv_s = mod_kv[:, :, :db].reshape(2, db, d)
    modkv_p = mod_kv[:, :, db:db + batch].reshape(2 * batch, 1, d)

    tm_p = _pick_tile(seq, 1024)
    rows_p = _Rows(batch * seq, batch, 1, tm_p)
    rows_s = _Rows(db, 1, db, db)
    paths = (
        dict(rows=rows_p, mod=mod_p, modkv=modkv_p, tf=_pick_tile(d_ff, 256)),
        dict(rows=rows_s, mod=mod_s, modkv=modkv_s, tf=_pick_tile(d_ff, 512)),
    )

    norm3 = norm_g.reshape(depth * 3, 1, d)
    kvn3 = kv_norm.reshape(1, 1, d)
    w_in = ffn_w_in.reshape(depth * 2, d, 2 * d_ff)
    w_out = ffn_w_out.reshape(depth * 2, d_ff, d)
    w_kv3 = w_kv[None]

    a_gain = [jnp.concatenate([jnp.tile(a_q_norm[l], A_HEADS), jnp.tile(a_k_norm[l], A_HEADS),
                               jnp.ones((A_HEADS * HEAD_DIM,), F32)])[None] for l in range(n_a)]
    kv_gain = jnp.concatenate([jnp.tile(sb_k_norm, n_b_heads), jnp.ones((n_b_heads * HEAD_DIM,), F32)])[None]
    b_gain = [jnp.tile(b_q_norm[j], n_b_heads)[None] for j in range(depth - n_a)]

    h_p = x_prompt.reshape(batch * seq, d)
    h_s = x_sample.reshape(db, d)
    tn_qkv = GROUP_COLS
    tn_d = _pick_tile(d, 512)
    tn_w = _pick_tile(d, 1024)
    win_p = [[] for _ in range(N_GROUPS)]
    win_s = [[] for _ in range(N_GROUPS)]
    kv_p = kv_s = kv_p_bf16 = None
    n_pages = page_table.shape[1]
    pages_per_step = 4 if n_pages % 4 == 0 else 1
    kv_pages = None
    slopes = jnp.asarray(_alibi_slopes(), F32)

    def ffn(h, path, l, which):
        return _ffn(h, path["rows"], path["mod"], (l * N_MOD + 6 * which), norm3, l * 3 + 2 * which,
                    w_in, w_out, l * 2 + which, path["tf"])

    for l in range(depth):
        h_p = ffn(h_p, paths[0], l, 0)
        h_s = ffn(h_s, paths[1], l, 0)
        if l < n_a:
            qkv_p = _proj(h_p, rows_p, mod_p, l * N_MOD + 3, norm3, l * 3 + 1, a_w_qkv, l, a_gain[l],
                          2 * A_HEADS * HEAD_DIM, tn_qkv)
            qkv_s = _proj(h_s, rows_s, mod_s, l * N_MOD + 3, norm3, l * 3 + 1, a_w_qkv, l, a_gain[l],
                          2 * A_HEADS * HEAD_DIM, tn_qkv)
            outs, lses = [], []
            for g in range(N_GROUPS):
                o_g, lse_g = _dilated_prompt(qkv_p, slopes, batch, seq, g, GROUP_HEADS if DIL_RATES[g] == 1 else 1)
                outs.append(o_g)
                lses.append(lse_g)
            h_p = _mix_out(outs, lses, a_w_o, l, h_p, rows_p, mod_p, l * N_MOD + 5, tn_d)
            mixed_s = _dilated_sample(qkv_s, caches, l)
            h_s = _out_proj(mixed_s, a_w_o, l, h_s, rows_s, mod_s, l * N_MOD + 5, tn_w)

            qkv_p3 = qkv_p.reshape(batch, seq, QKV_COLS)
            qkv_s5 = qkv_s.reshape(db, 1, 3, A_HEADS, HEAD_DIM)
            for g in range(N_GROUPS):
                hs = slice(g * GROUP_HEADS, (g + 1) * GROUP_HEADS)
                keep = min(DIL_WINDOWS[g], seq)
                kv_rows = [qkv_p3[:, seq - keep:, part * A_HEADS * HEAD_DIM + g * GROUP_COLS:
                                  part * A_HEADS * HEAD_DIM + (g + 1) * GROUP_COLS] for part in (1, 2)]
                win_p[g].append(jnp.stack(kv_rows, axis=2).reshape(batch, keep, 2, GROUP_HEADS, HEAD_DIM))
                buf = caches[g][l]
                win_s[g].append(jnp.concatenate([buf[:, 1:], qkv_s5[:, :, 1:3, hs]], axis=1))
        else:
            j = l - n_a
            q_p = _proj(h_p, rows_p, mod_p, l * N_MOD + 3, norm3, l * 3 + 1, b_w_q, j, b_gain[j],
                        n_b_heads * HEAD_DIM, tn_w)
            q_s = _proj(h_s, rows_s, mod_s, l * N_MOD + 3, norm3, l * 3 + 1, b_w_q, j, b_gain[j],
                        n_b_heads * HEAD_DIM, tn_w)
            o_p = _sb_prompt(q_p, kv_p_bf16, b_sb_bias[j], batch, seq, _pick_tile(seq, 512), _pick_tile(seq, 512))
            h_p = _out_proj(o_p, b_w_o, j, h_p, rows_p, mod_p, l * N_MOD + 5, tn_w)
            if kv_pages is None:
                o_s, kv_pages = _sb_sample_gather(q_s, cache_kv, page_table, b_sb_bias[j], pages_per_step)
            else:
                o_s = _sb_sample(q_s, kv_pages, b_sb_bias[j], 2 * pages_per_step if n_pages % 8 == 0 else pages_per_step)
            h_s = _out_proj(o_s, b_w_o, j, h_s, rows_s, mod_s, l * N_MOD + 5, tn_w)
        h_p = ffn(h_p, paths[0], l, 1)
        h_s = ffn(h_s, paths[1], l, 1)
        if l == n_a - 1:
            kv_p, kv_p_bf16 = _proj(h_p, rows_p, modkv_p, 0, kvn3, 0, w_kv3, 0, kv_gain,
                                    n_b_heads * HEAD_DIM, tn_w, with_bf16=True)
            kv_s = _proj(h_s, rows_s, modkv_s, 0, kvn3, 0, w_kv3, 0, kv_gain, n_b_heads * HEAD_DIM, tn_w)

    y_p = h_p.reshape(batch, seq, d)
    y_s = h_s.reshape(db, 1, d)
    win_p = [jnp.stack(w, axis=0) for w in win_p]
    win_s = [jnp.stack(w, axis=0) for w in win_s]
    kv_p = kv_p.reshape(batch, seq, 2, n_b_heads, HEAD_DIM)
    kv_s = kv_s.reshape(db, 1, 2, n_b_heads, HEAD_DIM)
    return (y_p, y_s, win_p[0], win_p[1], win_p[2], kv_p, win_s[0], win_s[1], win_s[2], kv_s)
```

```python
import functools
import math

import numpy as np
import jax
import jax.numpy as jnp
from jax import lax
from jax.experimental import pallas as pl
from jax.experimental.pallas import tpu as pltpu

F32 = jnp.float32
BF16 = jnp.bfloat16

EPS = 1e-6
HEAD_DIM = 128
LANES = 128
N_MOD = 9
DIL_WINDOWS = (128, 512, 2048)
DIL_RATES = (1, 4, 16)
N_GROUPS = 3
GROUP_HEADS = 5
A_HEADS = N_GROUPS * GROUP_HEADS
A_BLOCK = 128
GROUP_COLS = GROUP_HEADS * HEAD_DIM
QKV_COLS = 3 * A_HEADS * HEAD_DIM
NEG_BIG = -1e30
NORM_CHUNK = 2 * LANES
MIB = 1024 * 1024
MOD_ROWS = 16


def _alibi_slopes():
    return [float(2.0 ** (-8.0 * (i + 1) / A_HEADS)) for i in range(A_HEADS)]


def _params(semantics, vmem_mib):
    return pltpu.CompilerParams(dimension_semantics=semantics, vmem_limit_bytes=vmem_mib * MIB)


def _pick_tile(n, preferred):
    if n <= preferred:
        return n
    t = (preferred // LANES) * LANES
    while t >= LANES:
        if n % t == 0:
            return t
        t -= LANES
    return n


class _Rows:
    def __init__(self, n_rows, groups, cond_rows, tm):
        self.n_rows = n_rows
        self.groups = groups
        self.cond_rows = cond_rows
        self.tm = tm
        self.tiles_per_group = (n_rows // groups) // tm
        assert self.tiles_per_group * tm * groups == n_rows

    def cond_spec(self, chunk, width, col_of=None):
        g, tpg = self.groups, self.tiles_per_group
        if col_of is None:
            return pl.BlockSpec((1, self.cond_rows, width), lambda i, j: (chunk * g + i // tpg, 0, 0))
        return pl.BlockSpec((1, self.cond_rows, width), lambda i, j: (chunk * g + i // tpg, 0, col_of(j)))


def _norm_mod(x, gain, shift, scale):
    ms = jnp.mean(x * x, axis=-1, keepdims=True)
    y = x * lax.rsqrt(ms + EPS) * gain
    return y * (1.0 + scale) + shift


def _mod_kernel(c_ref, w_ref, b_ref, o_ref):
    c = c_ref[...]
    a = (c * jax.nn.sigmoid(c)).astype(BF16)
    o_ref[0, 0] = jnp.dot(a, w_ref[0].astype(BF16), preferred_element_type=F32) + b_ref[0]


def _modulation(c_all, w, b):
    n_layers, d, n_out = w.shape
    n_chunks = n_out // d
    tn = _pick_tile(d, 1024)
    per_chunk = d // tn
    return pl.pallas_call(
        _mod_kernel,
        grid=(n_layers, n_out // tn),
        in_specs=[
            pl.BlockSpec((MOD_ROWS, d), lambda l, j: (0, 0)),
            pl.BlockSpec((1, d, tn), lambda l, j: (l, 0, j)),
            pl.BlockSpec((1, 1, tn), lambda l, j: (l, 0, j)),
        ],
        out_specs=pl.BlockSpec((1, 1, MOD_ROWS, tn), lambda l, j: (l, j // per_chunk, 0, j % per_chunk)),
        out_shape=jax.ShapeDtypeStruct((n_layers, n_chunks, MOD_ROWS, d), F32),
        compiler_params=_params(("arbitrary", "arbitrary"), 40),
        name="modulation",
    )(c_all, w, b.reshape(n_layers, 1, n_out))


def _ffn_kernel(x_ref, sh_ref, sc_ref, gt_ref, ng_ref, wg_ref, wu_ref, wo_ref, o_ref, xn_ref, *, row_chunks):
    f = pl.program_id(1)
    chunk = x_ref.shape[0] // row_chunks

    def swiglu(xn, wg, wu, wo):
        gate = jnp.dot(xn, wg, preferred_element_type=F32)
        up = jnp.dot(xn, wu, preferred_element_type=F32)
        act = (gate * jax.nn.sigmoid(gate) * up).astype(BF16)
        return jnp.dot(act, wo, preferred_element_type=F32)

    @pl.when(f == 0)
    def _():
        wg, wu, wo = wg_ref[...].astype(BF16), wu_ref[...].astype(BF16), wo_ref[...].astype(BF16)
        for c in range(row_chunks):
            rows = slice(c * chunk, (c + 1) * chunk)
            shift = sh_ref[0] if sh_ref.shape[1] == 1 else sh_ref[0, rows]
            scale = sc_ref[0] if sc_ref.shape[1] == 1 else sc_ref[0, rows]
            xn = _norm_mod(x_ref[rows], ng_ref[0], shift, scale).astype(BF16)
            xn_ref[rows] = xn
            o_ref[rows] = swiglu(xn, wg, wu, wo)

    @pl.when(f > 0)
    def _():
        o_ref[...] += swiglu(xn_ref[...], wg_ref[...].astype(BF16), wu_ref[...].astype(BF16),
                             wo_ref[...].astype(BF16))

    @pl.when(f == pl.num_programs(1) - 1)
    def _():
        o_ref[...] = x_ref[...] + 0.5 * gt_ref[0] * o_ref[...]


def _ffn(h, rows, mod, chunk0, norm_g, norm_idx, w_in, w_out, w_idx, tf):
    m, d = h.shape
    d_ff = w_out.shape[1]
    n_f = d_ff // tf
    tm = rows.tm
    row_chunks = 4 if tm % (4 * 2 * LANES) == 0 else 1
    return pl.pallas_call(
        functools.partial(_ffn_kernel, row_chunks=row_chunks),
        grid=(m // tm, n_f),
        in_specs=[
            pl.BlockSpec((tm, d), lambda i, f: (i, 0)),
            rows.cond_spec(chunk0, d),
            rows.cond_spec(chunk0 + 1, d),
            rows.cond_spec(chunk0 + 2, d),
            pl.BlockSpec((1, 1, d), lambda i, f: (norm_idx, 0, 0)),
            pl.BlockSpec((None, d, tf), lambda i, f: (w_idx, 0, f)),
            pl.BlockSpec((None, d, tf), lambda i, f: (w_idx, 0, n_f + f)),
            pl.BlockSpec((None, tf, d), lambda i, f: (w_idx, f, 0)),
        ],
        out_specs=pl.BlockSpec((tm, d), lambda i, f: (i, 0)),
        out_shape=jax.ShapeDtypeStruct((m, d), F32),
        scratch_shapes=[pltpu.VMEM((tm, d), BF16)],
        compiler_params=_params(("arbitrary", "arbitrary"), 58),
        name="ffn",
    )(h, mod, mod, mod, norm_g, w_in, w_in, w_out)


def _proj_kernel(x_ref, sh_ref, sc_ref, ng_ref, w_ref, hg_ref, *rest, n_norm_tiles, with_bf16, row_chunks):
    if with_bf16:
        o_ref, obf_ref, xn_ref = rest
    else:
        o_ref, xn_ref = rest
        obf_ref = None
    j = pl.program_id(1)
    tn = w_ref.shape[1]
    chunk = x_ref.shape[0] // row_chunks

    def store(val, rows, sl):
        o_ref[rows, sl] = val
        if obf_ref is not None:
            obf_ref[rows, sl] = val.astype(BF16)

    def normed_tile(xn, rows):
        for c0 in range(0, tn, NORM_CHUNK):
            c1 = min(c0 + NORM_CHUNK, tn)
            acc = jnp.dot(xn, w_ref[:, c0:c1].astype(BF16), preferred_element_type=F32)
            for t in range((c1 - c0) // HEAD_DIM):
                a = acc[:, t * HEAD_DIM:(t + 1) * HEAD_DIM]
                sl = slice(c0 + t * HEAD_DIM, c0 + (t + 1) * HEAD_DIM)
                ms = jnp.mean(a * a, axis=-1, keepdims=True)
                store(a * lax.rsqrt(ms + EPS) * hg_ref[:, sl], rows, sl)

    @pl.when(j == 0)
    def _():
        for c in range(row_chunks):
            rows = slice(c * chunk, (c + 1) * chunk)
            shift = sh_ref[0] if sh_ref.shape[1] == 1 else sh_ref[0, rows]
            scale = sc_ref[0] if sc_ref.shape[1] == 1 else sc_ref[0, rows]
            xn = _norm_mod(x_ref[rows], ng_ref[0], shift, scale).astype(BF16)
            xn_ref[rows] = xn
            normed_tile(xn, rows)

    @pl.when(jnp.logical_and(j > 0, j < n_norm_tiles))
    def _():
        normed_tile(xn_ref[...], slice(None))

    @pl.when(j >= n_norm_tiles)
    def _():
        store(jnp.dot(xn_ref[...], w_ref[...].astype(BF16), preferred_element_type=F32), slice(None), slice(None))


def _proj(h, rows, mod, chunk0, norm_g, norm_idx, w, w_idx, head_gain, n_norm_cols, tn, with_bf16=False):
    m, d = h.shape
    n = w.shape[-1]
    tm = rows.tm
    assert n % tn == 0 and n_norm_cols % tn == 0 and tn % HEAD_DIM == 0 and n_norm_cols >= tn
    row_chunks = 4 if tm % (4 * 2 * LANES) == 0 else 1
    out_shape = [jax.ShapeDtypeStruct((m, n), F32)]
    out_specs = [pl.BlockSpec((tm, tn), lambda i, j: (i, j))]
    if with_bf16:
        out_shape.append(jax.ShapeDtypeStruct((m, n), BF16))
        out_specs.append(pl.BlockSpec((tm, tn), lambda i, j: (i, j)))
    res = pl.pallas_call(
        functools.partial(_proj_kernel, n_norm_tiles=n_norm_cols // tn, with_bf16=with_bf16, row_chunks=row_chunks),
        grid=(m // tm, n // tn),
        in_specs=[
            pl.BlockSpec((tm, d), lambda i, j: (i, 0)),
            rows.cond_spec(chunk0, d),
            rows.cond_spec(chunk0 + 1, d),
            pl.BlockSpec((1, 1, d), lambda i, j: (norm_idx, 0, 0)),
            pl.BlockSpec((None, d, tn), lambda i, j: (w_idx, 0, j)),
            pl.BlockSpec((1, tn), lambda i, j: (0, j)),
        ],
        out_specs=out_specs,
        out_shape=out_shape,
        scratch_shapes=[pltpu.VMEM((tm, d), BF16)],
        compiler_params=_params(("arbitrary", "arbitrary"), 62),
        name="proj",
    )(h, mod, mod, norm_g, w, head_gain)
    return res if with_bf16 else res[0]


def _out_kernel(x_ref, w_ref, h_ref, gt_ref, o_ref):
    y = jnp.dot(x_ref[...].astype(BF16), w_ref[...].astype(BF16), preferred_element_type=F32)
    o_ref[...] = h_ref[...] + gt_ref[0] * y


def _out_proj(x, w, w_idx, h, rows, mod, chunk, tn):
    m, k = x.shape
    n = w.shape[-1]
    tm = rows.tm
    return pl.pallas_call(
        _out_kernel,
        grid=(m // tm, n // tn),
        in_specs=[
            pl.BlockSpec((tm, k), lambda i, j: (i, 0)),
            pl.BlockSpec((None, k, tn), lambda i, j: (w_idx, 0, j)),
            pl.BlockSpec((tm, tn), lambda i, j: (i, j)),
            rows.cond_spec(chunk, tn, col_of=lambda j: j),
        ],
        out_specs=pl.BlockSpec((tm, tn), lambda i, j: (i, j)),
        out_shape=jax.ShapeDtypeStruct((m, n), F32),
        compiler_params=_params(("arbitrary", "arbitrary"), 48),
        name="out_proj",
    )(x, w, h, mod)


def _dil_kernel(slopes_ref, q_ref, kc_ref, kp_ref, vc_ref, vp_ref, o_ref, lse_ref, *, dil, heads, head0):
    n = pl.program_id(1)
    hb = pl.program_id(2)
    scale = 1.0 / math.sqrt(HEAD_DIM)
    qi = lax.broadcasted_iota(jnp.int32, (A_BLOCK, A_BLOCK), 0)
    ki = lax.broadcasted_iota(jnp.int32, (A_BLOCK, A_BLOCK), 1)
    step_c = qi - ki
    step_p = step_c + A_BLOCK
    valid_c = step_c >= 0
    valid_p = jnp.logical_and(step_p <= A_BLOCK, n > 0)
    dist_c = (step_c * dil).astype(F32)
    dist_p = (step_p * dil).astype(F32)
    nt = (((1,), (1,)), ((), ()))
    for hh in range(heads):
        slope = slopes_ref[head0 + hb * heads + hh]
        bias_c = jnp.where(valid_c, -slope * dist_c, NEG_BIG)
        bias_p = jnp.where(valid_p, -slope * dist_p, NEG_BIG)
        sl = slice(hh * HEAD_DIM, (hh + 1) * HEAD_DIM)
        for res in range(dil):
            rows = pl.ds(res, A_BLOCK, stride=dil) if dil > 1 else slice(None)
            q = (q_ref[rows, sl] * scale).astype(BF16)
            s_c = lax.dot_general(q, kc_ref[rows, sl].astype(BF16), nt, preferred_element_type=F32) + bias_c
            s_p = lax.dot_general(q, kp_ref[rows, sl].astype(BF16), nt, preferred_element_type=F32) + bias_p
            mx = jnp.maximum(jnp.max(s_c, axis=-1, keepdims=True), jnp.max(s_p, axis=-1, keepdims=True))
            p_c = jnp.exp(s_c - mx)
            p_p = jnp.exp(s_p - mx)
            den = jnp.sum(p_c, axis=-1, keepdims=True) + jnp.sum(p_p, axis=-1, keepdims=True)
            o = jnp.dot(p_c.astype(BF16), vc_ref[rows, sl].astype(BF16), preferred_element_type=F32)
            o = o + jnp.dot(p_p.astype(BF16), vp_ref[rows, sl].astype(BF16), preferred_element_type=F32)
            o_ref[rows, sl] = o / den
            lse_ref[rows, sl] = jnp.broadcast_to(mx + jnp.log(den), (A_BLOCK, HEAD_DIM))


def _dilated_prompt(qkv, slopes, batch, seq, g, heads):
    dil = DIL_RATES[g]
    span = dil * A_BLOCK
    assert DIL_WINDOWS[g] // dil == A_BLOCK and seq % span == 0 and GROUP_HEADS % heads == 0
    nb = seq // span
    cw = heads * HEAD_DIM
    q_col = g * GROUP_COLS // cw
    k_col = (A_HEADS * HEAD_DIM + g * GROUP_COLS) // cw
    v_col = (2 * A_HEADS * HEAD_DIM + g * GROUP_COLS) // cw
    blk = (span, cw)

    def cur(col):
        return pl.BlockSpec(blk, lambda b, n, hb: (b * nb + n, col + hb))

    def prev(col):
        return pl.BlockSpec(blk, lambda b, n, hb: (b * nb + jnp.maximum(n - 1, 0), col + hb))

    out_spec = pl.BlockSpec(blk, lambda b, n, hb: (b * nb + n, hb))
    return pl.pallas_call(
        functools.partial(_dil_kernel, dil=dil, heads=heads, head0=g * GROUP_HEADS),
        grid=(batch, nb, GROUP_HEADS // heads),
        in_specs=[pl.BlockSpec(memory_space=pltpu.SMEM), cur(q_col), cur(k_col), prev(k_col), cur(v_col), prev(v_col)],
        out_specs=[out_spec, out_spec],
        out_shape=[jax.ShapeDtypeStruct((batch * seq, GROUP_COLS), F32)] * 2,
        compiler_params=_params(("arbitrary", "arbitrary", "arbitrary"), 40),
        name="dilated_prompt",
    )(slopes, qkv, qkv, qkv, qkv, qkv)


def _mix_out_kernel(o0_ref, o1_ref, o2_ref, l0_ref, l1_ref, l2_ref, w_ref, h_ref, gt_ref, out_ref, mix_ref):
    j = pl.program_id(1)

    @pl.when(j == 0)
    def _():
        ls = []
        for l_ref in (l0_ref, l1_ref, l2_ref):
            heads = [l_ref[:, hh * HEAD_DIM:(hh + 1) * HEAD_DIM] for hh in range(GROUP_HEADS)]
            htop = functools.reduce(jnp.maximum, heads)
            htot = functools.reduce(lambda a, b: a + b, [jnp.exp(l - htop) for l in heads])
            ls.append(htop + jnp.log(htot) - math.log(GROUP_HEADS))
        top = jnp.maximum(jnp.maximum(ls[0], ls[1]), ls[2])
        es = [jnp.exp(l - top) for l in ls]
        den = es[0] + es[1] + es[2]
        for g, o_ref in enumerate((o0_ref, o1_ref, o2_ref)):
            alpha = es[g] / den
            for hh in range(GROUP_HEADS):
                src = slice(hh * HEAD_DIM, (hh + 1) * HEAD_DIM)
                dst = slice(g * GROUP_COLS + hh * HEAD_DIM, g * GROUP_COLS + (hh + 1) * HEAD_DIM)
                mix_ref[:, dst] = (o_ref[:, src] * alpha).astype(BF16)

    y = jnp.dot(mix_ref[...], w_ref[...].astype(BF16), preferred_element_type=F32)
    out_ref[...] = h_ref[...] + gt_ref[0] * y


def _mix_out(outs, lses, w, w_idx, h, rows, mod, chunk, tn):
    m = h.shape[0]
    n = w.shape[-1]
    k = w.shape[-2]
    tm = rows.tm
    o_spec = pl.BlockSpec((tm, GROUP_COLS), lambda i, j: (i, 0))
    return pl.pallas_call(
        _mix_out_kernel,
        grid=(m // tm, n // tn),
        in_specs=[
            o_spec, o_spec, o_spec, o_spec, o_spec, o_spec,
            pl.BlockSpec((None, k, tn), lambda i, j: (w_idx, 0, j)),
            pl.BlockSpec((tm, tn), lambda i, j: (i, j)),
            rows.cond_spec(chunk, tn, col_of=lambda j: j),
        ],
        out_specs=pl.BlockSpec((tm, tn), lambda i, j: (i, j)),
        out_shape=jax.ShapeDtypeStruct((m, n), F32),
        scratch_shapes=[pltpu.VMEM((tm, k), BF16)],
        compiler_params=_params(("arbitrary", "arbitrary"), 58),
        name="mix_out",
    )(*outs, *lses, w, h, mod)


def _dil_sample_kernel(qkv_ref, c0_ref, c1_ref, c2_ref, o_ref, *, slopes):
    scale = 1.0 / math.sqrt(HEAD_DIM)
    nt = (((1,), (1,)), ((), ()))
    kbase = A_HEADS * HEAD_DIM
    vbase = 2 * A_HEADS * HEAD_DIM
    steps = (A_BLOCK - lax.broadcasted_iota(jnp.int32, (1, A_BLOCK), 1)).astype(F32)
    outs, glses = [], []
    for g, c_ref in enumerate((c0_ref, c1_ref, c2_ref)):
        dil = DIL_RATES[g]
        head_outs, head_lses = [], []
        for hh in range(GROUP_HEADS):
            col = (g * GROUP_HEADS + hh) * HEAD_DIM
            q = qkv_ref[0, :, col:col + HEAD_DIM]
            k_new = qkv_ref[0, :, kbase + col:kbase + col + HEAD_DIM]
            v_new = qkv_ref[0, :, vbase + col:vbase + col + HEAD_DIM]
            k_buf = c_ref[0, 0, :, hh * HEAD_DIM:(hh + 1) * HEAD_DIM]
            v_buf = c_ref[0, 0, :, GROUP_COLS + hh * HEAD_DIM:GROUP_COLS + (hh + 1) * HEAD_DIM]
            q8 = jnp.broadcast_to(q * scale, (8, HEAD_DIM)).astype(BF16)
            s_buf = lax.dot_general(q8, k_buf.astype(BF16), nt, preferred_element_type=F32)[0:1]
            s_buf = s_buf - slopes[g * GROUP_HEADS + hh] * dil * steps
            s_new = jnp.sum((q * scale).astype(BF16).astype(F32) * k_new.astype(BF16).astype(F32),
                            axis=-1, keepdims=True)
            mx = jnp.maximum(jnp.max(s_buf, axis=-1, keepdims=True), s_new)
            p_buf = jnp.exp(s_buf - mx)
            p_new = jnp.exp(s_new - mx)
            den = jnp.sum(p_buf, axis=-1, keepdims=True) + p_new
            p8 = jnp.broadcast_to(p_buf, (8, A_BLOCK)).astype(BF16)
            o = jnp.dot(p8, v_buf.astype(BF16), preferred_element_type=F32)[0:1]
            o = (o + p_new.astype(BF16).astype(F32) * v_new.astype(BF16).astype(F32)) / den
            head_outs.append(o)
            head_lses.append(mx + jnp.log(den))
        top = functools.reduce(jnp.maximum, head_lses)
        tot = functools.reduce(lambda a, b: a + b, [jnp.exp(l - top) for l in head_lses])
        glses.append(top + jnp.log(tot) - math.log(GROUP_HEADS))
        outs.append(head_outs)
    top = functools.reduce(jnp.maximum, glses)
    es = [jnp.exp(l - top) for l in glses]
    den = es[0] + es[1] + es[2]
    for g in range(N_GROUPS):
        alpha = es[g] / den
        for hh in range(GROUP_HEADS):
            col = (g * GROUP_HEADS + hh) * HEAD_DIM
            o_ref[0, :, col:col + HEAD_DIM] = outs[g][hh] * alpha


def _dilated_sample(qkv_s, caches, layer):
    db = qkv_s.shape[0]
    row_cols = 2 * GROUP_COLS
    views, specs = [], []
    for g, c in enumerate(caches):
        dil = DIL_RATES[g]
        assert c.shape[2] == DIL_WINDOWS[g] and c.shape[2] // dil == A_BLOCK
        views.append(c[:, :, ::dil].reshape(c.shape[0], db, A_BLOCK, row_cols))
        specs.append(pl.BlockSpec((1, 1, A_BLOCK, row_cols), lambda b: (layer, b, 0, 0)))
    out = pl.pallas_call(
        functools.partial(_dil_sample_kernel, slopes=tuple(_alibi_slopes())),
        grid=(db,),
        in_specs=[pl.BlockSpec((1, 1, QKV_COLS), lambda b: (b, 0, 0))] + specs,
        out_specs=pl.BlockSpec((1, 1, A_HEADS * HEAD_DIM), lambda b: (b, 0, 0)),
        out_shape=jax.ShapeDtypeStruct((db, 1, A_HEADS * HEAD_DIM), F32),
        compiler_params=_params(("arbitrary",), 32),
        name="dilated_sample",
    )(qkv_s.reshape(db, 1, QKV_COLS), *views)
    return out.reshape(db, A_HEADS * HEAD_DIM)


def _stick_terms(z):
    lo = jnp.minimum(z, 0.0)
    hi = jnp.maximum(z, 0.0)
    l = jnp.log(1.0 + jnp.exp(lo - hi))
    return lo - l, hi + l


def _sb_prompt_kernel(bias_ref, q_ref, k_ref, v_ref, o_ref, *, tq, tk, sub, hpb):
    hb = pl.program_id(1)
    i = pl.program_id(2)
    n_sub = tk // sub
    biases = [bias_ref[hb * hpb + hh] for hh in range(hpb)]
    heads = [slice(hh * HEAD_DIM, (hh + 1) * HEAD_DIM) for hh in range(hpb)]
    qs = [(q_ref[0, :, sl] * (1.0 / math.sqrt(HEAD_DIM))).astype(BF16) for sl in heads]
    nt = (((1,), (1,)), ((), ()))
    r = lax.broadcasted_iota(jnp.int32, (sub, sub), 0)
    c = lax.broadcasted_iota(jnp.int32, (sub, sub), 1)
    later_mat = jnp.where(r > c, 1.0, 0.0).astype(BF16)
    q_pos = lax.broadcasted_iota(jnp.int32, (tq, sub), 0)
    if tq == tk:
        n_full = i
    else:
        n_full = (i * tq) // tk
        q_pos = q_pos + (i * tq - n_full * tk)
    k_off = lax.broadcasted_iota(jnp.int32, (tq, sub), 1)

    def block(start, state, masked):
        return tuple(head_block(start, hh, state[hh][0], state[hh][1], masked) for hh in range(hpb))

    def head_block(start, hh, carry, acc, masked):
        k = k_ref[0, pl.ds(start, tk), heads[hh]]
        v = v_ref[0, pl.ds(start, tk), heads[hh]]
        z_all = lax.dot_general(qs[hh], k, nt, preferred_element_type=F32) + biases[hh]
        ws = [None] * n_sub
        for cidx in reversed(range(n_sub)):
            log_take, cost = _stick_terms(z_all[:, cidx * sub:(cidx + 1) * sub])
            if masked:
                causal = (k_off + cidx * sub) < q_pos
                cost = jnp.where(causal, cost, 0.0)
            later = jnp.dot(cost.astype(BF16), later_mat, preferred_element_type=F32)
            w = jnp.exp(log_take - later - carry)
            if masked:
                w = jnp.where(causal, w, 0.0)
            ws[cidx] = w.astype(BF16)
            carry = carry + jnp.sum(cost, axis=-1, keepdims=True)
        acc = acc + jnp.dot(jnp.concatenate(ws, axis=1), v, preferred_element_type=F32)
        return carry, acc

    zeros = tuple((jnp.zeros((tq, 1), F32), jnp.zeros((tq, HEAD_DIM), F32)) for _ in range(hpb))
    state = block(pl.multiple_of(n_full * tk, tk), zeros, True)

    def body(t, state):
        return block(pl.multiple_of((n_full - 1 - t) * tk, tk), state, False)

    state = lax.fori_loop(0, n_full, body, state)
    for hh in range(hpb):
        o_ref[0, :, heads[hh]] = state[hh][1].astype(o_ref.dtype)


def _sb_prompt(q, kv_bf16, sb_bias, batch, seq, tq, tk):
    n_heads = q.shape[1] // HEAD_DIM
    hpb = 4 if n_heads % 4 == 0 else 1
    n_hb = n_heads // hpb
    assert tk % tq == 0 and seq % tk == 0
    q3 = q.reshape(batch, seq, n_heads * HEAD_DIM)
    kv3 = kv_bf16.reshape(batch, seq, 2 * n_heads * HEAD_DIM)
    out = pl.pallas_call(
        functools.partial(_sb_prompt_kernel, tq=tq, tk=tk, sub=min(tk, 2 * LANES), hpb=hpb),
        grid=(batch, n_hb, seq // tq),
        in_specs=[
            pl.BlockSpec(memory_space=pltpu.SMEM),
            pl.BlockSpec((1, tq, hpb * HEAD_DIM), lambda b, h, i: (b, i, h)),
            pl.BlockSpec((1, seq, hpb * HEAD_DIM), lambda b, h, i: (b, 0, h)),
            pl.BlockSpec((1, seq, hpb * HEAD_DIM), lambda b, h, i: (b, 0, n_hb + h)),
        ],
        out_specs=pl.BlockSpec((1, tq, hpb * HEAD_DIM), lambda b, h, i: (b, i, h)),
        out_shape=jax.ShapeDtypeStruct((batch, seq, n_heads * HEAD_DIM), BF16),
        compiler_params=_params(("arbitrary", "arbitrary", "arbitrary"), 40),
        name="sb_prompt",
    )(sb_bias, q3, kv3, kv3)
    return out.reshape(batch * seq, n_heads * HEAD_DIM)


def _relayout_page(x_ref, o_ref, t, n_heads):
    page = o_ref.shape[1]
    keys = 2 * LANES // n_heads
    n = keys * n_heads
    i = lax.broadcasted_iota(jnp.int32, (n, n), 0)
    j = lax.broadcasted_iota(jnp.int32, (n, n), 1)
    perm = jnp.where(jnp.logical_and(i // keys == j % n_heads, i % keys == j // n_heads), 1.0, 0.0).astype(BF16)
    for c in range(page // keys):
        k_rows = x_ref[c * keys:(c + 1) * keys, 0:n_heads, :].reshape(n, HEAD_DIM)
        v_rows = x_ref[c * keys:(c + 1) * keys, n_heads:2 * n_heads, :].reshape(n, HEAD_DIM)
        both = jnp.concatenate([k_rows, v_rows], axis=1).astype(BF16)
        y = jnp.dot(perm, both, preferred_element_type=F32)
        for hh in range(n_heads):
            blk = y[hh * keys:(hh + 1) * keys].astype(BF16)
            o_ref[t, c * keys:(c + 1) * keys, hh * HEAD_DIM:(hh + 1) * HEAD_DIM] = blk[:, :HEAD_DIM]
            o_ref[t, c * keys:(c + 1) * keys, (n_heads + hh) * HEAD_DIM:(n_heads + hh + 1) * HEAD_DIM] = (
                blk[:, HEAD_DIM:])


def _sb_sample_step(qmat_ref, bias_ref, kv_ref, o_ref, acc_ref, carry_ref, n_heads):
    p = pl.program_id(1)
    hd = n_heads * HEAD_DIM
    pages_per_step, page = kv_ref.shape[0], kv_ref.shape[1]

    @pl.when(p == 0)
    def _():
        acc_ref[...] = jnp.zeros_like(acc_ref)
        carry_ref[...] = jnp.zeros_like(carry_ref)

    r = lax.broadcasted_iota(jnp.int32, (page, page), 0)
    c = lax.broadcasted_iota(jnp.int32, (page, page), 1)
    later_mat = jnp.where(c > r, 1.0, 0.0).astype(BF16)
    k_all = kv_ref[:, :, :hd].reshape(pages_per_step * page, hd)
    half = hd // 2
    z = (jnp.dot(k_all[:, :half], qmat_ref[0, :half, :], preferred_element_type=F32)
         + jnp.dot(k_all[:, half:], qmat_ref[0, half:, :], preferred_element_type=F32) + bias_ref[...])
    log_take, cost = _stick_terms(z)
    carry = carry_ref[...]
    ws = [None] * pages_per_step
    for t in reversed(range(pages_per_step)):
        rows = slice(t * page, (t + 1) * page)
        later = jnp.dot(later_mat, cost[rows].astype(BF16), preferred_element_type=F32)
        w = jnp.exp(log_take[rows] - later - carry)
        ws[t] = w.T[:MOD_ROWS].astype(BF16)
        carry = carry + jnp.sum(cost[rows], axis=0, keepdims=True)
    carry_ref[...] = carry
    w_all = jnp.concatenate(ws, axis=1)
    v_all = kv_ref[:, :, hd:].reshape(pages_per_step * page, hd)
    acc_ref[...] += jnp.dot(w_all, v_all, preferred_element_type=F32)

    @pl.when(p == pl.num_programs(1) - 1)
    def _():
        for hh in range(n_heads):
            sl = slice(hh * HEAD_DIM, (hh + 1) * HEAD_DIM)
            o_ref[0, :, sl] = acc_ref[hh:hh + 1, sl]


def _sb_sample_kernel(qmat_ref, bias_ref, kv_ref, o_ref, acc_ref, carry_ref, *, n_heads):
    _sb_sample_step(qmat_ref, bias_ref, kv_ref, o_ref, acc_ref, carry_ref, n_heads)


def _sb_sample_gather_kernel(pt_ref, qmat_ref, bias_ref, *refs, n_heads, pages_per_step):
    x_refs = refs[:pages_per_step]
    o_ref, kv_ref, acc_ref, carry_ref = refs[pages_per_step:]
    for t, x_ref in enumerate(x_refs):
        _relayout_page(x_ref, kv_ref, t, n_heads)
    _sb_sample_step(qmat_ref, bias_ref, kv_ref, o_ref, acc_ref, carry_ref, n_heads)


def _sb_sample_operands(q, sb_bias):
    hd = q.shape[1]
    n_heads = hd // HEAD_DIM
    assert n_heads <= MOD_ROWS
    head_of_row = jnp.arange(hd, dtype=jnp.int32) // HEAD_DIM
    sel = (head_of_row[:, None] == jnp.arange(LANES, dtype=jnp.int32)[None, :]).astype(F32)
    qmat = ((q * (1.0 / math.sqrt(HEAD_DIM)))[:, :, None] * sel[None]).astype(BF16)
    bias_row = jnp.zeros((1, LANES), F32).at[0, :n_heads].set(sb_bias.astype(F32))
    return qmat, bias_row


def _sb_sample_gather(q, cache_kv, page_table, sb_bias, pages_per_step):
    db, hd = q.shape
    n_phys, page, _, n_heads, _ = cache_kv.shape
    n_pages = page_table.shape[1]
    slabs = 2 * n_heads
    assert (2 * LANES) % n_heads == 0 and page % (2 * LANES // n_heads) == 0 and n_heads % 8 == 0
    steps = n_pages // pages_per_step
    qmat, bias_row = _sb_sample_operands(q, sb_bias)
    rows = cache_kv.reshape(n_phys, page, slabs, HEAD_DIM)

    def page_spec(t):
        return pl.BlockSpec((None, page, slabs, HEAD_DIM),
                            lambda b, p, pt: (pt[b, (steps - 1 - p) * pages_per_step + t], 0, 0, 0))

    out, kv_pages = pl.pallas_call(
        functools.partial(_sb_sample_gather_kernel, n_heads=n_heads, pages_per_step=pages_per_step),
        grid_spec=pltpu.PrefetchScalarGridSpec(
            num_scalar_prefetch=1,
            grid=(db, steps),
            in_specs=[
                pl.BlockSpec((1, hd, LANES), lambda b, p, pt: (b, 0, 0)),
                pl.BlockSpec((1, LANES), lambda b, p, pt: (0, 0)),
            ] + [page_spec(t) for t in range(pages_per_step)],
            out_specs=[
                pl.BlockSpec((1, 1, hd), lambda b, p, pt: (b, 0, 0)),
                pl.BlockSpec((pages_per_step, page, 2 * hd), lambda b, p, pt: (b * steps + steps - 1 - p, 0, 0)),
            ],
            scratch_shapes=[pltpu.VMEM((MOD_ROWS, hd), F32), pltpu.VMEM((1, LANES), F32)],
        ),
        out_shape=[jax.ShapeDtypeStruct((db, 1, hd), F32),
                   jax.ShapeDtypeStruct((db * n_pages, page, 2 * hd), BF16)],
        compiler_params=_params(("arbitrary", "arbitrary"), 60),
        name="sb_sample_gather",
    )(page_table, qmat, bias_row, *([rows] * pages_per_step))
    return out.reshape(db, hd), kv_pages


def _sb_sample(q, kv_pages, sb_bias, pages_per_step):
    db, hd = q.shape
    n_heads = hd // HEAD_DIM
    page = kv_pages.shape[1]
    steps = kv_pages.shape[0] // (db * pages_per_step)
    qmat, bias_row = _sb_sample_operands(q, sb_bias)
    out = pl.pallas_call(
        functools.partial(_sb_sample_kernel, n_heads=n_heads),
        grid=(db, steps),
        in_specs=[
            pl.BlockSpec((1, hd, LANES), lambda b, p: (b, 0, 0)),
            pl.BlockSpec((1, LANES), lambda b, p: (0, 0)),
            pl.BlockSpec((pages_per_step, page, 2 * hd), lambda b, p: (b * steps + steps - 1 - p, 0, 0)),
        ],
        out_specs=pl.BlockSpec((1, 1, hd), lambda b, p: (b, 0, 0)),
        out_shape=jax.ShapeDtypeStruct((db, 1, hd), F32),
        scratch_shapes=[pltpu.VMEM((MOD_ROWS, hd), F32), pltpu.VMEM((1, LANES), F32)],
        compiler_params=_params(("arbitrary", "arbitrary"), 40),
        name="sb_sample",
    )(qmat, bias_row, kv_pages)
    return out.reshape(db, hd)


def kernel(x_prompt, x_sample, cache_win_g0, cache_win_g1, cache_win_g2, cache_kv, page_table, c_prompt, c_sample, w_mod, b_mod, norm_g, ffn_w_in, ffn_w_out, a_w_qkv, a_q_norm, a_k_norm, a_w_o, kv_norm, w_mod_kv, b_mod_kv, w_kv, sb_k_norm, b_w_q, b_q_norm, b_sb_bias, b_w_o):
    batch, seq, d = x_prompt.shape
    db, ds, _ = x_sample.shape
    depth = w_mod.shape[0]
    n_a = a_w_qkv.shape[0]
    d_ff = ffn_w_out.shape[2]
    n_b_heads = w_kv.shape[1] // (2 * HEAD_DIM)
    assert ds == 1 and db == 8 and db + batch <= MOD_ROWS
    caches = (cache_win_g0, cache_win_g1, cache_win_g2)

    c_all = jnp.zeros((MOD_ROWS, d), F32).at[:db].set(c_sample).at[db:db + batch].set(c_prompt)
    mod = _modulation(c_all, w_mod, b_mod)
    mod_kv = _modulation(c_all, w_mod_kv[None], b_mod_kv[None])
    mod_s = mod[:, :, :db].reshape(depth * N_MOD, db, d)
    mod_p = mod[:, :, db:db + batch].reshape(depth * N_MOD * batch, 1, d)
    modkv_s = mod_kv[:, :, :db].reshape(2, db, d)
    modkv_p = mod_kv[:, :, db:db + batch].reshape(2 * batch, 1, d)

    tm_p = _pick_tile(seq, 1024)
    rows_p = _Rows(batch * seq, batch, 1, tm_p)
    rows_s = _Rows(db, 1, db, db)
    paths = (
        dict(rows=rows_p, mod=mod_p, modkv=modkv_p, tf=_pick_tile(d_ff, 256)),
        dict(rows=rows_s, mod=mod_s, modkv=modkv_s, tf=_pick_tile(d_ff, 512)),
    )

    norm3 = norm_g.reshape(depth * 3, 1, d)
    kvn3 = kv_norm.reshape(1, 1, d)
    w_in = ffn_w_in.reshape(depth * 2, d, 2 * d_ff)
    w_out = ffn_w_out.reshape(depth * 2, d_ff, d)
    w_kv3 = w_kv[None]

    a_gain = [jnp.concatenate([jnp.tile(a_q_norm[l], A_HEADS), jnp.tile(a_k_norm[l], A_HEADS),
                               jnp.ones((A_HEADS * HEAD_DIM,), F32)])[None] for l in range(n_a)]
    kv_gain = jnp.concatenate([jnp.tile(sb_k_norm, n_b_heads), jnp.ones((n_b_heads * HEAD_DIM,), F32)])[None]
    b_gain = [jnp.tile(b_q_norm[j], n_b_heads)[None] for j in range(depth - n_a)]

    h_p = x_prompt.reshape(batch * seq, d)
    h_s = x_sample.reshape(db, d)
    tn_qkv = GROUP_COLS
    tn_d = _pick_tile(d, 512)
    tn_w = _pick_tile(d, 1024)
    win_p = [[] for _ in range(N_GROUPS)]
    win_s = [[] for _ in range(N_GROUPS)]
    kv_p = kv_s = kv_p_bf16 = None
    n_pages = page_table.shape[1]
    pages_per_step = 4 if n_pages % 4 == 0 else 1
    kv_pages = None
    slopes = jnp.asarray(_alibi_slopes(), F32)

    def ffn(h, path, l, which):
        return _ffn(h, path["rows"], path["mod"], (l * N_MOD + 6 * which), norm3, l * 3 + 2 * which,
                    w_in, w_out, l * 2 + which, path["tf"])

    for l in range(depth):
        h_p = ffn(h_p, paths[0], l, 0)
        h_s = ffn(h_s, paths[1], l, 0)
        if l < n_a:
            qkv_p = _proj(h_p, rows_p, mod_p, l * N_MOD + 3, norm3, l * 3 + 1, a_w_qkv, l, a_gain[l],
                          2 * A_HEADS * HEAD_DIM, tn_qkv)
            qkv_s = _proj(h_s, rows_s, mod_s, l * N_MOD + 3, norm3, l * 3 + 1, a_w_qkv, l, a_gain[l],
                          2 * A_HEADS * HEAD_DIM, tn_qkv)
            outs, lses = [], []
            for g in range(N_GROUPS):
                o_g, lse_g = _dilated_prompt(qkv_p, slopes, batch, seq, g, GROUP_HEADS if DIL_RATES[g] == 1 else 1)
                outs.append(o_g)
                lses.append(lse_g)
            h_p = _mix_out(outs, lses, a_w_o, l, h_p, rows_p, mod_p, l * N_MOD + 5, tn_d)
            mixed_s = _dilated_sample(qkv_s, caches, l)
            h_s = _out_proj(mixed_s, a_w_o, l, h_s, rows_s, mod_s, l * N_MOD + 5, tn_w)

            qkv_p3 = qkv_p.reshape(batch, seq, QKV_COLS)
            qkv_s5 = qkv_s.reshape(db, 1, 3, A_HEADS, HEAD_DIM)
            for g in range(N_GROUPS):
                hs = slice(g * GROUP_HEADS, (g + 1) * GROUP_HEADS)
                keep = min(DIL_WINDOWS[g], seq)
                kv_rows = [qkv_p3[:, seq - keep:, part * A_HEADS * HEAD_DIM + g * GROUP_COLS:
                                  part * A_HEADS * HEAD_DIM + (g + 1) * GROUP_COLS] for part in (1, 2)]
                win_p[g].append(jnp.stack(kv_rows, axis=2).reshape(batch, keep, 2, GROUP_HEADS, HEAD_DIM))
                buf = caches[g][l]
                win_s[g].append(jnp.concatenate([buf[:, 1:], qkv_s5[:, :, 1:3, hs]], axis=1))
        else:
            j = l - n_a
            q_p = _proj(h_p, rows_p, mod_p, l * N_MOD + 3, norm3, l * 3 + 1, b_w_q, j, b_gain[j],
                        n_b_heads * HEAD_DIM, tn_w)
            q_s = _proj(h_s, rows_s, mod_s, l * N_MOD + 3, norm3, l * 3 + 1, b_w_q, j, b_gain[j],
                        n_b_heads * HEAD_DIM, tn_w)
            o_p = _sb_prompt(q_p, kv_p_bf16, b_sb_bias[j], batch, seq, _pick_tile(seq, 512), _pick_tile(seq, 512))
            h_p = _out_proj(o_p, b_w_o, j, h_p, rows_p, mod_p, l * N_MOD + 5, tn_w)
            if kv_pages is None:
                o_s, kv_pages = _sb_sample_gather(q_s, cache_kv, page_table, b_sb_bias[j],
                                                  2 * pages_per_step if n_pages % 8 == 0 else pages_per_step)
            else:
                o_s = _sb_sample(q_s, kv_pages, b_sb_bias[j], 2 * pages_per_step if n_pages % 8 == 0 else pages_per_step)
            h_s = _out_proj(o_s, b_w_o, j, h_s, rows_s, mod_s, l * N_MOD + 5, tn_w)
        h_p = ffn(h_p, paths[0], l, 1)
        h_s = ffn(h_s, paths[1], l, 1)
        if l == n_a - 1:
            kv_p, kv_p_bf16 = _proj(h_p, rows_p, modkv_p, 0, kvn3, 0, w_kv3, 0, kv_gain,
                                    n_b_heads * HEAD_DIM, tn_w, with_bf16=True)
            kv_s = _proj(h_s, rows_s, modkv_s, 0, kvn3, 0, w_kv3, 0, kv_gain, n_b_heads * HEAD_DIM, tn_w)

    y_p = h_p.reshape(batch, seq, d)
    y_s = h_s.reshape(db, 1, d)
    win_p = [jnp.stack(w, axis=0) for w in win_p]
    win_s = [jnp.stack(w, axis=0) for w in win_s]
    kv_p = kv_p.reshape(batch, seq, 2, n_b_heads, HEAD_DIM)
    kv_s = kv_s.reshape(db, 1, 2, n_b_heads, HEAD_DIM)
    return (y_p, y_s, win_p[0], win_p[1], win_p[2], kv_p, win_s[0], win_s[1], win_s[2], kv_s)
```
